```python
import math
import jax, jax.numpy as jnp
from jax import lax
import numpy as np

D_MODEL = 1024
BATCH = 2
SEQ = 8192
DEPTH = 1

CHUNK = 64
D_PLE = 256
D_FF = 2816
RET_HEADS = 8
RET_DK = 128
RET_DV = 256
RET_QK = RET_HEADS * RET_DK
RET_V = RET_HEADS * RET_DV
MLA_HEADS = 8
MLA_NOPE = 128
MLA_ROPE = 64
MLA_DV = 128
Q_LORA = 256
KV_LORA = 256
Q_BLOCK = 128
ROPE_BASE = 10000.0
EPS = 1e-5
N_LN = 4
DEEPNORM_ALPHA = (2.0 * DEPTH) ** 0.25
DEEPNORM_BETA = (8.0 * DEPTH) ** -0.25
SPLITS = (RET_QK, RET_QK, RET_V, RET_V, Q_LORA, KV_LORA, MLA_ROPE, D_MODEL, D_MODEL)
D_IN_TOTAL = sum(SPLITS)

kernel_name = "hybrid_retention_mla_macaron_deepnorm"


def layer_norm(x, g, b):
    xf = x.astype(jnp.float32)
    mu = jnp.mean(xf, axis=-1, keepdims=True)
    var = jnp.mean(jnp.square(xf - mu), axis=-1, keepdims=True)
    y = (xf - mu) * lax.rsqrt(var + EPS)
    return (y * g.astype(jnp.float32) + b.astype(jnp.float32)).astype(x.dtype)


def rms_norm(x, g):
    xf = x.astype(jnp.float32)
    y = xf * lax.rsqrt(jnp.mean(jnp.square(xf), axis=-1, keepdims=True) + EPS)
    return (y * g.astype(jnp.float32)).astype(x.dtype)


def rope(t, positions):
    half = t.shape[-1] // 2
    inv_freq = ROPE_BASE ** (-jnp.arange(half, dtype=jnp.float32) / half)
    ang = positions.astype(jnp.float32)[:, :, None] * inv_freq
    cos = jnp.cos(ang)[:, :, None, :]
    sin = jnp.sin(ang)[:, :, None, :]
    t1 = t[..., :half].astype(jnp.float32)
    t2 = t[..., half:].astype(jnp.float32)
    return jnp.concatenate([t1 * cos - t2 * sin, t2 * cos + t1 * sin], axis=-1).astype(t.dtype)


def swiglu_ffn(x, w_in, w_out):
    g, u = jnp.split(x @ w_in, 2, axis=-1)
    return (jax.nn.silu(g) * u) @ w_out


def chunk_retention(q, k, v):
    B, S, H, dk = q.shape
    dv = v.shape[-1]
    n_chunks = S // CHUNK
    log_gamma = jnp.log(1.0 - 2.0 ** (-5.0 - jnp.arange(H, dtype=jnp.float32)))
    idx = jnp.arange(CHUNK, dtype=jnp.float32)
    intra_decay = jnp.exp(log_gamma[:, None, None] * jnp.abs(idx[:, None] - idx[None, :]))
    xi = jnp.exp(log_gamma[:, None] * (idx + 1.0))
    zeta = jnp.exp(log_gamma[:, None] * (CHUNK - 1.0 - idx))
    chunk_decay = jnp.exp(log_gamma * CHUNK)

    qc = q.reshape(B, n_chunks, CHUNK, H, dk)
    kc = k.reshape(B, n_chunks, CHUNK, H, dk)
    vc = v.reshape(B, n_chunks, CHUNK, H, dv)

    scores = jnp.einsum('bnchd,bnshd->bnhcs', qc, kc) * intra_decay
    y_intra = jnp.einsum('bnhcs,bnshe->bnche', scores, vc)

    xi_ch = xi.T[None, :, :, None]

    def step(state, inp):
        q_n, k_n, v_n = inp
        cross = jnp.einsum('bchd,bhde->bche', q_n, state) * xi_ch
        state = state * chunk_decay[None, :, None, None] + jnp.einsum('bchd,bche,hc->bhde', k_n, v_n, zeta)
        return state, cross

    s0 = jnp.zeros((B, H, dk, dv), jnp.float32)
    xs = (jnp.moveaxis(qc, 1, 0), jnp.moveaxis(kc, 1, 0), jnp.moveaxis(vc, 1, 0))
    _, y_cross = lax.scan(step, s0, xs)
    y = y_intra + jnp.moveaxis(y_cross, 0, 1)
    return y.reshape(B, S, H, dv)


def mla_block_attention(q_nope, q_pe, k_nope, k_pe, v):
    B, S, H, _ = q_nope.shape
    n_blocks = S // Q_BLOCK
    scale = (MLA_NOPE + MLA_ROPE) ** -0.5
    key_chunk = jnp.arange(S) // CHUNK

    def to_blocks(t):
        return jnp.moveaxis(t.reshape(B, n_blocks, Q_BLOCK, *t.shape[2:]), 1, 0)

    def one_block(args):
        blk, qn, qp = args
        s = (jnp.einsum('bqhd,bkhd->bhqk', qn, k_nope)
             + jnp.einsum('bqhr,bkr->bhqk', qp, k_pe)).astype(jnp.float32) * scale
        q_chunk = (blk * Q_BLOCK + jnp.arange(Q_BLOCK)) // CHUNK
        mask = key_chunk[None, :] <= q_chunk[:, None]
        s = jnp.where(mask[None, None], s, -jnp.inf)
        probs = jax.nn.softmax(s, axis=-1).astype(v.dtype)
        return jnp.einsum('bhqk,bkhe->bqhe', probs, v)

    o = lax.map(one_block, (jnp.arange(n_blocks), to_blocks(q_nope), to_blocks(q_pe)))
    return jnp.moveaxis(o, 0, 1).reshape(B, S, H * v.shape[-1])


def hybrid_mixer(h, positions, w_in, ret_gn_g, w_ret_o, q_norm_g, kv_norm_g,
                 w_uq, w_ukv, w_mla_o, w_out):
    B, S, _ = h.shape
    proj = h @ w_in
    (r_q, r_k, r_v, r_g, c_q, c_kv, k_pe_raw, gate_ret, gate_mla) = jnp.split(
        proj, np.cumsum(SPLITS)[:-1], axis=-1)

    rq = rope(r_q.reshape(B, S, RET_HEADS, RET_DK), positions)
    rk = rope(r_k.reshape(B, S, RET_HEADS, RET_DK), positions) * (RET_DK ** -0.5)
    rv = r_v.reshape(B, S, RET_HEADS, RET_DV)
    y = chunk_retention(rq.astype(jnp.float32), rk.astype(jnp.float32), rv.astype(jnp.float32))
    mu = jnp.mean(y, axis=-1, keepdims=True)
    var = jnp.mean(jnp.square(y - mu), axis=-1, keepdims=True)
    y = ((y - mu) * lax.rsqrt(var + EPS)).reshape(B, S, RET_V) * ret_gn_g.astype(jnp.float32)
    y = (jax.nn.silu(r_g.astype(jnp.float32)) * y).astype(h.dtype)
    y_ret = y @ w_ret_o

    q = (rms_norm(c_q, q_norm_g) @ w_uq).reshape(B, S, MLA_HEADS, MLA_NOPE + MLA_ROPE)
    q_nope, q_pe = q[..., :MLA_NOPE], rope(q[..., MLA_NOPE:], positions)
    kv = (rms_norm(c_kv, kv_norm_g) @ w_ukv).reshape(B, S, MLA_HEADS, MLA_NOPE + MLA_DV)
    k_nope, v = kv[..., :MLA_NOPE], kv[..., MLA_NOPE:]
    k_pe = rope(k_pe_raw[:, :, None, :], positions)[:, :, 0, :]
    y_mla = mla_block_attention(q_nope, q_pe, k_nope, k_pe, v) @ w_mla_o

    mix = jax.nn.sigmoid(gate_ret) * y_ret + jax.nn.sigmoid(gate_mla) * y_mla
    return mix @ w_out


def setup_inputs(seed: int = 0) -> dict:
    key = jax.random.key(seed)
    ks = jax.random.split(key, 24)
    f32 = jnp.float32

    def nrm(k, shape, scale):
        return jax.random.normal(k, shape, f32) * scale

    offset = jax.random.randint(ks[2], (BATCH, 1), 0, 4096, dtype=jnp.int32)
    positions = offset + jnp.arange(SEQ, dtype=jnp.int32)[None, :]
    return {
        "x": nrm(ks[0], (BATCH, SEQ, D_MODEL), 1.0),
        "p": nrm(ks[1], (DEPTH, BATCH, SEQ, D_PLE), 1.0),
        "positions": positions,
        "ln_g": 1.0 + nrm(ks[3], (DEPTH, N_LN, D_MODEL), 0.02),
        "ln_b": nrm(ks[4], (DEPTH, N_LN, D_MODEL), 0.02),
        "ffn1_w_in": nrm(ks[5], (DEPTH, D_MODEL, 2 * D_FF), D_MODEL ** -0.5),
        "ffn1_w_out": nrm(ks[6], (DEPTH, D_FF, D_MODEL), DEEPNORM_BETA * D_FF ** -0.5),
        "w_in": nrm(ks[7], (DEPTH, D_MODEL, D_IN_TOTAL), D_MODEL ** -0.5),
        "ret_gn_g": 1.0 + nrm(ks[8], (DEPTH, RET_V), 0.02),
        "w_ret_o": nrm(ks[9], (DEPTH, RET_V, D_MODEL), DEEPNORM_BETA * RET_V ** -0.5),
        "q_norm_g": 1.0 + nrm(ks[10], (DEPTH, Q_LORA), 0.02),
        "kv_norm_g": 1.0 + nrm(ks[11], (DEPTH, KV_LORA), 0.02),
        "w_uq": nrm(ks[12], (DEPTH, Q_LORA, MLA_HEADS * (MLA_NOPE + MLA_ROPE)), Q_LORA ** -0.5),
        "w_ukv": nrm(ks[13], (DEPTH, KV_LORA, MLA_HEADS * (MLA_NOPE + MLA_DV)), KV_LORA ** -0.5),
        "w_mla_o": nrm(ks[14], (DEPTH, MLA_HEADS * MLA_DV, D_MODEL), DEEPNORM_BETA * (MLA_HEADS * MLA_DV) ** -0.5),
        "w_out": nrm(ks[15], (DEPTH, D_MODEL, D_MODEL), DEEPNORM_BETA * D_MODEL ** -0.5),
        "ffn2_w_in": nrm(ks[16], (DEPTH, D_MODEL, 2 * D_FF), D_MODEL ** -0.5),
        "ffn2_w_out": nrm(ks[17], (DEPTH, D_FF, D_MODEL), DEEPNORM_BETA * D_FF ** -0.5),
        "ple_w_gate": nrm(ks[18], (DEPTH, D_MODEL, D_MODEL), D_MODEL ** -0.5),
        "ple_w_proj": nrm(ks[19], (DEPTH, D_PLE, D_MODEL), DEEPNORM_BETA * D_PLE ** -0.5),
    }


def reference(x, p, positions, ln_g, ln_b, ffn1_w_in, ffn1_w_out, w_in, ret_gn_g,
              w_ret_o, q_norm_g, kv_norm_g, w_uq, w_ukv, w_mla_o, w_out,
              ffn2_w_in, ffn2_w_out, ple_w_gate, ple_w_proj):
    h = x
    for i in range(DEPTH):
        h = layer_norm(DEEPNORM_ALPHA * h + 0.5 * swiglu_ffn(h, ffn1_w_in[i], ffn1_w_out[i]),
                       ln_g[i, 0], ln_b[i, 0])
        mixed = hybrid_mixer(h, positions, w_in[i], ret_gn_g[i], w_ret_o[i], q_norm_g[i],
                             kv_norm_g[i], w_uq[i], w_ukv[i], w_mla_o[i], w_out[i])
        h = layer_norm(DEEPNORM_ALPHA * h + mixed, ln_g[i, 1], ln_b[i, 1])
        h = layer_norm(DEEPNORM_ALPHA * h + 0.5 * swiglu_ffn(h, ffn2_w_in[i], ffn2_w_out[i]),
                       ln_g[i, 2], ln_b[i, 2])
        ple = jax.nn.sigmoid(h @ ple_w_gate[i]) * (p[i] @ ple_w_proj[i])
        h = layer_norm(DEEPNORM_ALPHA * h + ple, ln_g[i, 3], ln_b[i, 3])
    return h
```

```python
import functools
import math

import jax
import jax.numpy as jnp
from jax import lax
from jax.experimental import pallas as pl
from jax.experimental.pallas import tpu as pltpu

F32 = jnp.float32
BF16 = jnp.bfloat16

D_MODEL = 1024
CHUNK = 64
D_PLE = 256
D_FF = 2816
HEADS = 8
RET_DK = 128
RET_DV = 256
RET_QK = HEADS * RET_DK
RET_V = HEADS * RET_DV
MLA_NOPE = 128
MLA_ROPE = 64
MLA_DV = 128
Q_LORA = 256
KV_LORA = 256
ROPE_BASE = 10000.0
EPS = 1e-5
ALPHA = 2.0 ** 0.25

LANES = 128
FF_CHUNK = 256
TOKEN_TILE = 512
RET_BLOCK = 256
ATT_Q = 512
ATT_K = 512
MLA_QK = 256
VMEM_LIMIT = 52 * 1024 * 1024
NEG_BIG = -1e30


def _const_spec(shape):
    nd = len(shape)
    return pl.BlockSpec(shape, lambda *_: (0,) * nd, pipeline_mode=pl.Buffered(1))


def _row_spec(tile, width):
    return pl.BlockSpec((tile, width), lambda i: (i, 0))


def _params(*sem):
    return pltpu.CompilerParams(dimension_semantics=sem, vmem_limit_bytes=VMEM_LIMIT)


def _dot(a, b):
    return jnp.dot(a, b, preferred_element_type=F32)


def _layer_norm(y, g, b):
    mu = jnp.mean(y, axis=-1, keepdims=True)
    d = y - mu
    var = jnp.mean(d * d, axis=-1, keepdims=True)
    return d * lax.rsqrt(var + EPS) * g + b


def _rms_norm(x, g):
    return x * lax.rsqrt(jnp.mean(x * x, axis=-1, keepdims=True) + EPS) * g


def _swiglu(xb, w_in_ref, w_out_ref, act_ref):
    for c in range(D_FF // FF_CHUNK):
        lo = c * FF_CHUNK
        g = _dot(xb, w_in_ref[:, lo:lo + FF_CHUNK])
        u = _dot(xb, w_in_ref[:, D_FF + lo:D_FF + lo + FF_CHUNK])
        act_ref[:, lo:lo + FF_CHUNK] = (g * jax.nn.sigmoid(g) * u).astype(BF16)
    return _dot(act_ref[...], w_out_ref[...])


def _ffn_ln_kernel(x_ref, w_in_ref, w_out_ref, g_ref, b_ref, o_ref, act_ref):
    x = x_ref[...]
    f = _swiglu(x.astype(BF16), w_in_ref, w_out_ref, act_ref)
    o_ref[...] = _layer_norm(ALPHA * x + 0.5 * f, g_ref[...], b_ref[...])


def _ffn_ln(x, w_in, w_out, g, b):
    t = x.shape[0]
    return pl.pallas_call(
        _ffn_ln_kernel,
        name="ffn_ln",
        grid=(t // TOKEN_TILE,),
        in_specs=[
            _row_spec(TOKEN_TILE, D_MODEL),
            _const_spec((D_MODEL, 2 * D_FF)),
            _const_spec((D_FF, D_MODEL)),
            _const_spec((1, D_MODEL)),
            _const_spec((1, D_MODEL)),
        ],
        out_specs=_row_spec(TOKEN_TILE, D_MODEL),
        out_shape=jax.ShapeDtypeStruct((t, D_MODEL), F32),
        scratch_shapes=[pltpu.VMEM((TOKEN_TILE, D_FF), BF16)],
        compiler_params=_params("parallel"),
    )(x, w_in, w_out, g, b)


def _ffn_ple_ln_kernel(h_ref, p_ref, w_in_ref, w_out_ref, wg_ref, wp_ref,
                       g2_ref, b2_ref, g3_ref, b3_ref, o_ref, act_ref):
    h = h_ref[...]
    f = _swiglu(h.astype(BF16), w_in_ref, w_out_ref, act_ref)
    h2 = _layer_norm(ALPHA * h + 0.5 * f, g2_ref[...], b2_ref[...])
    gate = jax.nn.sigmoid(_dot(h2.astype(BF16), wg_ref[...]))
    proj = _dot(p_ref[...].astype(BF16), wp_ref[...])
    o_ref[...] = _layer_norm(ALPHA * h2 + gate * proj, g3_ref[...], b3_ref[...])


def _ffn_ple_ln(h, p, w_in, w_out, wg, wp, g2, b2, g3, b3):
    t = h.shape[0]
    return pl.pallas_call(
        _ffn_ple_ln_kernel,
        name="ffn_ple_ln",
        grid=(t // TOKEN_TILE,),
        in_specs=[
            _row_spec(TOKEN_TILE, D_MODEL),
            _row_spec(TOKEN_TILE, D_PLE),
            _const_spec((D_MODEL, 2 * D_FF)),
            _const_spec((D_FF, D_MODEL)),
            _const_spec((D_MODEL, D_MODEL)),
            _const_spec((D_PLE, D_MODEL)),
            _const_spec((1, D_MODEL)),
            _const_spec((1, D_MODEL)),
            _const_spec((1, D_MODEL)),
            _const_spec((1, D_MODEL)),
        ],
        out_specs=_row_spec(TOKEN_TILE, D_MODEL),
        out_shape=jax.ShapeDtypeStruct((t, D_MODEL), F32),
        scratch_shapes=[pltpu.VMEM((TOKEN_TILE, D_FF), BF16)],
        compiler_params=_params("parallel"),
    )(h, p, w_in, w_out, wg, wp, g2, b2, g3, b3)


def _ret_proj_kernel(h_ref, pos_ref, invf_ref, w_ref, rq_ref, rk_ref, rv_ref, rg_ref):
    hb = h_ref[...].astype(BF16)
    ang = pos_ref[...] * invf_ref[...]
    cos = jnp.cos(ang)
    lane = lax.broadcasted_iota(jnp.int32, (1, LANES), 1)
    sin = jnp.sin(ang) * jnp.where(lane < RET_DK // 2, -1.0, 1.0)
    k_scale = RET_DK ** -0.5
    q = _dot(hb, w_ref[:, 0:RET_QK])
    k = _dot(hb, w_ref[:, RET_QK:2 * RET_QK])
    for h in range(HEADS):
        sl = slice(h * RET_DK, (h + 1) * RET_DK)
        tq = q[:, sl]
        rq_ref[:, sl] = (tq * cos + pltpu.roll(tq, RET_DK // 2, 1) * sin).astype(BF16)
        tk = k[:, sl]
        rk_ref[:, sl] = ((tk * cos + pltpu.roll(tk, RET_DK // 2, 1) * sin) * k_scale).astype(BF16)
    rv_ref[...] = _dot(hb, w_ref[:, 2 * RET_QK:2 * RET_QK + RET_V]).astype(BF16)
    rg_ref[...] = _dot(hb, w_ref[:, 2 * RET_QK + RET_V:2 * RET_QK + 2 * RET_V])


def _ret_proj(h, pos_b, invf, w):
    t = h.shape[0]
    return pl.pallas_call(
        _ret_proj_kernel,
        name="ret_proj",
        grid=(t // TOKEN_TILE,),
        in_specs=[
            _row_spec(TOKEN_TILE, D_MODEL),
            _row_spec(TOKEN_TILE, LANES),
            _const_spec((1, LANES)),
            _const_spec((D_MODEL, 2 * RET_QK + 2 * RET_V)),
        ],
        out_specs=[
            _row_spec(TOKEN_TILE, RET_QK),
            _row_spec(TOKEN_TILE, RET_QK),
            _row_spec(TOKEN_TILE, RET_V),
            _row_spec(TOKEN_TILE, RET_V),
        ],
        out_shape=[
            jax.ShapeDtypeStruct((t, RET_QK), BF16),
            jax.ShapeDtypeStruct((t, RET_QK), BF16),
            jax.ShapeDtypeStruct((t, RET_V), BF16),
            jax.ShapeDtypeStruct((t, RET_V), F32),
        ],
        compiler_params=_params("parallel"),
    )(h, pos_b, invf, w)


MLA_LAT = Q_LORA + KV_LORA + 2 * LANES


def _mla_proj_kernel(h_ref, pos_ref, invf_ref, w_ref, qg_ref, kvg_ref, wuq_ref, wukv_ref,
                     qcat_ref, kcat_ref, v_ref, gr_ref, gm_ref):
    hb = h_ref[...].astype(BF16)
    ang = pos_ref[...] * invf_ref[...]
    cos = jnp.cos(ang)
    sin = jnp.sin(ang)
    lane = lax.broadcasted_iota(jnp.int32, (1, LANES), 1)
    first_half = lane < MLA_ROPE

    lat = _dot(hb, w_ref[:, 0:MLA_LAT])
    cq = lat[:, 0:Q_LORA]
    ckv = lat[:, Q_LORA:Q_LORA + KV_LORA]
    kpe = lat[:, Q_LORA + KV_LORA:Q_LORA + KV_LORA + LANES]
    kpe_rot = lat[:, Q_LORA + KV_LORA + LANES:MLA_LAT]

    qf = _dot(_rms_norm(cq, qg_ref[...]).astype(BF16), wuq_ref[...])
    nope_w = HEADS * MLA_NOPE
    rope_w = HEADS * MLA_ROPE
    for j in range(HEADS // 2):
        r = (qf[:, nope_w + j * LANES:nope_w + (j + 1) * LANES] * cos
             + qf[:, nope_w + rope_w + j * LANES:nope_w + rope_w + (j + 1) * LANES] * sin)
        base = 2 * j * MLA_QK
        qcat_ref[:, base + MLA_NOPE:base + MLA_QK] = jnp.where(first_half, r, 0.0).astype(BF16)
        base += MLA_QK
        qcat_ref[:, base + MLA_NOPE:base + MLA_QK] = jnp.where(first_half, 0.0, r).astype(BF16)
    for h in range(HEADS):
        qcat_ref[:, h * MLA_QK:h * MLA_QK + MLA_NOPE] = (
            qf[:, h * MLA_NOPE:(h + 1) * MLA_NOPE].astype(BF16))

    kv = _dot(_rms_norm(ckv, kvg_ref[...]).astype(BF16), wukv_ref[...])
    kr = kpe * cos + kpe_rot * sin
    kr_even = jnp.where(first_half, kr, 0.0).astype(BF16)
    kr_odd = jnp.where(first_half, 0.0, kr).astype(BF16)
    for h in range(HEADS):
        kcat_ref[:, h * MLA_QK:h * MLA_QK + MLA_NOPE] = (
            kv[:, h * MLA_NOPE:(h + 1) * MLA_NOPE].astype(BF16))
        kcat_ref[:, h * MLA_QK + MLA_NOPE:(h + 1) * MLA_QK] = kr_even if h % 2 == 0 else kr_odd
    v_ref[...] = kv[:, nope_w:].astype(BF16)

    gr_ref[...] = _dot(hb, w_ref[:, MLA_LAT:MLA_LAT + D_MODEL])
    gm_ref[...] = _dot(hb, w_ref[:, MLA_LAT + D_MODEL:MLA_LAT + 2 * D_MODEL])


def _mla_proj(h, pos_b, invf, w, qg, kvg, wuq, wukv):
    t = h.shape[0]
    w_cols = MLA_LAT + 2 * D_MODEL
    uq_cols = HEADS * (MLA_NOPE + 2 * MLA_ROPE)
    ukv_cols = HEADS * (MLA_NOPE + MLA_DV)
    return pl.pallas_call(
        _mla_proj_kernel,
        name="mla_proj",
        grid=(t // TOKEN_TILE,),
        in_specs=[
            _row_spec(TOKEN_TILE, D_MODEL),
            _row_spec(TOKEN_TILE, LANES),
            _const_spec((1, LANES)),
            _const_spec((D_MODEL, w_cols)),
            _const_spec((1, Q_LORA)),
            _const_spec((1, KV_LORA)),
            _const_spec((Q_LORA, uq_cols)),
            _const_spec((KV_LORA, ukv_cols)),
        ],
        out_specs=[
            _row_spec(TOKEN_TILE, HEADS * MLA_QK),
            _row_spec(TOKEN_TILE, HEADS * MLA_QK),
            _row_spec(TOKEN_TILE, HEADS * MLA_DV),
            _row_spec(TOKEN_TILE, D_MODEL),
            _row_spec(TOKEN_TILE, D_MODEL),
        ],
        out_shape=[
            jax.ShapeDtypeStruct((t, HEADS * MLA_QK), BF16),
            jax.ShapeDtypeStruct((t, HEADS * MLA_QK), BF16),
            jax.ShapeDtypeStruct((t, HEADS * MLA_DV), BF16),
            jax.ShapeDtypeStruct((t, D_MODEL), F32),
            jax.ShapeDtypeStruct((t, D_MODEL), F32),
        ],
        compiler_params=_params("parallel"),
    )(h, pos_b, invf, w, qg, kvg, wuq, wukv)


def _retention_kernel(q_ref, k_ref, v_ref, g_ref, gn_ref, dmask_ref, xi_ref, zeta_ref,
                      cdec_ref, o_ref, state_ref):
    @pl.when(pl.program_id(1) == 0)
    def _():
        state_ref[...] = jnp.zeros_like(state_ref)

    for h in range(HEADS):
        qs = slice(h * RET_DK, (h + 1) * RET_DK)
        vs = slice(h * RET_DV, (h + 1) * RET_DV)
        q = q_ref[0, :, qs]
        k = k_ref[0, :, qs]
        v = v_ref[0, :, vs]
        state = state_ref[h]
        s = lax.dot_general(q, k, (((1,), (1,)), ((), ())), preferred_element_type=F32)
        s = s * dmask_ref[h]
        y = _dot(s.astype(BF16), v) + _dot(q, state.astype(BF16)) * xi_ref[h]
        kz = (k.astype(F32) * zeta_ref[h]).astype(BF16)
        upd = lax.dot_general(kz, v, (((0,), (0,)), ((), ())), preferred_element_type=F32)
        state_ref[h] = state * cdec_ref[h] + upd
        mu = jnp.mean(y, axis=-1, keepdims=True)
        d = y - mu
        var = jnp.mean(d * d, axis=-1, keepdims=True)
        gate = g_ref[0, :, vs]
        yn = d * lax.rsqrt(var + EPS) * gn_ref[:, vs]
        o_ref[0, :, vs] = (gate * jax.nn.sigmoid(gate) * yn).astype(BF16)


def _retention(rq, rk, rv, rg, gn, dmask, xi, zeta, cdec):
    b, s, _ = rq.shape
    blk = lambda w: pl.BlockSpec((1, RET_BLOCK, w), lambda i, j: (i, j, 0))
    return pl.pallas_call(
        _retention_kernel,
        name="retention",
        grid=(b, s // RET_BLOCK),
        in_specs=[
            blk(RET_QK), blk(RET_QK), blk(RET_V), blk(RET_V),
            _const_spec((1, RET_V)),
            _const_spec((HEADS, RET_BLOCK, RET_BLOCK)),
            _const_spec((HEADS, RET_BLOCK, RET_DV)),
            _const_spec((HEADS, RET_BLOCK, RET_DK)),
            _const_spec((HEADS, 1, RET_DV)),
        ],
        out_specs=blk(RET_V),
        out_shape=jax.ShapeDtypeStruct((b, s, RET_V), BF16),
        scratch_shapes=[pltpu.VMEM((HEADS, RET_DK, RET_DV), F32)],
        compiler_params=_params("parallel", "arbitrary"),
    )(rq, rk, rv, rg, gn, dmask, xi, zeta, cdec)


def _softmax_step(q, k, v, carry, mask=None):
    m, l, acc = carry
    s = lax.dot_general(q, k, (((1,), (1,)), ((), ())), preferred_element_type=F32)
    if mask is not None:
        s = jnp.where(mask, s, -jnp.inf)
    m_new = jnp.maximum(m, jnp.max(s, axis=-1, keepdims=True))
    a = jnp.exp2(m - m_new)
    p = jnp.exp2(s - m_new)
    l = a * l + jnp.sum(p, axis=-1, keepdims=True)
    acc = a * acc + _dot(p.astype(BF16), v)
    return m_new, l, acc


def _mla_attn_kernel(q_ref, k_ref, v_ref, o_ref):
    qi = pl.program_id(2)
    q = q_ref[0]

    def body(j, carry):
        off = pl.multiple_of(j * ATT_K, ATT_K)
        return _softmax_step(q, k_ref[0, pl.ds(off, ATT_K), :], v_ref[0, pl.ds(off, ATT_K), :],
                             carry)

    init = (jnp.full((ATT_Q, 1), NEG_BIG, F32), jnp.zeros((ATT_Q, 1), F32),
            jnp.zeros((ATT_Q, MLA_DV), F32))
    carry = lax.fori_loop(0, qi, body, init)

    off = pl.multiple_of(qi * ATT_K, ATT_K)
    row_chunk = lax.broadcasted_iota(jnp.int32, (ATT_Q, ATT_K), 0) // CHUNK
    col_chunk = lax.broadcasted_iota(jnp.int32, (ATT_Q, ATT_K), 1) // CHUNK
    _, l, acc = _softmax_step(q, k_ref[0, pl.ds(off, ATT_K), :], v_ref[0, pl.ds(off, ATT_K), :],
                              carry, mask=col_chunk <= row_chunk)
    o_ref[0] = (acc / l).astype(BF16)


def _mla_attn(qcat, kcat, v):
    b, s, _ = qcat.shape
    assert ATT_Q == ATT_K
    return pl.pallas_call(
        _mla_attn_kernel,
        name="mla_attn",
        grid=(b, HEADS, s // ATT_Q),
        in_specs=[
            pl.BlockSpec((1, ATT_Q, MLA_QK), lambda i, h, j: (i, j, h)),
            pl.BlockSpec((1, s, MLA_QK), lambda i, h, j: (i, 0, h)),
            pl.BlockSpec((1, s, MLA_DV), lambda i, h, j: (i, 0, h)),
        ],
        out_specs=pl.BlockSpec((1, ATT_Q, MLA_DV), lambda i, h, j: (i, j, h)),
        out_shape=jax.ShapeDtypeStruct((b, s, HEADS * MLA_DV), BF16),
        compiler_params=_params("parallel", "parallel", "arbitrary"),
    )(qcat, kcat, v)


def _merge_ln_kernel(yr_ref, om_ref, gr_ref, gm_ref, h_ref, wro_ref, wmo_ref, wout_ref,
                     g_ref, b_ref, o_ref):
    y_ret = _dot(yr_ref[...], wro_ref[...])
    y_mla = _dot(om_ref[...], wmo_ref[...])
    mix = jax.nn.sigmoid(gr_ref[...]) * y_ret + jax.nn.sigmoid(gm_ref[...]) * y_mla
    mixed = _dot(mix.astype(BF16), wout_ref[...])
    o_ref[...] = _layer_norm(ALPHA * h_ref[...] + mixed, g_ref[...], b_ref[...])


def _merge_ln(yr, om, gr, gm, h, wro, wmo, wout, g, b):
    t = h.shape[0]
    return pl.pallas_call(
        _merge_ln_kernel,
        name="merge_ln",
        grid=(t // TOKEN_TILE,),
        in_specs=[
            _row_spec(TOKEN_TILE, RET_V),
            _row_spec(TOKEN_TILE, HEADS * MLA_DV),
            _row_spec(TOKEN_TILE, D_MODEL),
            _row_spec(TOKEN_TILE, D_MODEL),
            _row_spec(TOKEN_TILE, D_MODEL),
            _const_spec((RET_V, D_MODEL)),
            _const_spec((HEADS * MLA_DV, D_MODEL)),
            _const_spec((D_MODEL, D_MODEL)),
            _const_spec((1, D_MODEL)),
            _const_spec((1, D_MODEL)),
        ],
        out_specs=_row_spec(TOKEN_TILE, D_MODEL),
        out_shape=jax.ShapeDtypeStruct((t, D_MODEL), F32),
        compiler_params=_params("parallel"),
    )(yr, om, gr, gm, h, wro, wmo, wout, g, b)


def _retention_tables():
    log_gamma = jnp.log(1.0 - 2.0 ** (-5.0 - jnp.arange(HEADS, dtype=F32)))
    idx = jnp.arange(RET_BLOCK, dtype=F32)
    chunk = jnp.arange(RET_BLOCK) // CHUNK
    visible = chunk[None, :] <= chunk[:, None]
    dist = jnp.abs(idx[:, None] - idx[None, :])
    dmask = jnp.where(visible[None], jnp.exp(log_gamma[:, None, None] * dist[None]), 0.0)
    xi = jnp.exp(log_gamma[:, None] * (idx + 1.0))
    zeta = jnp.exp(log_gamma[:, None] * (RET_BLOCK - 1.0 - idx))
    cdec = jnp.exp(log_gamma * RET_BLOCK)
    return (dmask,
            jnp.broadcast_to(xi[:, :, None], (HEADS, RET_BLOCK, RET_DV)),
            jnp.broadcast_to(zeta[:, :, None], (HEADS, RET_BLOCK, RET_DK)),
            jnp.broadcast_to(cdec[:, None, None], (HEADS, 1, RET_DV)))


def _rot_half_cols(w, half):
    return jnp.concatenate([-w[:, half:], w[:, :half]], axis=1)


def kernel(x, p, positions, ln_g, ln_b, ffn1_w_in, ffn1_w_out, w_in, ret_gn_g, w_ret_o,
           q_norm_g, kv_norm_g, w_uq, w_ukv, w_mla_o, w_out, ffn2_w_in, ffn2_w_out,
           ple_w_gate, ple_w_proj):
    b, s, d = x.shape
    t = b * s
    h = x.reshape(t, d)
    pos_b = jnp.broadcast_to(positions.astype(F32).reshape(t, 1), (t, LANES))
    row = lambda v: v.reshape(1, -1)

    for i in range(ln_g.shape[0]):
        wi = w_in[i]
        c0 = 2 * RET_QK + 2 * RET_V
        w_ret = wi[:, :c0].astype(BF16)
        w_cq = wi[:, c0:c0 + Q_LORA]
        w_ckv = wi[:, c0 + Q_LORA:c0 + Q_LORA + KV_LORA]
        c1 = c0 + Q_LORA + KV_LORA
        w_kpe = wi[:, c1:c1 + MLA_ROPE]
        w_kpe_rot = _rot_half_cols(w_kpe, MLA_ROPE // 2)
        w_gates = wi[:, c1 + MLA_ROPE:]
        w_mla = jnp.concatenate(
            [w_cq, w_ckv, w_kpe, w_kpe, w_kpe_rot, w_kpe_rot, w_gates], axis=1).astype(BF16)

        q_scale = (MLA_NOPE + MLA_ROPE) ** -0.5 * math.log2(math.e)
        uq = w_uq[i].reshape(Q_LORA, HEADS, MLA_NOPE + MLA_ROPE) * q_scale
        uq_nope = uq[:, :, :MLA_NOPE].reshape(Q_LORA, HEADS * MLA_NOPE)
        uq_rope = uq[:, :, MLA_NOPE:]
        uq_rot = jnp.concatenate(
            [-uq_rope[:, :, MLA_ROPE // 2:], uq_rope[:, :, :MLA_ROPE // 2]], axis=2)
        wuq = jnp.concatenate(
            [uq_nope, uq_rope.reshape(Q_LORA, -1), uq_rot.reshape(Q_LORA, -1)], axis=1).astype(BF16)
        ukv = w_ukv[i].reshape(KV_LORA, HEADS, MLA_NOPE + MLA_DV)
        wukv = jnp.concatenate(
            [ukv[:, :, :MLA_NOPE].reshape(KV_LORA, -1), ukv[:, :, MLA_NOPE:].reshape(KV_LORA, -1)],
            axis=1).astype(BF16)

        invf_ret = ROPE_BASE ** (-jnp.arange(RET_DK // 2, dtype=F32) / (RET_DK // 2))
        invf_ret = jnp.tile(invf_ret, 2).reshape(1, LANES)
        invf_mla = ROPE_BASE ** (-jnp.arange(MLA_ROPE // 2, dtype=F32) / (MLA_ROPE // 2))
        invf_mla = jnp.tile(invf_mla, 4).reshape(1, LANES)

        h = _ffn_ln(h, ffn1_w_in[i].astype(BF16), ffn1_w_out[i].astype(BF16),
                    row(ln_g[i, 0]), row(ln_b[i, 0]))

        rq, rk, rv, rg = _ret_proj(h, pos_b, invf_ret, w_ret)
        qcat, kcat, vm, gate_ret, gate_mla = _mla_proj(
            h, pos_b, invf_mla, w_mla, row(q_norm_g[i]), row(kv_norm_g[i]), wuq, wukv)
        dmask, xi, zeta, cdec = _retention_tables()
        y_ret = _retention(rq.reshape(b, s, -1), rk.reshape(b, s, -1), rv.reshape(b, s, -1),
                           rg.reshape(b, s, -1), row(ret_gn_g[i]), dmask, xi, zeta, cdec)
        o_mla = _mla_attn(qcat.reshape(b, s, -1), kcat.reshape(b, s, -1), vm.reshape(b, s, -1))
        h = _merge_ln(y_ret.reshape(t, -1), o_mla.reshape(t, -1), gate_ret, gate_mla, h,
                      w_ret_o[i].astype(BF16), w_mla_o[i].astype(BF16), w_out[i].astype(BF16),
                      row(ln_g[i, 1]), row(ln_b[i, 1]))

        h = _ffn_ple_ln(h, p[i].reshape(t, -1), ffn2_w_in[i].astype(BF16),
                        ffn2_w_out[i].astype(BF16), ple_w_gate[i].astype(BF16),
                        ple_w_proj[i].astype(BF16), row(ln_g[i, 2]), row(ln_b[i, 2]),
                        row(ln_g[i, 3]), row(ln_b[i, 3]))
    return h.reshape(b, s, d)
```

```python
import functools
import math

import jax
import jax.numpy as jnp
from jax import lax
from jax.experimental import pallas as pl
from jax.experimental.pallas import tpu as pltpu

F32 = jnp.float32
BF16 = jnp.bfloat16

D_MODEL = 1024
CHUNK = 64
D_PLE = 256
D_FF = 2816
HEADS = 8
RET_DK = 128
RET_DV = 256
RET_QK = HEADS * RET_DK
RET_V = HEADS * RET_DV
MLA_NOPE = 128
MLA_ROPE = 64
MLA_DV = 128
Q_LORA = 256
KV_LORA = 256
ROPE_BASE = 10000.0
EPS = 1e-5
ALPHA = 2.0 ** 0.25

LANES = 128
FF_CHUNK = 256
TOKEN_TILE = 512
RET_BLOCK = 256
ATT_Q = 512
ATT_K = 512
ATT_HEADS = 2
MLA_QK = 256
VMEM_LIMIT = 52 * 1024 * 1024
NEG_BIG = -1e30


def _const_spec(shape):
    nd = len(shape)
    return pl.BlockSpec(shape, lambda *_: (0,) * nd, pipeline_mode=pl.Buffered(1))


def _row_spec(tile, width):
    return pl.BlockSpec((tile, width), lambda i: (i, 0))


def _params(*sem):
    return pltpu.CompilerParams(dimension_semantics=sem, vmem_limit_bytes=VMEM_LIMIT)


def _dot(a, b):
    return jnp.dot(a, b, preferred_element_type=F32)


def _layer_norm(y, g, b):
    mu = jnp.mean(y, axis=-1, keepdims=True)
    d = y - mu
    var = jnp.mean(d * d, axis=-1, keepdims=True)
    return d * lax.rsqrt(var + EPS) * g + b


def _rms_norm(x, g):
    return x * lax.rsqrt(jnp.mean(x * x, axis=-1, keepdims=True) + EPS) * g


def _swiglu(xb, w_in_ref, w_out_ref, act_ref):
    for c in range(D_FF // FF_CHUNK):
        lo = c * FF_CHUNK
        g = _dot(xb, w_in_ref[:, lo:lo + FF_CHUNK])
        u = _dot(xb, w_in_ref[:, D_FF + lo:D_FF + lo + FF_CHUNK])
        act_ref[:, lo:lo + FF_CHUNK] = (g * jax.nn.sigmoid(g) * u).astype(BF16)
    return _dot(act_ref[...], w_out_ref[...])


def _ffn_ln_kernel(x_ref, w_in_ref, w_out_ref, g_ref, b_ref, o_ref, act_ref):
    x = x_ref[...]
    f = _swiglu(x.astype(BF16), w_in_ref, w_out_ref, act_ref)
    o_ref[...] = _layer_norm(ALPHA * x + 0.5 * f, g_ref[...], b_ref[...])


def _ffn_ln(x, w_in, w_out, g, b):
    t = x.shape[0]
    return pl.pallas_call(
        _ffn_ln_kernel,
        name="ffn_ln",
        grid=(t // TOKEN_TILE,),
        in_specs=[
            _row_spec(TOKEN_TILE, D_MODEL),
            _const_spec((D_MODEL, 2 * D_FF)),
            _const_spec((D_FF, D_MODEL)),
            _const_spec((1, D_MODEL)),
            _const_spec((1, D_MODEL)),
        ],
        out_specs=_row_spec(TOKEN_TILE, D_MODEL),
        out_shape=jax.ShapeDtypeStruct((t, D_MODEL), F32),
        scratch_shapes=[pltpu.VMEM((TOKEN_TILE, D_FF), BF16)],
        compiler_params=_params("parallel"),
    )(x, w_in, w_out, g, b)


def _ffn_ple_ln_kernel(h_ref, p_ref, w_in_ref, w_out_ref, wg_ref, wp_ref,
                       g2_ref, b2_ref, g3_ref, b3_ref, o_ref, act_ref):
    h = h_ref[...]
    f = _swiglu(h.astype(BF16), w_in_ref, w_out_ref, act_ref)
    h2 = _layer_norm(ALPHA * h + 0.5 * f, g2_ref[...], b2_ref[...])
    gate = jax.nn.sigmoid(_dot(h2.astype(BF16), wg_ref[...]))
    proj = _dot(p_ref[...].astype(BF16), wp_ref[...])
    o_ref[...] = _layer_norm(ALPHA * h2 + gate * proj, g3_ref[...], b3_ref[...])


def _ffn_ple_ln(h, p, w_in, w_out, wg, wp, g2, b2, g3, b3):
    t = h.shape[0]
    return pl.pallas_call(
        _ffn_ple_ln_kernel,
        name="ffn_ple_ln",
        grid=(t // TOKEN_TILE,),
        in_specs=[
            _row_spec(TOKEN_TILE, D_MODEL),
            _row_spec(TOKEN_TILE, D_PLE),
            _const_spec((D_MODEL, 2 * D_FF)),
            _const_spec((D_FF, D_MODEL)),
            _const_spec((D_MODEL, D_MODEL)),
            _const_spec((D_PLE, D_MODEL)),
            _const_spec((1, D_MODEL)),
            _const_spec((1, D_MODEL)),
            _const_spec((1, D_MODEL)),
            _const_spec((1, D_MODEL)),
        ],
        out_specs=_row_spec(TOKEN_TILE, D_MODEL),
        out_shape=jax.ShapeDtypeStruct((t, D_MODEL), F32),
        scratch_shapes=[pltpu.VMEM((TOKEN_TILE, D_FF), BF16)],
        compiler_params=_params("parallel"),
    )(h, p, w_in, w_out, wg, wp, g2, b2, g3, b3)


def _ret_proj_kernel(h_ref, pos_ref, invf_ref, w_ref, rq_ref, rk_ref, rv_ref, rg_ref):
    hb = h_ref[...].astype(BF16)
    ang = pos_ref[...] * invf_ref[...]
    cos = jnp.cos(ang)
    lane = lax.broadcasted_iota(jnp.int32, (1, LANES), 1)
    sin = jnp.sin(ang) * jnp.where(lane < RET_DK // 2, -1.0, 1.0)
    k_scale = RET_DK ** -0.5
    q = _dot(hb, w_ref[:, 0:RET_QK])
    k = _dot(hb, w_ref[:, RET_QK:2 * RET_QK])
    for h in range(HEADS):
        sl = slice(h * RET_DK, (h + 1) * RET_DK)
        tq = q[:, sl]
        rq_ref[:, sl] = (tq * cos + pltpu.roll(tq, RET_DK // 2, 1) * sin).astype(BF16)
        tk = k[:, sl]
        rk_ref[:, sl] = ((tk * cos + pltpu.roll(tk, RET_DK // 2, 1) * sin) * k_scale).astype(BF16)
    rv_ref[...] = _dot(hb, w_ref[:, 2 * RET_QK:2 * RET_QK + RET_V]).astype(BF16)
    rg_ref[...] = _dot(hb, w_ref[:, 2 * RET_QK + RET_V:2 * RET_QK + 2 * RET_V])


def _ret_proj(h, pos_b, invf, w):
    t = h.shape[0]
    return pl.pallas_call(
        _ret_proj_kernel,
        name="ret_proj",
        grid=(t // TOKEN_TILE,),
        in_specs=[
            _row_spec(TOKEN_TILE, D_MODEL),
            _row_spec(TOKEN_TILE, LANES),
            _const_spec((1, LANES)),
            _const_spec((D_MODEL, 2 * RET_QK + 2 * RET_V)),
        ],
        out_specs=[
            _row_spec(TOKEN_TILE, RET_QK),
            _row_spec(TOKEN_TILE, RET_QK),
            _row_spec(TOKEN_TILE, RET_V),
            _row_spec(TOKEN_TILE, RET_V),
        ],
        out_shape=[
            jax.ShapeDtypeStruct((t, RET_QK), BF16),
            jax.ShapeDtypeStruct((t, RET_QK), BF16),
            jax.ShapeDtypeStruct((t, RET_V), BF16),
            jax.ShapeDtypeStruct((t, RET_V), F32),
        ],
        compiler_params=_params("parallel"),
    )(h, pos_b, invf, w)


MLA_LAT = Q_LORA + KV_LORA + 2 * LANES


def _mla_proj_kernel(h_ref, pos_ref, invf_ref, w_ref, qg_ref, kvg_ref, wuq_ref, wuk_ref, wuvt_ref,
                     qcat_ref, kcat_ref, vt_ref, gr_ref, gm_ref):
    hb = h_ref[...].astype(BF16)
    ang = pos_ref[...] * invf_ref[...]
    cos = jnp.cos(ang)
    sin = jnp.sin(ang)
    lane = lax.broadcasted_iota(jnp.int32, (1, LANES), 1)
    first_half = lane < MLA_ROPE

    lat = _dot(hb, w_ref[:, 0:MLA_LAT])
    cq = lat[:, 0:Q_LORA]
    ckv = lat[:, Q_LORA:Q_LORA + KV_LORA]
    kpe = lat[:, Q_LORA + KV_LORA:Q_LORA + KV_LORA + LANES]
    kpe_rot = lat[:, Q_LORA + KV_LORA + LANES:MLA_LAT]

    qf = _dot(_rms_norm(cq, qg_ref[...]).astype(BF16), wuq_ref[...])
    nope_w = HEADS * MLA_NOPE
    rope_w = HEADS * MLA_ROPE
    for j in range(HEADS // 2):
        r = (qf[:, nope_w + j * LANES:nope_w + (j + 1) * LANES] * cos
             + qf[:, nope_w + rope_w + j * LANES:nope_w + rope_w + (j + 1) * LANES] * sin)
        base = 2 * j * MLA_QK
        qcat_ref[:, base + MLA_NOPE:base + MLA_QK] = jnp.where(first_half, r, 0.0).astype(BF16)
        base += MLA_QK
        qcat_ref[:, base + MLA_NOPE:base + MLA_QK] = jnp.where(first_half, 0.0, r).astype(BF16)
    for h in range(HEADS):
        qcat_ref[:, h * MLA_QK:h * MLA_QK + MLA_NOPE] = (
            qf[:, h * MLA_NOPE:(h + 1) * MLA_NOPE].astype(BF16))

    ckvn = _rms_norm(ckv, kvg_ref[...])
    kv = _dot(ckvn.astype(BF16), wuk_ref[...])
    vt_ref[0] = _dot(wuvt_ref[...], ckvn.T.astype(BF16)).astype(BF16)
    kr = kpe * cos + kpe_rot * sin
    kr_even = jnp.where(first_half, kr, 0.0).astype(BF16)
    kr_odd = jnp.where(first_half, 0.0, kr).astype(BF16)
    for h in range(HEADS):
        kcat_ref[:, h * MLA_QK:h * MLA_QK + MLA_NOPE] = (
            kv[:, h * MLA_NOPE:(h + 1) * MLA_NOPE].astype(BF16))
        kcat_ref[:, h * MLA_QK + MLA_NOPE:(h + 1) * MLA_QK] = kr_even if h % 2 == 0 else kr_odd

    gr_ref[...] = _dot(hb, w_ref[:, MLA_LAT:MLA_LAT + D_MODEL])
    gm_ref[...] = _dot(hb, w_ref[:, MLA_LAT + D_MODEL:MLA_LAT + 2 * D_MODEL])


def _mla_proj(h, pos_b, invf, w, qg, kvg, wuq, wuk, wuvt):
    t = h.shape[0]
    w_cols = MLA_LAT + 2 * D_MODEL
    uq_cols = HEADS * (MLA_NOPE + 2 * MLA_ROPE)
    assert TOKEN_TILE == ATT_K
    return pl.pallas_call(
        _mla_proj_kernel,
        name="mla_proj",
        grid=(t // TOKEN_TILE,),
        in_specs=[
            _row_spec(TOKEN_TILE, D_MODEL),
            _row_spec(TOKEN_TILE, LANES),
            _const_spec((1, LANES)),
            _const_spec((D_MODEL, w_cols)),
            _const_spec((1, Q_LORA)),
            _const_spec((1, KV_LORA)),
            _const_spec((Q_LORA, uq_cols)),
            _const_spec((KV_LORA, HEADS * MLA_NOPE)),
            _const_spec((HEADS * MLA_DV, KV_LORA)),
        ],
        out_specs=[
            _row_spec(TOKEN_TILE, HEADS * MLA_QK),
            _row_spec(TOKEN_TILE, HEADS * MLA_QK),
            pl.BlockSpec((1, HEADS * MLA_DV, TOKEN_TILE), lambda i: (i, 0, 0)),
            _row_spec(TOKEN_TILE, D_MODEL),
            _row_spec(TOKEN_TILE, D_MODEL),
        ],
        out_shape=[
            jax.ShapeDtypeStruct((t, HEADS * MLA_QK), BF16),
            jax.ShapeDtypeStruct((t, HEADS * MLA_QK), BF16),
            jax.ShapeDtypeStruct((t // TOKEN_TILE, HEADS * MLA_DV, TOKEN_TILE), BF16),
            jax.ShapeDtypeStruct((t, D_MODEL), F32),
            jax.ShapeDtypeStruct((t, D_MODEL), F32),
        ],
        compiler_params=_params("parallel"),
    )(h, pos_b, invf, w, qg, kvg, wuq, wuk, wuvt)


def _retention_kernel(q_ref, k_ref, v_ref, g_ref, gn_ref, dmask_ref, xi_ref, zeta_ref,
                      cdec_ref, o_ref, state_ref):
    @pl.when(pl.program_id(1) == 0)
    def _():
        state_ref[...] = jnp.zeros_like(state_ref)

    for h in range(HEADS):
        qs = slice(h * RET_DK, (h + 1) * RET_DK)
        vs = slice(h * RET_DV, (h + 1) * RET_DV)
        q = q_ref[0, :, qs]
        k = k_ref[0, :, qs]
        v = v_ref[0, :, vs]
        state = state_ref[h]
        s = lax.dot_general(q, k, (((1,), (1,)), ((), ())), preferred_element_type=F32)
        s = s * dmask_ref[h]
        y = _dot(s.astype(BF16), v) + _dot(q, state.astype(BF16)) * xi_ref[h]
        kz = (k.astype(F32) * zeta_ref[h]).astype(BF16)
        upd = lax.dot_general(kz, v, (((0,), (0,)), ((), ())), preferred_element_type=F32)
        state_ref[h] = state * cdec_ref[h] + upd
        mu = jnp.mean(y, axis=-1, keepdims=True)
        d = y - mu
        var = jnp.mean(d * d, axis=-1, keepdims=True)
        gate = g_ref[0, :, vs]
        yn = d * lax.rsqrt(var + EPS) * gn_ref[:, vs]
        o_ref[0, :, vs] = (gate * jax.nn.sigmoid(gate) * yn).astype(BF16)


def _retention(rq, rk, rv, rg, gn, dmask, xi, zeta, cdec):
    b, s, _ = rq.shape
    blk = lambda w: pl.BlockSpec((1, RET_BLOCK, w), lambda i, j: (i, j, 0))
    return pl.pallas_call(
        _retention_kernel,
        name="retention",
        grid=(b, s // RET_BLOCK),
        in_specs=[
            blk(RET_QK), blk(RET_QK), blk(RET_V), blk(RET_V),
            _const_spec((1, RET_V)),
            _const_spec((HEADS, RET_BLOCK, RET_BLOCK)),
            _const_spec((HEADS, RET_BLOCK, RET_DV)),
            _const_spec((HEADS, RET_BLOCK, RET_DK)),
            _const_spec((HEADS, 1, RET_DV)),
        ],
        out_specs=blk(RET_V),
        out_shape=jax.ShapeDtypeStruct((b, s, RET_V), BF16),
        scratch_shapes=[pltpu.VMEM((HEADS, RET_DK, RET_DV), F32)],
        compiler_params=_params("parallel", "arbitrary"),
    )(rq, rk, rv, rg, gn, dmask, xi, zeta, cdec)


def _softmax_step(st, vt, stats, mask=None):
    m, l, acc = stats
    if mask is not None:
        st = jnp.where(mask, st, -jnp.inf)
    m_new = jnp.maximum(m, jnp.max(st, axis=0, keepdims=True))
    a = jnp.exp2(m - m_new)
    p = jnp.exp2(st - m_new)
    l = a * l + jnp.sum(p, axis=0, keepdims=True)
    acc = a * acc + _dot(vt, p.astype(BF16))
    return m_new, l, acc


def _mla_attn_kernel(q_ref, k_ref, vt_ref, o_ref, s0_ref, s1_ref, m_ref, l_ref, acc_ref):
    qi = pl.program_id(2)
    heads = range(ATT_HEADS)

    def scores(j, s_ref):
        off = pl.multiple_of(j * ATT_K, ATT_K)
        for hh in heads:
            s_ref[hh] = lax.dot_general(
                k_ref[0, pl.ds(off, ATT_K), hh * MLA_QK:(hh + 1) * MLA_QK],
                q_ref[0, :, hh * MLA_QK:(hh + 1) * MLA_QK],
                (((1,), (1,)), ((), ())), preferred_element_type=F32)

    def consume(j, s_ref, mask=None):
        for hh in heads:
            m, l, acc = _softmax_step(
                s_ref[hh], vt_ref[0, j, hh * MLA_DV:(hh + 1) * MLA_DV, :],
                (m_ref[hh], l_ref[hh], acc_ref[hh]), mask)
            m_ref[hh] = m
            l_ref[hh] = l
            acc_ref[hh] = acc

    m_ref[...] = jnp.full_like(m_ref, NEG_BIG)
    l_ref[...] = jnp.zeros_like(l_ref)
    acc_ref[...] = jnp.zeros_like(acc_ref)

    scores(0, s0_ref)

    def pair(jj, c):
        j = 2 * jj
        scores(j + 1, s1_ref)
        consume(j, s0_ref)
        scores(j + 2, s0_ref)
        consume(j + 1, s1_ref)
        return c

    lax.fori_loop(0, qi // 2, pair, 0)

    key_chunk = lax.broadcasted_iota(jnp.int32, (ATT_K, ATT_Q), 0) // CHUNK
    qry_chunk = lax.broadcasted_iota(jnp.int32, (ATT_K, ATT_Q), 1) // CHUNK
    mask = key_chunk <= qry_chunk

    @pl.when(qi % 2 == 0)
    def _():
        consume(qi, s0_ref, mask)

    @pl.when(qi % 2 == 1)
    def _():
        scores(qi, s1_ref)
        consume(qi - 1, s0_ref)
        consume(qi, s1_ref, mask)

    for hh in heads:
        o_ref[0, :, hh * MLA_DV:(hh + 1) * MLA_DV] = (acc_ref[hh] / l_ref[hh]).T.astype(BF16)


def _mla_attn(qcat, kcat, vt):
    b, s, _ = qcat.shape
    assert ATT_Q == ATT_K
    n_kb = s // ATT_K
    return pl.pallas_call(
        _mla_attn_kernel,
        name="mla_attn",
        grid=(b, HEADS // ATT_HEADS, s // ATT_Q),
        in_specs=[
            pl.BlockSpec((1, ATT_Q, ATT_HEADS * MLA_QK), lambda i, h, j: (i, j, h)),
            pl.BlockSpec((1, s, ATT_HEADS * MLA_QK), lambda i, h, j: (i, 0, h)),
            pl.BlockSpec((1, n_kb, ATT_HEADS * MLA_DV, ATT_K), lambda i, h, j: (i, 0, h, 0)),
        ],
        out_specs=pl.BlockSpec((1, ATT_Q, ATT_HEADS * MLA_DV), lambda i, h, j: (i, j, h)),
        out_shape=jax.ShapeDtypeStruct((b, s, HEADS * MLA_DV), BF16),
        scratch_shapes=[
            pltpu.VMEM((ATT_HEADS, ATT_K, ATT_Q), F32),
            pltpu.VMEM((ATT_HEADS, ATT_K, ATT_Q), F32),
            pltpu.VMEM((ATT_HEADS, 1, ATT_Q), F32),
            pltpu.VMEM((ATT_HEADS, 1, ATT_Q), F32),
            pltpu.VMEM((ATT_HEADS, MLA_DV, ATT_Q), F32),
        ],
        compiler_params=_params("parallel", "parallel", "arbitrary"),
    )(qcat, kcat, vt)


def _merge_ln_kernel(yr_ref, om_ref, gr_ref, gm_ref, h_ref, wro_ref, wmo_ref, wout_ref,
                     g_ref, b_ref, o_ref):
    y_ret = _dot(yr_ref[...], wro_ref[...])
    y_mla = _dot(om_ref[...], wmo_ref[...])
    mix = jax.nn.sigmoid(gr_ref[...]) * y_ret + jax.nn.sigmoid(gm_ref[...]) * y_mla
    mixed = _dot(mix.astype(BF16), wout_ref[...])
    o_ref[...] = _layer_norm(ALPHA * h_ref[...] + mixed, g_ref[...], b_ref[...])


def _merge_ln(yr, om, gr, gm, h, wro, wmo, wout, g, b):
    t = h.shape[0]
    return pl.pallas_call(
        _merge_ln_kernel,
        name="merge_ln",
        grid=(t // TOKEN_TILE,),
        in_specs=[
            _row_spec(TOKEN_TILE, RET_V),
            _row_spec(TOKEN_TILE, HEADS * MLA_DV),
            _row_spec(TOKEN_TILE, D_MODEL),
            _row_spec(TOKEN_TILE, D_MODEL),
            _row_spec(TOKEN_TILE, D_MODEL),
            _const_spec((RET_V, D_MODEL)),
            _const_spec((HEADS * MLA_DV, D_MODEL)),
            _const_spec((D_MODEL, D_MODEL)),
            _const_spec((1, D_MODEL)),
            _const_spec((1, D_MODEL)),
        ],
        out_specs=_row_spec(TOKEN_TILE, D_MODEL),
        out_shape=jax.ShapeDtypeStruct((t, D_MODEL), F32),
        compiler_params=_params("parallel"),
    )(yr, om, gr, gm, h, wro, wmo, wout, g, b)


def _retention_tables():
    log_gamma = jnp.log(1.0 - 2.0 ** (-5.0 - jnp.arange(HEADS, dtype=F32)))
    idx = jnp.arange(RET_BLOCK, dtype=F32)
    chunk = jnp.arange(RET_BLOCK) // CHUNK
    visible = chunk[None, :] <= chunk[:, None]
    dist = jnp.abs(idx[:, None] - idx[None, :])
    dmask = jnp.where(visible[None], jnp.exp(log_gamma[:, None, None] * dist[None]), 0.0)
    xi = jnp.exp(log_gamma[:, None] * (idx + 1.0))
    zeta = jnp.exp(log_gamma[:, None] * (RET_BLOCK - 1.0 - idx))
    cdec = jnp.exp(log_gamma * RET_BLOCK)
    return (dmask,
            jnp.broadcast_to(xi[:, :, None], (HEADS, RET_BLOCK, RET_DV)),
            jnp.broadcast_to(zeta[:, :, None], (HEADS, RET_BLOCK, RET_DK)),
            jnp.broadcast_to(cdec[:, None, None], (HEADS, 1, RET_DV)))


def _rot_half_cols(w, half):
    return jnp.concatenate([-w[:, half:], w[:, :half]], axis=1)


def kernel(x, p, positions, ln_g, ln_b, ffn1_w_in, ffn1_w_out, w_in, ret_gn_g, w_ret_o,
           q_norm_g, kv_norm_g, w_uq, w_ukv, w_mla_o, w_out, ffn2_w_in, ffn2_w_out,
           ple_w_gate, ple_w_proj):
    b, s, d = x.shape
    t = b * s
    h = x.reshape(t, d)
    pos_b = jnp.broadcast_to(positions.astype(F32).reshape(t, 1), (t, LANES))
    row = lambda v: v.reshape(1, -1)

    for i in range(ln_g.shape[0]):
        wi = w_in[i]
        c0 = 2 * RET_QK + 2 * RET_V
        w_ret = wi[:, :c0].astype(BF16)
        w_cq = wi[:, c0:c0 + Q_LORA]
        w_ckv = wi[:, c0 + Q_LORA:c0 + Q_LORA + KV_LORA]
        c1 = c0 + Q_LORA + KV_LORA
        w_kpe = wi[:, c1:c1 + MLA_ROPE]
        w_kpe_rot = _rot_half_cols(w_kpe, MLA_ROPE // 2)
        w_gates = wi[:, c1 + MLA_ROPE:]
        w_mla = jnp.concatenate(
            [w_cq, w_ckv, w_kpe, w_kpe, w_kpe_rot, w_kpe_rot, w_gates], axis=1).astype(BF16)

        q_scale = (MLA_NOPE + MLA_ROPE) ** -0.5 * math.log2(math.e)
        uq = w_uq[i].reshape(Q_LORA, HEADS, MLA_NOPE + MLA_ROPE) * q_scale
        uq_nope = uq[:, :, :MLA_NOPE].reshape(Q_LORA, HEADS * MLA_NOPE)
        uq_rope = uq[:, :, MLA_NOPE:]
        uq_rot = jnp.concatenate(
            [-uq_rope[:, :, MLA_ROPE // 2:], uq_rope[:, :, :MLA_ROPE // 2]], axis=2)
        wuq = jnp.concatenate(
            [uq_nope, uq_rope.reshape(Q_LORA, -1), uq_rot.reshape(Q_LORA, -1)], axis=1).astype(BF16)
        ukv = w_ukv[i].reshape(KV_LORA, HEADS, MLA_NOPE + MLA_DV)
        wuk = ukv[:, :, :MLA_NOPE].reshape(KV_LORA, -1).astype(BF16)
        wuvt = ukv[:, :, MLA_NOPE:].reshape(KV_LORA, -1).T.astype(BF16)

        invf_ret = ROPE_BASE ** (-jnp.arange(RET_DK // 2, dtype=F32) / (RET_DK // 2))
        invf_ret = jnp.tile(invf_ret, 2).reshape(1, LANES)
        invf_mla = ROPE_BASE ** (-jnp.arange(MLA_ROPE // 2, dtype=F32) / (MLA_ROPE // 2))
        invf_mla = jnp.tile(invf_mla, 4).reshape(1, LANES)

        h = _ffn_ln(h, ffn1_w_in[i].astype(BF16), ffn1_w_out[i].astype(BF16),
                    row(ln_g[i, 0]), row(ln_b[i, 0]))

        rq, rk, rv, rg = _ret_proj(h, pos_b, invf_ret, w_ret)
        qcat, kcat, vt, gate_ret, gate_mla = _mla_proj(
            h, pos_b, invf_mla, w_mla, row(q_norm_g[i]), row(kv_norm_g[i]), wuq, wuk, wuvt)
        dmask, xi, zeta, cdec = _retention_tables()
        y_ret = _retention(rq.reshape(b, s, -1), rk.reshape(b, s, -1), rv.reshape(b, s, -1),
                           rg.reshape(b, s, -1), row(ret_gn_g[i]), dmask, xi, zeta, cdec)
        o_mla = _mla_attn(qcat.reshape(b, s, -1), kcat.reshape(b, s, -1),
                          vt.reshape(b, s // ATT_K, HEADS * MLA_DV, ATT_K))
        h = _merge_ln(y_ret.reshape(t, -1), o_mla.reshape(t, -1), gate_ret, gate_mla, h,
                      w_ret_o[i].astype(BF16), w_mla_o[i].astype(BF16), w_out[i].astype(BF16),
                      row(ln_g[i, 1]), row(ln_b[i, 1]))

        h = _ffn_ple_ln(h, p[i].reshape(t, -1), ffn2_w_in[i].astype(BF16),
                        ffn2_w_out[i].astype(BF16), ple_w_gate[i].astype(BF16),
                        ple_w_proj[i].astype(BF16), row(ln_g[i, 2]), row(ln_b[i, 2]),
                        row(ln_g[i, 3]), row(ln_b[i, 3]))
    return h.reshape(b, s, d)
```

```python
import functools
import math

import jax
import jax.numpy as jnp
from jax import lax
from jax.experimental import pallas as pl
from jax.experimental.pallas import tpu as pltpu

F32 = jnp.float32
BF16 = jnp.bfloat16

D_MODEL = 1024
CHUNK = 64
D_PLE = 256
D_FF = 2816
HEADS = 8
RET_DK = 128
RET_DV = 256
RET_QK = HEADS * RET_DK
RET_V = HEADS * RET_DV
MLA_NOPE = 128
MLA_ROPE = 64
MLA_DV = 128
Q_LORA = 256
KV_LORA = 256
ROPE_BASE = 10000.0
EPS = 1e-5
ALPHA = 2.0 ** 0.25

LANES = 128
FF_CHUNK = 256
TOKEN_TILE = 512
RET_BLOCK = 256
ATT_Q = 512
ATT_K = 512
ATT_HEADS = 2
MLA_QK = 256
VMEM_LIMIT = 52 * 1024 * 1024
NEG_BIG = -1e30


def _const_spec(shape):
    nd = len(shape)
    return pl.BlockSpec(shape, lambda *_: (0,) * nd, pipeline_mode=pl.Buffered(1))


def _row_spec(tile, width):
    return pl.BlockSpec((tile, width), lambda i: (i, 0))


def _params(*sem):
    return pltpu.CompilerParams(dimension_semantics=sem, vmem_limit_bytes=VMEM_LIMIT)


def _dot(a, b):
    return jnp.dot(a, b, preferred_element_type=F32)


def _layer_norm(y, g, b):
    mu = jnp.mean(y, axis=-1, keepdims=True)
    d = y - mu
    var = jnp.mean(d * d, axis=-1, keepdims=True)
    return d * lax.rsqrt(var + EPS) * g + b


def _rms_norm(x, g):
    return x * lax.rsqrt(jnp.mean(x * x, axis=-1, keepdims=True) + EPS) * g


def _swiglu(xb, w_in_ref, w_out_ref, act_ref):
    for c in range(D_FF // FF_CHUNK):
        lo = c * FF_CHUNK
        g = _dot(xb, w_in_ref[:, lo:lo + FF_CHUNK])
        u = _dot(xb, w_in_ref[:, D_FF + lo:D_FF + lo + FF_CHUNK])
        act_ref[:, lo:lo + FF_CHUNK] = (g * jax.nn.sigmoid(g) * u).astype(BF16)
    return _dot(act_ref[...], w_out_ref[...])


def _ffn_ln_kernel(x_ref, w_in_ref, w_out_ref, g_ref, b_ref, o_ref, act_ref):
    x = x_ref[...]
    f = _swiglu(x.astype(BF16), w_in_ref, w_out_ref, act_ref)
    o_ref[...] = _layer_norm(ALPHA * x + 0.5 * f, g_ref[...], b_ref[...])


def _ffn_ln(x, w_in, w_out, g, b):
    t = x.shape[0]
    return pl.pallas_call(
        _ffn_ln_kernel,
        name="ffn_ln",
        grid=(t // TOKEN_TILE,),
        in_specs=[
            _row_spec(TOKEN_TILE, D_MODEL),
            _const_spec((D_MODEL, 2 * D_FF)),
            _const_spec((D_FF, D_MODEL)),
            _const_spec((1, D_MODEL)),
            _const_spec((1, D_MODEL)),
        ],
        out_specs=_row_spec(TOKEN_TILE, D_MODEL),
        out_shape=jax.ShapeDtypeStruct((t, D_MODEL), F32),
        scratch_shapes=[pltpu.VMEM((TOKEN_TILE, D_FF), BF16)],
        compiler_params=_params("parallel"),
    )(x, w_in, w_out, g, b)


def _ffn_ple_ln_kernel(h_ref, p_ref, w_in_ref, w_out_ref, wg_ref, wp_ref,
                       g2_ref, b2_ref, g3_ref, b3_ref, o_ref, act_ref):
    h = h_ref[...]
    f = _swiglu(h.astype(BF16), w_in_ref, w_out_ref, act_ref)
    h2 = _layer_norm(ALPHA * h + 0.5 * f, g2_ref[...], b2_ref[...])
    gate = jax.nn.sigmoid(_dot(h2.astype(BF16), wg_ref[...]))
    proj = _dot(p_ref[...].astype(BF16), wp_ref[...])
    o_ref[...] = _layer_norm(ALPHA * h2 + gate * proj, g3_ref[...], b3_ref[...])


def _ffn_ple_ln(h, p, w_in, w_out, wg, wp, g2, b2, g3, b3):
    t = h.shape[0]
    return pl.pallas_call(
        _ffn_ple_ln_kernel,
        name="ffn_ple_ln",
        grid=(t // TOKEN_TILE,),
        in_specs=[
            _row_spec(TOKEN_TILE, D_MODEL),
            _row_spec(TOKEN_TILE, D_PLE),
            _const_spec((D_MODEL, 2 * D_FF)),
            _const_spec((D_FF, D_MODEL)),
            _const_spec((D_MODEL, D_MODEL)),
            _const_spec((D_PLE, D_MODEL)),
            _const_spec((1, D_MODEL)),
            _const_spec((1, D_MODEL)),
            _const_spec((1, D_MODEL)),
            _const_spec((1, D_MODEL)),
        ],
        out_specs=_row_spec(TOKEN_TILE, D_MODEL),
        out_shape=jax.ShapeDtypeStruct((t, D_MODEL), F32),
        scratch_shapes=[pltpu.VMEM((TOKEN_TILE, D_FF), BF16)],
        compiler_params=_params("parallel"),
    )(h, p, w_in, w_out, wg, wp, g2, b2, g3, b3)


def _ret_proj_kernel(h_ref, pos_ref, invf_ref, xi_ref, zeta_ref, w_ref,
                     rq_ref, rqx_ref, rk_ref, rkz_ref, rv_ref, sg_ref):
    hb = h_ref[...].astype(BF16)
    ang = pos_ref[...] * invf_ref[...]
    cos = jnp.cos(ang)
    lane = lax.broadcasted_iota(jnp.int32, (1, LANES), 1)
    sin = jnp.sin(ang) * jnp.where(lane < RET_DK // 2, -1.0, 1.0)
    k_scale = RET_DK ** -0.5
    g = _dot(hb, w_ref[:, 2 * RET_QK + RET_V:2 * RET_QK + 2 * RET_V])
    sg_ref[...] = (g * jax.nn.sigmoid(g)).astype(BF16)
    q = _dot(hb, w_ref[:, 0:RET_QK])
    k = _dot(hb, w_ref[:, RET_QK:2 * RET_QK])
    for h in range(HEADS):
        sl = slice(h * RET_DK, (h + 1) * RET_DK)
        tq = q[:, sl]
        tq = tq * cos + pltpu.roll(tq, RET_DK // 2, 1) * sin
        rq_ref[:, sl] = tq.astype(BF16)
        rqx_ref[:, sl] = (tq * xi_ref[:, sl]).astype(BF16)
        tk = k[:, sl]
        tk = (tk * cos + pltpu.roll(tk, RET_DK // 2, 1) * sin) * k_scale
        rk_ref[:, sl] = tk.astype(BF16)
        rkz_ref[:, sl] = (tk * zeta_ref[:, sl]).astype(BF16)
    rv_ref[...] = _dot(hb, w_ref[:, 2 * RET_QK:2 * RET_QK + RET_V]).astype(BF16)


def _ret_proj(h, pos_b, invf, xi, zeta, w):
    t = h.shape[0]
    qk = jax.ShapeDtypeStruct((t, RET_QK), BF16)
    vv = jax.ShapeDtypeStruct((t, RET_V), BF16)
    return pl.pallas_call(
        _ret_proj_kernel,
        name="ret_proj",
        grid=(t // TOKEN_TILE,),
        in_specs=[
            _row_spec(TOKEN_TILE, D_MODEL),
            _row_spec(TOKEN_TILE, LANES),
            _const_spec((1, LANES)),
            _const_spec((TOKEN_TILE, RET_QK)),
            _const_spec((TOKEN_TILE, RET_QK)),
            _const_spec((D_MODEL, 2 * RET_QK + 2 * RET_V)),
        ],
        out_specs=[_row_spec(TOKEN_TILE, RET_QK)] * 4 + [_row_spec(TOKEN_TILE, RET_V)] * 2,
        out_shape=[qk, qk, qk, qk, vv, vv],
        compiler_params=_params("parallel"),
    )(h, pos_b, invf, xi, zeta, w)


MLA_LAT = Q_LORA + KV_LORA + 2 * LANES


def _mla_proj_kernel(h_ref, pos_ref, invf_ref, w_ref, qg_ref, kvg_ref, wuq_ref, wuk_ref, wuvt_ref,
                     qcat_ref, kcat_ref, vt_ref, gr_ref, gm_ref):
    hb = h_ref[...].astype(BF16)
    ang = pos_ref[...] * invf_ref[...]
    cos = jnp.cos(ang)
    sin = jnp.sin(ang)
    lane = lax.broadcasted_iota(jnp.int32, (1, LANES), 1)
    first_half = lane < MLA_ROPE

    lat = _dot(hb, w_ref[:, 0:MLA_LAT])
    cq = lat[:, 0:Q_LORA]
    ckv = lat[:, Q_LORA:Q_LORA + KV_LORA]
    kpe = lat[:, Q_LORA + KV_LORA:Q_LORA + KV_LORA + LANES]
    kpe_rot = lat[:, Q_LORA + KV_LORA + LANES:MLA_LAT]

    qf = _dot(_rms_norm(cq, qg_ref[...]).astype(BF16), wuq_ref[...])
    nope_w = HEADS * MLA_NOPE
    rope_w = HEADS * MLA_ROPE
    for j in range(HEADS // 2):
        r = (qf[:, nope_w + j * LANES:nope_w + (j + 1) * LANES] * cos
             + qf[:, nope_w + rope_w + j * LANES:nope_w + rope_w + (j + 1) * LANES] * sin)
        base = 2 * j * MLA_QK
        qcat_ref[:, base + MLA_NOPE:base + MLA_QK] = jnp.where(first_half, r, 0.0).astype(BF16)
        base += MLA_QK
        qcat_ref[:, base + MLA_NOPE:base + MLA_QK] = jnp.where(first_half, 0.0, r).astype(BF16)
    for h in range(HEADS):
        qcat_ref[:, h * MLA_QK:h * MLA_QK + MLA_NOPE] = (
            qf[:, h * MLA_NOPE:(h + 1) * MLA_NOPE].astype(BF16))

    ckvn = _rms_norm(ckv, kvg_ref[...])
    kv = _dot(ckvn.astype(BF16), wuk_ref[...])
    vt_ref[0] = _dot(wuvt_ref[...], ckvn.T.astype(BF16)).astype(BF16)
    kr = kpe * cos + kpe_rot * sin
    kr_even = jnp.where(first_half, kr, 0.0).astype(BF16)
    kr_odd = jnp.where(first_half, 0.0, kr).astype(BF16)
    for h in range(HEADS):
        kcat_ref[:, h * MLA_QK:h * MLA_QK + MLA_NOPE] = (
            kv[:, h * MLA_NOPE:(h + 1) * MLA_NOPE].astype(BF16))
        kcat_ref[:, h * MLA_QK + MLA_NOPE:(h + 1) * MLA_QK] = kr_even if h % 2 == 0 else kr_odd

    gr_ref[...] = _dot(hb, w_ref[:, MLA_LAT:MLA_LAT + D_MODEL])
    gm_ref[...] = _dot(hb, w_ref[:, MLA_LAT + D_MODEL:MLA_LAT + 2 * D_MODEL])


def _mla_proj(h, pos_b, invf, w, qg, kvg, wuq, wuk, wuvt):
    t = h.shape[0]
    w_cols = MLA_LAT + 2 * D_MODEL
    uq_cols = HEADS * (MLA_NOPE + 2 * MLA_ROPE)
    assert TOKEN_TILE == ATT_K
    return pl.pallas_call(
        _mla_proj_kernel,
        name="mla_proj",
        grid=(t // TOKEN_TILE,),
        in_specs=[
            _row_spec(TOKEN_TILE, D_MODEL),
            _row_spec(TOKEN_TILE, LANES),
            _const_spec((1, LANES)),
            _const_spec((D_MODEL, w_cols)),
            _const_spec((1, Q_LORA)),
            _const_spec((1, KV_LORA)),
            _const_spec((Q_LORA, uq_cols)),
            _const_spec((KV_LORA, HEADS * MLA_NOPE)),
            _const_spec((HEADS * MLA_DV, KV_LORA)),
        ],
        out_specs=[
            _row_spec(TOKEN_TILE, HEADS * MLA_QK),
            _row_spec(TOKEN_TILE, HEADS * MLA_QK),
            pl.BlockSpec((1, HEADS * MLA_DV, TOKEN_TILE), lambda i: (i, 0, 0)),
            _row_spec(TOKEN_TILE, D_MODEL),
            _row_spec(TOKEN_TILE, D_MODEL),
        ],
        out_shape=[
            jax.ShapeDtypeStruct((t, HEADS * MLA_QK), BF16),
            jax.ShapeDtypeStruct((t, HEADS * MLA_QK), BF16),
            jax.ShapeDtypeStruct((t // TOKEN_TILE, HEADS * MLA_DV, TOKEN_TILE), BF16),
            jax.ShapeDtypeStruct((t, D_MODEL), F32),
            jax.ShapeDtypeStruct((t, D_MODEL), F32),
        ],
        compiler_params=_params("parallel"),
    )(h, pos_b, invf, w, qg, kvg, wuq, wuk, wuvt)


def _retention_kernel(q_ref, qx_ref, k_ref, kz_ref, v_ref, sg_ref, dmask_ref, cdec_ref,
                      o_ref, state_ref):
    @pl.when(pl.program_id(1) == 0)
    def _():
        state_ref[...] = jnp.zeros_like(state_ref)

    def qk(h):
        qs = slice(h * RET_DK, (h + 1) * RET_DK)
        return lax.dot_general(q_ref[0, :, qs], k_ref[0, :, qs], (((1,), (1,)), ((), ())),
                               preferred_element_type=F32)

    s_next = qk(0)
    for h in range(HEADS):
        qs = slice(h * RET_DK, (h + 1) * RET_DK)
        vs = slice(h * RET_DV, (h + 1) * RET_DV)
        v = v_ref[0, :, vs]
        state = state_ref[h]
        s = s_next
        if h + 1 < HEADS:
            s_next = qk(h + 1)
        cross = _dot(qx_ref[0, :, qs], state.astype(BF16))
        upd = lax.dot_general(kz_ref[0, :, qs], v, (((0,), (0,)), ((), ())),
                              preferred_element_type=F32)
        state_ref[h] = state * cdec_ref[h] + upd
        y = _dot((s * dmask_ref[h]).astype(BF16), v) + cross
        mu = jnp.mean(y, axis=-1, keepdims=True)
        d = y - mu
        var = jnp.mean(d * d, axis=-1, keepdims=True)
        o_ref[0, :, vs] = (d * lax.rsqrt(var + EPS) * sg_ref[0, :, vs].astype(F32)).astype(BF16)


def _retention(rq, rqx, rk, rkz, rv, sg, dmask, cdec):
    b, s, _ = rq.shape
    blk = lambda w: pl.BlockSpec((1, RET_BLOCK, w), lambda i, j: (i, j, 0))
    return pl.pallas_call(
        _retention_kernel,
        name="retention",
        grid=(b, s // RET_BLOCK),
        in_specs=[
            blk(RET_QK), blk(RET_QK), blk(RET_QK), blk(RET_QK), blk(RET_V), blk(RET_V),
            _const_spec((HEADS, RET_BLOCK, RET_BLOCK)),
            _const_spec((HEADS, 1, RET_DV)),
        ],
        out_specs=blk(RET_V),
        out_shape=jax.ShapeDtypeStruct((b, s, RET_V), BF16),
        scratch_shapes=[pltpu.VMEM((HEADS, RET_DK, RET_DV), F32)],
        compiler_params=_params("parallel", "arbitrary"),
    )(rq, rqx, rk, rkz, rv, sg, dmask, cdec)


def _softmax_step(st, mx, vt, stats):
    m, l, acc = stats
    m_new = jnp.maximum(m, mx)
    a = jnp.exp2(m - m_new)
    p = jnp.exp2(st - m_new)
    l = a * l + jnp.sum(p, axis=0, keepdims=True)
    acc = a * acc + _dot(vt, p.astype(BF16))
    return m_new, l, acc


def _mla_attn_kernel(q_ref, k_ref, vt_ref, o_ref, s0_ref, s1_ref, mx0_ref, mx1_ref,
                     m_ref, l_ref, acc_ref):
    qi = pl.program_id(2)
    heads = range(ATT_HEADS)

    def scores(j, s_ref, mx_ref):
        off = pl.multiple_of(j * ATT_K, ATT_K)
        for hh in heads:
            st = lax.dot_general(
                k_ref[0, pl.ds(off, ATT_K), hh * MLA_QK:(hh + 1) * MLA_QK],
                q_ref[0, :, hh * MLA_QK:(hh + 1) * MLA_QK],
                (((1,), (1,)), ((), ())), preferred_element_type=F32)
            s_ref[hh] = st
            mx_ref[hh] = jnp.max(st, axis=0, keepdims=True)

    def consume(j, s_ref, mx_ref, mask=None):
        for hh in heads:
            st = s_ref[hh]
            if mask is None:
                mx = mx_ref[hh]
            else:
                st = jnp.where(mask, st, -jnp.inf)
                mx = jnp.max(st, axis=0, keepdims=True)
            m, l, acc = _softmax_step(st, mx, vt_ref[0, j, hh * MLA_DV:(hh + 1) * MLA_DV, :],
                                      (m_ref[hh], l_ref[hh], acc_ref[hh]))
            m_ref[hh] = m
            l_ref[hh] = l
            acc_ref[hh] = acc

    m_ref[...] = jnp.full_like(m_ref, NEG_BIG)
    l_ref[...] = jnp.zeros_like(l_ref)
    acc_ref[...] = jnp.zeros_like(acc_ref)

    scores(0, s0_ref, mx0_ref)

    def pair(jj, c):
        j = 2 * jj
        scores(j + 1, s1_ref, mx1_ref)
        consume(j, s0_ref, mx0_ref)
        scores(j + 2, s0_ref, mx0_ref)
        consume(j + 1, s1_ref, mx1_ref)
        return c

    lax.fori_loop(0, qi // 2, pair, 0)

    key_chunk = lax.broadcasted_iota(jnp.int32, (ATT_K, ATT_Q), 0) // CHUNK
    qry_chunk = lax.broadcasted_iota(jnp.int32, (ATT_K, ATT_Q), 1) // CHUNK
    mask = key_chunk <= qry_chunk

    @pl.when(qi % 2 == 0)
    def _():
        consume(qi, s0_ref, mx0_ref, mask)

    @pl.when(qi % 2 == 1)
    def _():
        scores(qi, s1_ref, mx1_ref)
        consume(qi - 1, s0_ref, mx0_ref)
        consume(qi, s1_ref, mx1_ref, mask)

    for hh in heads:
        o_ref[0, :, hh * MLA_DV:(hh + 1) * MLA_DV] = (acc_ref[hh] / l_ref[hh]).T.astype(BF16)


def _mla_attn(qcat, kcat, vt):
    b, s, _ = qcat.shape
    assert ATT_Q == ATT_K
    n_kb = s // ATT_K
    return pl.pallas_call(
        _mla_attn_kernel,
        name="mla_attn",
        grid=(b, HEADS // ATT_HEADS, s // ATT_Q),
        in_specs=[
            pl.BlockSpec((1, ATT_Q, ATT_HEADS * MLA_QK), lambda i, h, j: (i, j, h)),
            pl.BlockSpec((1, s, ATT_HEADS * MLA_QK), lambda i, h, j: (i, 0, h)),
            pl.BlockSpec((1, n_kb, ATT_HEADS * MLA_DV, ATT_K), lambda i, h, j: (i, 0, h, 0)),
        ],
        out_specs=pl.BlockSpec((1, ATT_Q, ATT_HEADS * MLA_DV), lambda i, h, j: (i, j, h)),
        out_shape=jax.ShapeDtypeStruct((b, s, HEADS * MLA_DV), BF16),
        scratch_shapes=[
            pltpu.VMEM((ATT_HEADS, ATT_K, ATT_Q), F32),
            pltpu.VMEM((ATT_HEADS, ATT_K, ATT_Q), F32),
            pltpu.VMEM((ATT_HEADS, 1, ATT_Q), F32),
            pltpu.VMEM((ATT_HEADS, 1, ATT_Q), F32),
            pltpu.VMEM((ATT_HEADS, 1, ATT_Q), F32),
            pltpu.VMEM((ATT_HEADS, 1, ATT_Q), F32),
            pltpu.VMEM((ATT_HEADS, MLA_DV, ATT_Q), F32),
        ],
        compiler_params=_params("parallel", "parallel", "arbitrary"),
    )(qcat, kcat, vt)


def _merge_ln_kernel(yr_ref, om_ref, gr_ref, gm_ref, h_ref, wro_ref, wmo_ref, wout_ref,
                     g_ref, b_ref, o_ref):
    y_ret = _dot(yr_ref[...], wro_ref[...])
    y_mla = _dot(om_ref[...], wmo_ref[...])
    mix = jax.nn.sigmoid(gr_ref[...]) * y_ret + jax.nn.sigmoid(gm_ref[...]) * y_mla
    mixed = _dot(mix.astype(BF16), wout_ref[...])
    o_ref[...] = _layer_norm(ALPHA * h_ref[...] + mixed, g_ref[...], b_ref[...])


def _merge_ln(yr, om, gr, gm, h, wro, wmo, wout, g, b):
    t = h.shape[0]
    return pl.pallas_call(
        _merge_ln_kernel,
        name="merge_ln",
        grid=(t // TOKEN_TILE,),
        in_specs=[
            _row_spec(TOKEN_TILE, RET_V),
            _row_spec(TOKEN_TILE, HEADS * MLA_DV),
            _row_spec(TOKEN_TILE, D_MODEL),
            _row_spec(TOKEN_TILE, D_MODEL),
            _row_spec(TOKEN_TILE, D_MODEL),
            _const_spec((RET_V, D_MODEL)),
            _const_spec((HEADS * MLA_DV, D_MODEL)),
            _const_spec((D_MODEL, D_MODEL)),
            _const_spec((1, D_MODEL)),
            _const_spec((1, D_MODEL)),
        ],
        out_specs=_row_spec(TOKEN_TILE, D_MODEL),
        out_shape=jax.ShapeDtypeStruct((t, D_MODEL), F32),
        compiler_params=_params("parallel"),
    )(yr, om, gr, gm, h, wro, wmo, wout, g, b)


def _retention_tables():
    log_gamma = jnp.log(1.0 - 2.0 ** (-5.0 - jnp.arange(HEADS, dtype=F32)))
    idx = jnp.arange(RET_BLOCK, dtype=F32)
    chunk = jnp.arange(RET_BLOCK) // CHUNK
    visible = chunk[None, :] <= chunk[:, None]
    dist = jnp.abs(idx[:, None] - idx[None, :])
    dmask = jnp.where(visible[None], jnp.exp(log_gamma[:, None, None] * dist[None]), 0.0)
    xi = jnp.exp(log_gamma[:, None] * (idx + 1.0))
    zeta = jnp.exp(log_gamma[:, None] * (RET_BLOCK - 1.0 - idx))
    cdec = jnp.exp(log_gamma * RET_BLOCK)
    def per_token_rows(tab):
        rows = jnp.broadcast_to(tab.T[:, :, None], (RET_BLOCK, HEADS, RET_DK))
        return jnp.tile(rows.reshape(RET_BLOCK, RET_QK), (TOKEN_TILE // RET_BLOCK, 1))

    return (dmask, per_token_rows(xi), per_token_rows(zeta),
            jnp.broadcast_to(cdec[:, None, None], (HEADS, 1, RET_DV)))


def _rot_half_cols(w, half):
    return jnp.concatenate([-w[:, half:], w[:, :half]], axis=1)


def kernel(x, p, positions, ln_g, ln_b, ffn1_w_in, ffn1_w_out, w_in, ret_gn_g, w_ret_o,
           q_norm_g, kv_norm_g, w_uq, w_ukv, w_mla_o, w_out, ffn2_w_in, ffn2_w_out,
           ple_w_gate, ple_w_proj):
    b, s, d = x.shape
    t = b * s
    h = x.reshape(t, d)
    pos_b = jnp.broadcast_to(positions.astype(F32).reshape(t, 1), (t, LANES))
    row = lambda v: v.reshape(1, -1)

    for i in range(ln_g.shape[0]):
        wi = w_in[i]
        c0 = 2 * RET_QK + 2 * RET_V
        w_ret = wi[:, :c0].astype(BF16)
        w_cq = wi[:, c0:c0 + Q_LORA]
        w_ckv = wi[:, c0 + Q_LORA:c0 + Q_LORA + KV_LORA]
        c1 = c0 + Q_LORA + KV_LORA
        w_kpe = wi[:, c1:c1 + MLA_ROPE]
        w_kpe_rot = _rot_half_cols(w_kpe, MLA_ROPE // 2)
        w_gates = wi[:, c1 + MLA_ROPE:]
        w_mla = jnp.concatenate(
            [w_cq, w_ckv, w_kpe, w_kpe, w_kpe_rot, w_kpe_rot, w_gates], axis=1).astype(BF16)

        q_scale = (MLA_NOPE + MLA_ROPE) ** -0.5 * math.log2(math.e)
        uq = w_uq[i].reshape(Q_LORA, HEADS, MLA_NOPE + MLA_ROPE) * q_scale
        uq_nope = uq[:, :, :MLA_NOPE].reshape(Q_LORA, HEADS * MLA_NOPE)
        uq_rope = uq[:, :, MLA_NOPE:]
        uq_rot = jnp.concatenate(
            [-uq_rope[:, :, MLA_ROPE // 2:], uq_rope[:, :, :MLA_ROPE // 2]], axis=2)
        wuq = jnp.concatenate(
            [uq_nope, uq_rope.reshape(Q_LORA, -1), uq_rot.reshape(Q_LORA, -1)], axis=1).astype(BF16)
        ukv = w_ukv[i].reshape(KV_LORA, HEADS, MLA_NOPE + MLA_DV)
        wuk = ukv[:, :, :MLA_NOPE].reshape(KV_LORA, -1).astype(BF16)
        wuvt = ukv[:, :, MLA_NOPE:].reshape(KV_LORA, -1).T.astype(BF16)

        invf_ret = ROPE_BASE ** (-jnp.arange(RET_DK // 2, dtype=F32) / (RET_DK // 2))
        invf_ret = jnp.tile(invf_ret, 2).reshape(1, LANES)
        invf_mla = ROPE_BASE ** (-jnp.arange(MLA_ROPE // 2, dtype=F32) / (MLA_ROPE // 2))
        invf_mla = jnp.tile(invf_mla, 4).reshape(1, LANES)

        h = _ffn_ln(h, ffn1_w_in[i].astype(BF16), ffn1_w_out[i].astype(BF16),
                    row(ln_g[i, 0]), row(ln_b[i, 0]))

        dmask, xi, zeta, cdec = _retention_tables()
        ret = _ret_proj(h, pos_b, invf_ret, xi, zeta, w_ret)
        qcat, kcat, vt, gate_ret, gate_mla = _mla_proj(
            h, pos_b, invf_mla, w_mla, row(q_norm_g[i]), row(kv_norm_g[i]), wuq, wuk, wuvt)
        y_ret = _retention(*(a.reshape(b, s, -1) for a in ret), dmask, cdec)
        o_mla = _mla_attn(qcat.reshape(b, s, -1), kcat.reshape(b, s, -1),
                          vt.reshape(b, s // ATT_K, HEADS * MLA_DV, ATT_K))
        w_ro = (ret_gn_g[i][:, None] * w_ret_o[i]).astype(BF16)
        h = _merge_ln(y_ret.reshape(t, -1), o_mla.reshape(t, -1), gate_ret, gate_mla, h,
                      w_ro, w_mla_o[i].astype(BF16), w_out[i].astype(BF16),
                      row(ln_g[i, 1]), row(ln_b[i, 1]))

        h = _ffn_ple_ln(h, p[i].reshape(t, -1), ffn2_w_in[i].astype(BF16),
                        ffn2_w_out[i].astype(BF16), ple_w_gate[i].astype(BF16),
                        ple_w_proj[i].astype(BF16), row(ln_g[i, 2]), row(ln_b[i, 2]),
                        row(ln_g[i, 3]), row(ln_b[i, 3]))
    return h.reshape(b, s, d)
```

```python
import functools
import math

import jax
import jax.numpy as jnp
from jax import lax
from jax.experimental import pallas as pl
from jax.experimental.pallas import tpu as pltpu

F32 = jnp.float32
BF16 = jnp.bfloat16

D_MODEL = 1024
CHUNK = 64
D_PLE = 256
D_FF = 2816
HEADS = 8
RET_DK = 128
RET_DV = 256
RET_QK = HEADS * RET_DK
RET_V = HEADS * RET_DV
MLA_NOPE = 128
MLA_ROPE = 64
MLA_DV = 128
Q_LORA = 256
KV_LORA = 256
ROPE_BASE = 10000.0
EPS = 1e-5
ALPHA = 2.0 ** 0.25

LANES = 128
FF_CHUNK = 256
TOKEN_TILE = 512
RET_BLOCK = 256
ATT_Q = 512
ATT_K = 512
ATT_HEADS = 2
MLA_QK = 256
VMEM_LIMIT = 52 * 1024 * 1024
NEG_BIG = -1e30


def _const_spec(shape):
    nd = len(shape)
    return pl.BlockSpec(shape, lambda *_: (0,) * nd, pipeline_mode=pl.Buffered(1))


def _row_spec(tile, width):
    return pl.BlockSpec((tile, width), lambda i: (i, 0))


def _params(*sem):
    return pltpu.CompilerParams(dimension_semantics=sem, vmem_limit_bytes=VMEM_LIMIT)


def _dot(a, b):
    return jnp.dot(a, b, preferred_element_type=F32)


def _layer_norm(y, g, b):
    mu = jnp.mean(y, axis=-1, keepdims=True)
    d = y - mu
    var = jnp.mean(d * d, axis=-1, keepdims=True)
    return d * lax.rsqrt(var + EPS) * g + b


def _rms_norm(x, g):
    return x * lax.rsqrt(jnp.mean(x * x, axis=-1, keepdims=True) + EPS) * g


def _swiglu(xb, w_in_ref, w_out_ref, act_ref):
    for c in range(D_FF // FF_CHUNK):
        lo = c * FF_CHUNK
        g = _dot(xb, w_in_ref[:, lo:lo + FF_CHUNK])
        u = _dot(xb, w_in_ref[:, D_FF + lo:D_FF + lo + FF_CHUNK])
        act_ref[:, lo:lo + FF_CHUNK] = (g * jax.nn.sigmoid(g) * u).astype(BF16)
    return _dot(act_ref[...], w_out_ref[...])


def _ffn_ln_kernel(x_ref, w_in_ref, w_out_ref, g_ref, b_ref, o_ref, act_ref):
    x = x_ref[...]
    f = _swiglu(x.astype(BF16), w_in_ref, w_out_ref, act_ref)
    o_ref[...] = _layer_norm(ALPHA * x + 0.5 * f, g_ref[...], b_ref[...])


def _ffn_ln(x, w_in, w_out, g, b):
    t = x.shape[0]
    return pl.pallas_call(
        _ffn_ln_kernel,
        name="ffn_ln",
        grid=(t // TOKEN_TILE,),
        in_specs=[
            _row_spec(TOKEN_TILE, D_MODEL),
            _const_spec((D_MODEL, 2 * D_FF)),
            _const_spec((D_FF, D_MODEL)),
            _const_spec((1, D_MODEL)),
            _const_spec((1, D_MODEL)),
        ],
        out_specs=_row_spec(TOKEN_TILE, D_MODEL),
        out_shape=jax.ShapeDtypeStruct((t, D_MODEL), F32),
        scratch_shapes=[pltpu.VMEM((TOKEN_TILE, D_FF), BF16)],
        compiler_params=_params("parallel"),
    )(x, w_in, w_out, g, b)


def _ffn_ple_ln_kernel(h_ref, p_ref, w_in_ref, w_out_ref, wg_ref, wp_ref,
                       g2_ref, b2_ref, g3_ref, b3_ref, o_ref, act_ref):
    h = h_ref[...]
    f = _swiglu(h.astype(BF16), w_in_ref, w_out_ref, act_ref)
    h2 = _layer_norm(ALPHA * h + 0.5 * f, g2_ref[...], b2_ref[...])
    gate = jax.nn.sigmoid(_dot(h2.astype(BF16), wg_ref[...]))
    proj = _dot(p_ref[...].astype(BF16), wp_ref[...])
    o_ref[...] = _layer_norm(ALPHA * h2 + gate * proj, g3_ref[...], b3_ref[...])


def _ffn_ple_ln(h, p, w_in, w_out, wg, wp, g2, b2, g3, b3):
    t = h.shape[0]
    return pl.pallas_call(
        _ffn_ple_ln_kernel,
        name="ffn_ple_ln",
        grid=(t // TOKEN_TILE,),
        in_specs=[
            _row_spec(TOKEN_TILE, D_MODEL),
            _row_spec(TOKEN_TILE, D_PLE),
            _const_spec((D_MODEL, 2 * D_FF)),
            _const_spec((D_FF, D_MODEL)),
            _const_spec((D_MODEL, D_MODEL)),
            _const_spec((D_PLE, D_MODEL)),
            _const_spec((1, D_MODEL)),
            _const_spec((1, D_MODEL)),
            _const_spec((1, D_MODEL)),
            _const_spec((1, D_MODEL)),
        ],
        out_specs=_row_spec(TOKEN_TILE, D_MODEL),
        out_shape=jax.ShapeDtypeStruct((t, D_MODEL), F32),
        scratch_shapes=[pltpu.VMEM((TOKEN_TILE, D_FF), BF16)],
        compiler_params=_params("parallel"),
    )(h, p, w_in, w_out, wg, wp, g2, b2, g3, b3)


def _retention_kernel(h_ref, pos_ref, invf_ref, xi_ref, zeta_ref, dmask_ref, cdec_ref, w_ref,
                      o_ref, rq_ref, rqx_ref, rk_ref, rkz_ref, rv_ref, sg_ref, state_ref,
                      *, tiles_per_seq):
    @pl.when(pl.program_id(0) % tiles_per_seq == 0)
    def _():
        state_ref[...] = jnp.zeros_like(state_ref)

    hb = h_ref[...].astype(BF16)
    ang = pos_ref[...] * invf_ref[...]
    cos = jnp.cos(ang)
    lane = lax.broadcasted_iota(jnp.int32, (1, LANES), 1)
    sin = jnp.sin(ang) * jnp.where(lane < RET_DK // 2, -1.0, 1.0)
    k_scale = RET_DK ** -0.5
    g = _dot(hb, w_ref[:, 2 * RET_QK + RET_V:2 * RET_QK + 2 * RET_V])
    sg_ref[...] = (g * jax.nn.sigmoid(g)).astype(BF16)
    q = _dot(hb, w_ref[:, 0:RET_QK])
    k = _dot(hb, w_ref[:, RET_QK:2 * RET_QK])
    for h in range(HEADS):
        sl = slice(h * RET_DK, (h + 1) * RET_DK)
        tq = q[:, sl]
        tq = tq * cos + pltpu.roll(tq, RET_DK // 2, 1) * sin
        rq_ref[:, sl] = tq.astype(BF16)
        rqx_ref[:, sl] = (tq * xi_ref[:, sl]).astype(BF16)
        tk = k[:, sl]
        tk = (tk * cos + pltpu.roll(tk, RET_DK // 2, 1) * sin) * k_scale
        rk_ref[:, sl] = tk.astype(BF16)
        rkz_ref[:, sl] = (tk * zeta_ref[:, sl]).astype(BF16)
    rv_ref[...] = _dot(hb, w_ref[:, 2 * RET_QK:2 * RET_QK + RET_V]).astype(BF16)

    def qk(rows, h):
        qs = slice(h * RET_DK, (h + 1) * RET_DK)
        return lax.dot_general(rq_ref[rows, qs], rk_ref[rows, qs], (((1,), (1,)), ((), ())),
                               preferred_element_type=F32)

    for blk in range(TOKEN_TILE // RET_BLOCK):
        rows = slice(blk * RET_BLOCK, (blk + 1) * RET_BLOCK)
        s_next = qk(rows, 0)
        for h in range(HEADS):
            qs = slice(h * RET_DK, (h + 1) * RET_DK)
            vs = slice(h * RET_DV, (h + 1) * RET_DV)
            v = rv_ref[rows, vs]
            state = state_ref[h]
            s = s_next
            if h + 1 < HEADS:
                s_next = qk(rows, h + 1)
            cross = _dot(rqx_ref[rows, qs], state.astype(BF16))
            upd = lax.dot_general(rkz_ref[rows, qs], v, (((0,), (0,)), ((), ())),
                                  preferred_element_type=F32)
            state_ref[h] = state * cdec_ref[h] + upd
            y = _dot((s * dmask_ref[h]).astype(BF16), v) + cross
            mu = jnp.mean(y, axis=-1, keepdims=True)
            d = y - mu
            var = jnp.mean(d * d, axis=-1, keepdims=True)
            o_ref[rows, vs] = (d * lax.rsqrt(var + EPS) * sg_ref[rows, vs].astype(F32)).astype(BF16)


def _retention(h, pos_b, invf, xi, zeta, dmask, cdec, w, seq_len):
    t = h.shape[0]
    return pl.pallas_call(
        functools.partial(_retention_kernel, tiles_per_seq=seq_len // TOKEN_TILE),
        name="retention",
        grid=(t // TOKEN_TILE,),
        in_specs=[
            _row_spec(TOKEN_TILE, D_MODEL),
            _row_spec(TOKEN_TILE, LANES),
            _const_spec((1, LANES)),
            _const_spec((TOKEN_TILE, RET_QK)),
            _const_spec((TOKEN_TILE, RET_QK)),
            _const_spec((HEADS, RET_BLOCK, RET_BLOCK)),
            _const_spec((HEADS, 1, RET_DV)),
            _const_spec((D_MODEL, 2 * RET_QK + 2 * RET_V)),
        ],
        out_specs=_row_spec(TOKEN_TILE, RET_V),
        out_shape=jax.ShapeDtypeStruct((t, RET_V), BF16),
        scratch_shapes=[pltpu.VMEM((TOKEN_TILE, RET_QK), BF16)] * 4
        + [pltpu.VMEM((TOKEN_TILE, RET_V), BF16)] * 2
        + [pltpu.VMEM((HEADS, RET_DK, RET_DV), F32)],
        compiler_params=_params("arbitrary"),
    )(h, pos_b, invf, xi, zeta, dmask, cdec, w)


MLA_LAT = Q_LORA + KV_LORA + 2 * LANES


def _mla_proj_kernel(h_ref, pos_ref, invf_ref, w_ref, qg_ref, kvg_ref, wuq_ref, wuk_ref, wuvt_ref,
                     qcat_ref, kcat_ref, vt_ref):
    hb = h_ref[...].astype(BF16)
    ang = pos_ref[...] * invf_ref[...]
    cos = jnp.cos(ang)
    sin = jnp.sin(ang)
    lane = lax.broadcasted_iota(jnp.int32, (1, LANES), 1)
    first_half = lane < MLA_ROPE

    lat = _dot(hb, w_ref[:, 0:MLA_LAT])
    cq = lat[:, 0:Q_LORA]
    ckv = lat[:, Q_LORA:Q_LORA + KV_LORA]
    kpe = lat[:, Q_LORA + KV_LORA:Q_LORA + KV_LORA + LANES]
    kpe_rot = lat[:, Q_LORA + KV_LORA + LANES:MLA_LAT]

    qf = _dot(_rms_norm(cq, qg_ref[...]).astype(BF16), wuq_ref[...])
    nope_w = HEADS * MLA_NOPE
    rope_w = HEADS * MLA_ROPE
    for j in range(HEADS // 2):
        r = (qf[:, nope_w + j * LANES:nope_w + (j + 1) * LANES] * cos
             + qf[:, nope_w + rope_w + j * LANES:nope_w + rope_w + (j + 1) * LANES] * sin)
        base = 2 * j * MLA_QK
        qcat_ref[:, base + MLA_NOPE:base + MLA_QK] = jnp.where(first_half, r, 0.0).astype(BF16)
        base += MLA_QK
        qcat_ref[:, base + MLA_NOPE:base + MLA_QK] = jnp.where(first_half, 0.0, r).astype(BF16)
    for h in range(HEADS):
        qcat_ref[:, h * MLA_QK:h * MLA_QK + MLA_NOPE] = (
            qf[:, h * MLA_NOPE:(h + 1) * MLA_NOPE].astype(BF16))

    ckvn = _rms_norm(ckv, kvg_ref[...])
    kv = _dot(ckvn.astype(BF16), wuk_ref[...])
    vt_ref[0] = _dot(wuvt_ref[...], ckvn.T.astype(BF16)).astype(BF16)
    kr = kpe * cos + kpe_rot * sin
    kr_even = jnp.where(first_half, kr, 0.0).astype(BF16)
    kr_odd = jnp.where(first_half, 0.0, kr).astype(BF16)
    for h in range(HEADS):
        kcat_ref[:, h * MLA_QK:h * MLA_QK + MLA_NOPE] = (
            kv[:, h * MLA_NOPE:(h + 1) * MLA_NOPE].astype(BF16))
        kcat_ref[:, h * MLA_QK + MLA_NOPE:(h + 1) * MLA_QK] = kr_even if h % 2 == 0 else kr_odd


def _mla_proj(h, pos_b, invf, w, qg, kvg, wuq, wuk, wuvt):
    t = h.shape[0]
    uq_cols = HEADS * (MLA_NOPE + 2 * MLA_ROPE)
    assert TOKEN_TILE == ATT_K
    return pl.pallas_call(
        _mla_proj_kernel,
        name="mla_proj",
        grid=(t // TOKEN_TILE,),
        in_specs=[
            _row_spec(TOKEN_TILE, D_MODEL),
            _row_spec(TOKEN_TILE, LANES),
            _const_spec((1, LANES)),
            _const_spec((D_MODEL, MLA_LAT)),
            _const_spec((1, Q_LORA)),
            _const_spec((1, KV_LORA)),
            _const_spec((Q_LORA, uq_cols)),
            _const_spec((KV_LORA, HEADS * MLA_NOPE)),
            _const_spec((HEADS * MLA_DV, KV_LORA)),
        ],
        out_specs=[
            _row_spec(TOKEN_TILE, HEADS * MLA_QK),
            _row_spec(TOKEN_TILE, HEADS * MLA_QK),
            pl.BlockSpec((1, HEADS * MLA_DV, TOKEN_TILE), lambda i: (i, 0, 0)),
        ],
        out_shape=[
            jax.ShapeDtypeStruct((t, HEADS * MLA_QK), BF16),
            jax.ShapeDtypeStruct((t, HEADS * MLA_QK), BF16),
            jax.ShapeDtypeStruct((t // TOKEN_TILE, HEADS * MLA_DV, TOKEN_TILE), BF16),
        ],
        compiler_params=_params("parallel"),
    )(h, pos_b, invf, w, qg, kvg, wuq, wuk, wuvt)


def _softmax_step(st, mx, vt, stats):
    m, l, acc = stats
    m_new = jnp.maximum(m, mx)
    a = jnp.exp2(m - m_new)
    p = jnp.exp2(st - m_new)
    l = a * l + jnp.sum(p, axis=0, keepdims=True)
    acc = a * acc + _dot(vt, p.astype(BF16))
    return m_new, l, acc


def _mla_attn_kernel(q_ref, k_ref, vt_ref, o_ref, s0_ref, s1_ref, mx0_ref, mx1_ref,
                     m_ref, l_ref, acc_ref):
    qi = pl.program_id(2)
    heads = range(ATT_HEADS)

    def scores(j, s_ref, mx_ref):
        off = pl.multiple_of(j * ATT_K, ATT_K)
        for hh in heads:
            st = lax.dot_general(
                k_ref[0, pl.ds(off, ATT_K), hh * MLA_QK:(hh + 1) * MLA_QK],
                q_ref[0, :, hh * MLA_QK:(hh + 1) * MLA_QK],
                (((1,), (1,)), ((), ())), preferred_element_type=F32)
            s_ref[hh] = st
            mx_ref[hh] = jnp.max(st, axis=0, keepdims=True)

    def consume(j, s_ref, mx_ref, mask=None):
        for hh in heads:
            st = s_ref[hh]
            if mask is None:
                mx = mx_ref[hh]
            else:
                st = jnp.where(mask, st, -jnp.inf)
                mx = jnp.max(st, axis=0, keepdims=True)
            m, l, acc = _softmax_step(st, mx, vt_ref[0, j, hh * MLA_DV:(hh + 1) * MLA_DV, :],
                                      (m_ref[hh], l_ref[hh], acc_ref[hh]))
            m_ref[hh] = m
            l_ref[hh] = l
            acc_ref[hh] = acc

    m_ref[...] = jnp.full_like(m_ref, NEG_BIG)
    l_ref[...] = jnp.zeros_like(l_ref)
    acc_ref[...] = jnp.zeros_like(acc_ref)

    scores(0, s0_ref, mx0_ref)

    def pair(jj, c):
        j = 2 * jj
        scores(j + 1, s1_ref, mx1_ref)
        consume(j, s0_ref, mx0_ref)
        scores(j + 2, s0_ref, mx0_ref)
        consume(j + 1, s1_ref, mx1_ref)
        return c

    lax.fori_loop(0, qi // 2, pair, 0)

    key_chunk = lax.broadcasted_iota(jnp.int32, (ATT_K, ATT_Q), 0) // CHUNK
    qry_chunk = lax.broadcasted_iota(jnp.int32, (ATT_K, ATT_Q), 1) // CHUNK
    mask = key_chunk <= qry_chunk

    @pl.when(qi % 2 == 0)
    def _():
        consume(qi, s0_ref, mx0_ref, mask)

    @pl.when(qi % 2 == 1)
    def _():
        scores(qi, s1_ref, mx1_ref)
        consume(qi - 1, s0_ref, mx0_ref)
        consume(qi, s1_ref, mx1_ref, mask)

    for hh in heads:
        o_ref[0, :, hh * MLA_DV:(hh + 1) * MLA_DV] = (acc_ref[hh] / l_ref[hh]).T.astype(BF16)


def _mla_attn(qcat, kcat, vt):
    b, s, _ = qcat.shape
    assert ATT_Q == ATT_K
    n_kb = s // ATT_K
    return pl.pallas_call(
        _mla_attn_kernel,
        name="mla_attn",
        grid=(b, HEADS // ATT_HEADS, s // ATT_Q),
        in_specs=[
            pl.BlockSpec((1, ATT_Q, ATT_HEADS * MLA_QK), lambda i, h, j: (i, j, h)),
            pl.BlockSpec((1, s, ATT_HEADS * MLA_QK), lambda i, h, j: (i, 0, h)),
            pl.BlockSpec((1, n_kb, ATT_HEADS * MLA_DV, ATT_K), lambda i, h, j: (i, 0, h, 0)),
        ],
        out_specs=pl.BlockSpec((1, ATT_Q, ATT_HEADS * MLA_DV), lambda i, h, j: (i, j, h)),
        out_shape=jax.ShapeDtypeStruct((b, s, HEADS * MLA_DV), BF16),
        scratch_shapes=[
            pltpu.VMEM((ATT_HEADS, ATT_K, ATT_Q), F32),
            pltpu.VMEM((ATT_HEADS, ATT_K, ATT_Q), F32),
            pltpu.VMEM((ATT_HEADS, 1, ATT_Q), F32),
            pltpu.VMEM((ATT_HEADS, 1, ATT_Q), F32),
            pltpu.VMEM((ATT_HEADS, 1, ATT_Q), F32),
            pltpu.VMEM((ATT_HEADS, 1, ATT_Q), F32),
            pltpu.VMEM((ATT_HEADS, MLA_DV, ATT_Q), F32),
        ],
        compiler_params=_params("parallel", "parallel", "arbitrary"),
    )(qcat, kcat, vt)


def _merge_ln_kernel(yr_ref, om_ref, h_ref, wg_ref, wro_ref, wmo_ref, wout_ref,
                     g_ref, b_ref, o_ref):
    h = h_ref[...]
    hb = h.astype(BF16)
    mix = (jax.nn.sigmoid(_dot(hb, wg_ref[:, 0:D_MODEL])) * _dot(yr_ref[...], wro_ref[...])
           + jax.nn.sigmoid(_dot(hb, wg_ref[:, D_MODEL:2 * D_MODEL])) * _dot(om_ref[...], wmo_ref[...]))
    mixed = _dot(mix.astype(BF16), wout_ref[...])
    o_ref[...] = _layer_norm(ALPHA * h + mixed, g_ref[...], b_ref[...])


def _merge_ln(yr, om, h, wg, wro, wmo, wout, g, b):
    t = h.shape[0]
    return pl.pallas_call(
        _merge_ln_kernel,
        name="merge_ln",
        grid=(t // TOKEN_TILE,),
        in_specs=[
            _row_spec(TOKEN_TILE, RET_V),
            _row_spec(TOKEN_TILE, HEADS * MLA_DV),
            _row_spec(TOKEN_TILE, D_MODEL),
            _const_spec((D_MODEL, 2 * D_MODEL)),
            _const_spec((RET_V, D_MODEL)),
            _const_spec((HEADS * MLA_DV, D_MODEL)),
            _const_spec((D_MODEL, D_MODEL)),
            _const_spec((1, D_MODEL)),
            _const_spec((1, D_MODEL)),
        ],
        out_specs=_row_spec(TOKEN_TILE, D_MODEL),
        out_shape=jax.ShapeDtypeStruct((t, D_MODEL), F32),
        compiler_params=_params("parallel"),
    )(yr, om, h, wg, wro, wmo, wout, g, b)


def _retention_tables():
    log_gamma = jnp.log(1.0 - 2.0 ** (-5.0 - jnp.arange(HEADS, dtype=F32)))
    idx = jnp.arange(RET_BLOCK, dtype=F32)
    chunk = jnp.arange(RET_BLOCK) // CHUNK
    visible = chunk[None, :] <= chunk[:, None]
    dist = jnp.abs(idx[:, None] - idx[None, :])
    dmask = jnp.where(visible[None], jnp.exp(log_gamma[:, None, None] * dist[None]), 0.0)
    xi = jnp.exp(log_gamma[:, None] * (idx + 1.0))
    zeta = jnp.exp(log_gamma[:, None] * (RET_BLOCK - 1.0 - idx))
    cdec = jnp.exp(log_gamma * RET_BLOCK)
    def per_token_rows(tab):
        rows = jnp.broadcast_to(tab.T[:, :, None], (RET_BLOCK, HEADS, RET_DK))
        return jnp.tile(rows.reshape(RET_BLOCK, RET_QK), (TOKEN_TILE // RET_BLOCK, 1))

    return (dmask, per_token_rows(xi), per_token_rows(zeta),
            jnp.broadcast_to(cdec[:, None, None], (HEADS, 1, RET_DV)))


def _rot_half_cols(w, half):
    return jnp.concatenate([-w[:, half:], w[:, :half]], axis=1)


def kernel(x, p, positions, ln_g, ln_b, ffn1_w_in, ffn1_w_out, w_in, ret_gn_g, w_ret_o,
           q_norm_g, kv_norm_g, w_uq, w_ukv, w_mla_o, w_out, ffn2_w_in, ffn2_w_out,
           ple_w_gate, ple_w_proj):
    b, s, d = x.shape
    t = b * s
    h = x.reshape(t, d)
    pos_b = jnp.broadcast_to(positions.astype(F32).reshape(t, 1), (t, LANES))
    row = lambda v: v.reshape(1, -1)

    for i in range(ln_g.shape[0]):
        wi = w_in[i]
        c0 = 2 * RET_QK + 2 * RET_V
        w_ret = wi[:, :c0].astype(BF16)
        w_cq = wi[:, c0:c0 + Q_LORA]
        w_ckv = wi[:, c0 + Q_LORA:c0 + Q_LORA + KV_LORA]
        c1 = c0 + Q_LORA + KV_LORA
        w_kpe = wi[:, c1:c1 + MLA_ROPE]
        w_kpe_rot = _rot_half_cols(w_kpe, MLA_ROPE // 2)
        w_gates = wi[:, c1 + MLA_ROPE:].astype(BF16)
        w_mla = jnp.concatenate(
            [w_cq, w_ckv, w_kpe, w_kpe, w_kpe_rot, w_kpe_rot], axis=1).astype(BF16)

        q_scale = (MLA_NOPE + MLA_ROPE) ** -0.5 * math.log2(math.e)
        uq = w_uq[i].reshape(Q_LORA, HEADS, MLA_NOPE + MLA_ROPE) * q_scale
        uq_nope = uq[:, :, :MLA_NOPE].reshape(Q_LORA, HEADS * MLA_NOPE)
        uq_rope = uq[:, :, MLA_NOPE:]
        uq_rot = jnp.concatenate(
            [-uq_rope[:, :, MLA_ROPE // 2:], uq_rope[:, :, :MLA_ROPE // 2]], axis=2)
        wuq = jnp.concatenate(
            [uq_nope, uq_rope.reshape(Q_LORA, -1), uq_rot.reshape(Q_LORA, -1)], axis=1).astype(BF16)
        ukv = w_ukv[i].reshape(KV_LORA, HEADS, MLA_NOPE + MLA_DV)
        wuk = ukv[:, :, :MLA_NOPE].reshape(KV_LORA, -1).astype(BF16)
        wuvt = ukv[:, :, MLA_NOPE:].reshape(KV_LORA, -1).T.astype(BF16)

        invf_ret = ROPE_BASE ** (-jnp.arange(RET_DK // 2, dtype=F32) / (RET_DK // 2))
        invf_ret = jnp.tile(invf_ret, 2).reshape(1, LANES)
        invf_mla = ROPE_BASE ** (-jnp.arange(MLA_ROPE // 2, dtype=F32) / (MLA_ROPE // 2))
        invf_mla = jnp.tile(invf_mla, 4).reshape(1, LANES)

        h = _ffn_ln(h, ffn1_w_in[i].astype(BF16), ffn1_w_out[i].astype(BF16),
                    row(ln_g[i, 0]), row(ln_b[i, 0]))

        dmask, xi, zeta, cdec = _retention_tables()
        y_ret = _retention(h, pos_b, invf_ret, xi, zeta, dmask, cdec, w_ret, s)
        qcat, kcat, vt = _mla_proj(
            h, pos_b, invf_mla, w_mla, row(q_norm_g[i]), row(kv_norm_g[i]), wuq, wuk, wuvt)
        o_mla = _mla_attn(qcat.reshape(b, s, -1), kcat.reshape(b, s, -1),
                          vt.reshape(b, s // ATT_K, HEADS * MLA_DV, ATT_K))
        w_ro = (ret_gn_g[i][:, None] * w_ret_o[i]).astype(BF16)
        h = _merge_ln(y_ret, o_mla.reshape(t, -1), h, w_gates,
                      w_ro, w_mla_o[i].astype(BF16), w_out[i].astype(BF16),
                      row(ln_g[i, 1]), row(ln_b[i, 1]))

        h = _ffn_ple_ln(h, p[i].reshape(t, -1), ffn2_w_in[i].astype(BF16),
                        ffn2_w_out[i].astype(BF16), ple_w_gate[i].astype(BF16),
                        ple_w_proj[i].astype(BF16), row(ln_g[i, 2]), row(ln_b[i, 2]),
                        row(ln_g[i, 3]), row(ln_b[i, 3]))
    return h.reshape(b, s, d)
```

```python
import functools
import math

import jax
import jax.numpy as jnp
from jax import lax
from jax.experimental import pallas as pl
from jax.experimental.pallas import tpu as pltpu

F32 = jnp.float32
BF16 = jnp.bfloat16

D_MODEL = 1024
CHUNK = 64
D_PLE = 256
D_FF = 2816
HEADS = 8
RET_DK = 128
RET_DV = 256
RET_QK = HEADS * RET_DK
RET_V = HEADS * RET_DV
MLA_NOPE = 128
MLA_ROPE = 64
MLA_DV = 128
Q_LORA = 256
KV_LORA = 256
ROPE_BASE = 10000.0
EPS = 1e-5
ALPHA = 2.0 ** 0.25

LANES = 128
FF_CHUNK = 256
TOKEN_TILE = 512
RET_BLOCK = 256
ATT_Q = 512
ATT_K = 512
ATT_HEADS = 2
MLA_QK = 256
VMEM_LIMIT = 52 * 1024 * 1024
NEG_BIG = -1e30


def _const_spec(shape):
    nd = len(shape)
    return pl.BlockSpec(shape, lambda *_: (0,) * nd, pipeline_mode=pl.Buffered(1))


def _row_spec(tile, width):
    return pl.BlockSpec((tile, width), lambda i: (i, 0))


def _params(*sem):
    return pltpu.CompilerParams(dimension_semantics=sem, vmem_limit_bytes=VMEM_LIMIT)


def _dot(a, b):
    return jnp.dot(a, b, preferred_element_type=F32)


def _layer_norm(y, g, b):
    mu = jnp.mean(y, axis=-1, keepdims=True)
    d = y - mu
    var = jnp.mean(d * d, axis=-1, keepdims=True)
    return d * lax.rsqrt(var + EPS) * g + b


def _rms_norm(x, g):
    return x * lax.rsqrt(jnp.mean(x * x, axis=-1, keepdims=True) + EPS) * g


def _swiglu(xb, w_in_ref, w_out_ref, act_ref):
    for c in range(D_FF // FF_CHUNK):
        lo = c * FF_CHUNK
        g = _dot(xb, w_in_ref[:, lo:lo + FF_CHUNK])
        u = _dot(xb, w_in_ref[:, D_FF + lo:D_FF + lo + FF_CHUNK])
        act_ref[:, lo:lo + FF_CHUNK] = (g * jax.nn.sigmoid(g) * u).astype(BF16)
    return _dot(act_ref[...], w_out_ref[...])


def _ffn_ln_kernel(x_ref, w_in_ref, w_out_ref, g_ref, b_ref, o_ref, act_ref):
    x = x_ref[...]
    f = _swiglu(x.astype(BF16), w_in_ref, w_out_ref, act_ref)
    o_ref[...] = _layer_norm(ALPHA * x + 0.5 * f, g_ref[...], b_ref[...])


def _ffn_ln(x, w_in, w_out, g, b):
    t = x.shape[0]
    return pl.pallas_call(
        _ffn_ln_kernel,
        name="ffn_ln",
        grid=(t // TOKEN_TILE,),
        in_specs=[
            _row_spec(TOKEN_TILE, D_MODEL),
            _const_spec((D_MODEL, 2 * D_FF)),
            _const_spec((D_FF, D_MODEL)),
            _const_spec((1, D_MODEL)),
            _const_spec((1, D_MODEL)),
        ],
        out_specs=_row_spec(TOKEN_TILE, D_MODEL),
        out_shape=jax.ShapeDtypeStruct((t, D_MODEL), F32),
        scratch_shapes=[pltpu.VMEM((TOKEN_TILE, D_FF), BF16)],
        compiler_params=_params("parallel"),
    )(x, w_in, w_out, g, b)


def _ffn_ple_ln_kernel(h_ref, p_ref, w_in_ref, w_out_ref, wg_ref, wp_ref,
                       g2_ref, b2_ref, g3_ref, b3_ref, o_ref, act_ref):
    h = h_ref[...]
    f = _swiglu(h.astype(BF16), w_in_ref, w_out_ref, act_ref)
    h2 = _layer_norm(ALPHA * h + 0.5 * f, g2_ref[...], b2_ref[...])
    gate = jax.nn.sigmoid(_dot(h2.astype(BF16), wg_ref[...]))
    proj = _dot(p_ref[...].astype(BF16), wp_ref[...])
    o_ref[...] = _layer_norm(ALPHA * h2 + gate * proj, g3_ref[...], b3_ref[...])


def _ffn_ple_ln(h, p, w_in, w_out, wg, wp, g2, b2, g3, b3):
    t = h.shape[0]
    return pl.pallas_call(
        _ffn_ple_ln_kernel,
        name="ffn_ple_ln",
        grid=(t // TOKEN_TILE,),
        in_specs=[
            _row_spec(TOKEN_TILE, D_MODEL),
            _row_spec(TOKEN_TILE, D_PLE),
            _const_spec((D_MODEL, 2 * D_FF)),
            _const_spec((D_FF, D_MODEL)),
            _const_spec((D_MODEL, D_MODEL)),
            _const_spec((D_PLE, D_MODEL)),
            _const_spec((1, D_MODEL)),
            _const_spec((1, D_MODEL)),
            _const_spec((1, D_MODEL)),
            _const_spec((1, D_MODEL)),
        ],
        out_specs=_row_spec(TOKEN_TILE, D_MODEL),
        out_shape=jax.ShapeDtypeStruct((t, D_MODEL), F32),
        scratch_shapes=[pltpu.VMEM((TOKEN_TILE, D_FF), BF16)],
        compiler_params=_params("parallel"),
    )(h, p, w_in, w_out, wg, wp, g2, b2, g3, b3)


def _rope_tables(pos_ref, invf_ref, dcos_ref, dsin_ref, cos_ref, sin_ref):
    pos = pos_ref[...]
    invf = invf_ref[...]
    base = pos[0:1, :]
    row = lax.broadcasted_iota(jnp.int32, pos.shape, 0).astype(F32)
    exact_f32_int = 2.0 ** 24 - pos.shape[0]
    consecutive = jnp.logical_and(jnp.all(pos == base + row),
                                  jnp.all(jnp.abs(base) < exact_f32_int))

    @pl.when(consecutive)
    def _():
        a = base * invf
        ca = jnp.cos(a)
        sa = jnp.sin(a)
        cos_ref[...] = ca * dcos_ref[...] - sa * dsin_ref[...]
        sin_ref[...] = sa * dcos_ref[...] + ca * dsin_ref[...]

    @pl.when(jnp.logical_not(consecutive))
    def _():
        ang = pos * invf
        cos_ref[...] = jnp.cos(ang)
        sin_ref[...] = jnp.sin(ang)


def _retention_kernel(h_ref, pos_ref, invf_ref, dcos_ref, dsin_ref, xi_ref, zeta_ref, dmask_ref,
                      cdec_ref, w_ref, o_ref, rq_ref, rqx_ref, rk_ref, rkz_ref, rv_ref, sg_ref,
                      state_ref, cos_ref, sin_ref, *, tiles_per_seq):
    @pl.when(pl.program_id(0) % tiles_per_seq == 0)
    def _():
        state_ref[...] = jnp.zeros_like(state_ref)

    _rope_tables(pos_ref, invf_ref, dcos_ref, dsin_ref, cos_ref, sin_ref)
    hb = h_ref[...].astype(BF16)
    cos = cos_ref[...]
    lane = lax.broadcasted_iota(jnp.int32, (1, LANES), 1)
    sin = sin_ref[...] * jnp.where(lane < RET_DK // 2, -1.0, 1.0)
    k_scale = RET_DK ** -0.5
    g = _dot(hb, w_ref[:, 2 * RET_QK + RET_V:2 * RET_QK + 2 * RET_V])
    sg_ref[...] = (g * jax.nn.sigmoid(g)).astype(BF16)
    q = _dot(hb, w_ref[:, 0:RET_QK])
    k = _dot(hb, w_ref[:, RET_QK:2 * RET_QK])
    for h in range(HEADS):
        sl = slice(h * RET_DK, (h + 1) * RET_DK)
        tq = q[:, sl]
        tq = tq * cos + pltpu.roll(tq, RET_DK // 2, 1) * sin
        rq_ref[:, sl] = tq.astype(BF16)
        rqx_ref[:, sl] = (tq * xi_ref[:, sl]).astype(BF16)
        tk = k[:, sl]
        tk = (tk * cos + pltpu.roll(tk, RET_DK // 2, 1) * sin) * k_scale
        rk_ref[:, sl] = tk.astype(BF16)
        rkz_ref[:, sl] = (tk * zeta_ref[:, sl]).astype(BF16)
    rv_ref[...] = _dot(hb, w_ref[:, 2 * RET_QK:2 * RET_QK + RET_V]).astype(BF16)

    def qk(rows, h):
        qs = slice(h * RET_DK, (h + 1) * RET_DK)
        return lax.dot_general(rq_ref[rows, qs], rk_ref[rows, qs], (((1,), (1,)), ((), ())),
                               preferred_element_type=F32)

    for blk in range(TOKEN_TILE // RET_BLOCK):
        rows = slice(blk * RET_BLOCK, (blk + 1) * RET_BLOCK)
        s_next = qk(rows, 0)
        for h in range(HEADS):
            qs = slice(h * RET_DK, (h + 1) * RET_DK)
            vs = slice(h * RET_DV, (h + 1) * RET_DV)
            v = rv_ref[rows, vs]
            state = state_ref[h]
            s = s_next
            if h + 1 < HEADS:
                s_next = qk(rows, h + 1)
            cross = _dot(rqx_ref[rows, qs], state.astype(BF16))
            upd = lax.dot_general(rkz_ref[rows, qs], v, (((0,), (0,)), ((), ())),
                                  preferred_element_type=F32)
            state_ref[h] = state * cdec_ref[h] + upd
            y = _dot((s * dmask_ref[h]).astype(BF16), v) + cross
            mu = jnp.mean(y, axis=-1, keepdims=True)
            d = y - mu
            var = jnp.mean(d * d, axis=-1, keepdims=True)
            o_ref[rows, vs] = (d * lax.rsqrt(var + EPS) * sg_ref[rows, vs].astype(F32)).astype(BF16)


def _retention(h, pos_b, rope, xi, zeta, dmask, cdec, w, seq_len):
    t = h.shape[0]
    return pl.pallas_call(
        functools.partial(_retention_kernel, tiles_per_seq=seq_len // TOKEN_TILE),
        name="retention",
        grid=(t // TOKEN_TILE,),
        in_specs=[
            _row_spec(TOKEN_TILE, D_MODEL),
            _row_spec(TOKEN_TILE, LANES),
            _const_spec((1, LANES)),
            _const_spec((TOKEN_TILE, LANES)),
            _const_spec((TOKEN_TILE, LANES)),
            _const_spec((TOKEN_TILE, RET_QK)),
            _const_spec((TOKEN_TILE, RET_QK)),
            _const_spec((HEADS, RET_BLOCK, RET_BLOCK)),
            _const_spec((HEADS, 1, RET_DV)),
            _const_spec((D_MODEL, 2 * RET_QK + 2 * RET_V)),
        ],
        out_specs=_row_spec(TOKEN_TILE, RET_V),
        out_shape=jax.ShapeDtypeStruct((t, RET_V), BF16),
        scratch_shapes=[pltpu.VMEM((TOKEN_TILE, RET_QK), BF16)] * 4
        + [pltpu.VMEM((TOKEN_TILE, RET_V), BF16)] * 2
        + [pltpu.VMEM((HEADS, RET_DK, RET_DV), F32)]
        + [pltpu.VMEM((TOKEN_TILE, LANES), F32)] * 2,
        compiler_params=_params("arbitrary"),
    )(h, pos_b, *rope, xi, zeta, dmask, cdec, w)


MLA_LAT = Q_LORA + KV_LORA + 2 * LANES


def _mla_proj_kernel(h_ref, pos_ref, invf_ref, dcos_ref, dsin_ref, w_ref, qg_ref, kvg_ref,
                     wuq_ref, wuk_ref, wuvt_ref, qcat_ref, kcat_ref, vt_ref, cos_ref, sin_ref):
    _rope_tables(pos_ref, invf_ref, dcos_ref, dsin_ref, cos_ref, sin_ref)
    hb = h_ref[...].astype(BF16)
    cos = cos_ref[...]
    sin = sin_ref[...]
    lane = lax.broadcasted_iota(jnp.int32, (1, LANES), 1)
    first_half = lane < MLA_ROPE

    lat = _dot(hb, w_ref[:, 0:MLA_LAT])
    cq = lat[:, 0:Q_LORA]
    ckv = lat[:, Q_LORA:Q_LORA + KV_LORA]
    kpe = lat[:, Q_LORA + KV_LORA:Q_LORA + KV_LORA + LANES]
    kpe_rot = lat[:, Q_LORA + KV_LORA + LANES:MLA_LAT]

    qf = _dot(_rms_norm(cq, qg_ref[...]).astype(BF16), wuq_ref[...])
    nope_w = HEADS * MLA_NOPE
    rope_w = HEADS * MLA_ROPE
    for j in range(HEADS // 2):
        r = (qf[:, nope_w + j * LANES:nope_w + (j + 1) * LANES] * cos
             + qf[:, nope_w + rope_w + j * LANES:nope_w + rope_w + (j + 1) * LANES] * sin)
        base = 2 * j * MLA_QK
        qcat_ref[:, base + MLA_NOPE:base + MLA_QK] = jnp.where(first_half, r, 0.0).astype(BF16)
        base += MLA_QK
        qcat_ref[:, base + MLA_NOPE:base + MLA_QK] = jnp.where(first_half, 0.0, r).astype(BF16)
    for h in range(HEADS):
        qcat_ref[:, h * MLA_QK:h * MLA_QK + MLA_NOPE] = (
            qf[:, h * MLA_NOPE:(h + 1) * MLA_NOPE].astype(BF16))

    ckvn = _rms_norm(ckv, kvg_ref[...])
    kv = _dot(ckvn.astype(BF16), wuk_ref[...])
    vt_ref[0] = _dot(wuvt_ref[...], ckvn.T.astype(BF16)).astype(BF16)
    kr = kpe * cos + kpe_rot * sin
    kr_even = jnp.where(first_half, kr, 0.0).astype(BF16)
    kr_odd = jnp.where(first_half, 0.0, kr).astype(BF16)
    for h in range(HEADS):
        kcat_ref[:, h * MLA_QK:h * MLA_QK + MLA_NOPE] = (
            kv[:, h * MLA_NOPE:(h + 1) * MLA_NOPE].astype(BF16))
        kcat_ref[:, h * MLA_QK + MLA_NOPE:(h + 1) * MLA_QK] = kr_even if h % 2 == 0 else kr_odd


def _mla_proj(h, pos_b, rope, w, qg, kvg, wuq, wuk, wuvt):
    t = h.shape[0]
    uq_cols = HEADS * (MLA_NOPE + 2 * MLA_ROPE)
    assert TOKEN_TILE == ATT_K
    return pl.pallas_call(
        _mla_proj_kernel,
        name="mla_proj",
        grid=(t // TOKEN_TILE,),
        in_specs=[
            _row_spec(TOKEN_TILE, D_MODEL),
            _row_spec(TOKEN_TILE, LANES),
            _const_spec((1, LANES)),
            _const_spec((TOKEN_TILE, LANES)),
            _const_spec((TOKEN_TILE, LANES)),
            _const_spec((D_MODEL, MLA_LAT)),
            _const_spec((1, Q_LORA)),
            _const_spec((1, KV_LORA)),
            _const_spec((Q_LORA, uq_cols)),
            _const_spec((KV_LORA, HEADS * MLA_NOPE)),
            _const_spec((HEADS * MLA_DV, KV_LORA)),
        ],
        out_specs=[
            _row_spec(TOKEN_TILE, HEADS * MLA_QK),
            _row_spec(TOKEN_TILE, HEADS * MLA_QK),
            pl.BlockSpec((1, HEADS * MLA_DV, TOKEN_TILE), lambda i: (i, 0, 0)),
        ],
        out_shape=[
            jax.ShapeDtypeStruct((t, HEADS * MLA_QK), BF16),
            jax.ShapeDtypeStruct((t, HEADS * MLA_QK), BF16),
            jax.ShapeDtypeStruct((t // TOKEN_TILE, HEADS * MLA_DV, TOKEN_TILE), BF16),
        ],
        scratch_shapes=[pltpu.VMEM((TOKEN_TILE, LANES), F32)] * 2,
        compiler_params=_params("parallel"),
    )(h, pos_b, *rope, w, qg, kvg, wuq, wuk, wuvt)


def _softmax_step(st, mx, vt, stats):
    m, l, acc = stats
    m_new = jnp.maximum(m, mx)
    a = jnp.exp2(m - m_new)
    p = jnp.exp2(st - m_new)
    l = a * l + jnp.sum(p, axis=0, keepdims=True)
    acc = a * acc + _dot(vt, p.astype(BF16))
    return m_new, l, acc


def _mla_attn_kernel(q_ref, k_ref, vt_ref, o_ref, s0_ref, s1_ref, mx0_ref, mx1_ref,
                     m_ref, l_ref, acc_ref):
    qi = pl.program_id(2)
    heads = range(ATT_HEADS)

    def scores(j, s_ref, mx_ref):
        off = pl.multiple_of(j * ATT_K, ATT_K)
        for hh in heads:
            st = lax.dot_general(
                k_ref[0, pl.ds(off, ATT_K), hh * MLA_QK:(hh + 1) * MLA_QK],
                q_ref[0, :, hh * MLA_QK:(hh + 1) * MLA_QK],
                (((1,), (1,)), ((), ())), preferred_element_type=F32)
            s_ref[hh] = st
            mx_ref[hh] = jnp.max(st, axis=0, keepdims=True)

    def consume(j, s_ref, mx_ref, mask=None):
        for hh in heads:
            st = s_ref[hh]
            if mask is None:
                mx = mx_ref[hh]
            else:
                st = jnp.where(mask, st, -jnp.inf)
                mx = jnp.max(st, axis=0, keepdims=True)
            m, l, acc = _softmax_step(st, mx, vt_ref[0, j, hh * MLA_DV:(hh + 1) * MLA_DV, :],
                                      (m_ref[hh], l_ref[hh], acc_ref[hh]))
            m_ref[hh] = m
            l_ref[hh] = l
            acc_ref[hh] = acc

    m_ref[...] = jnp.full_like(m_ref, NEG_BIG)
    l_ref[...] = jnp.zeros_like(l_ref)
    acc_ref[...] = jnp.zeros_like(acc_ref)

    scores(0, s0_ref, mx0_ref)

    def pair(jj, c):
        j = 2 * jj
        scores(j + 1, s1_ref, mx1_ref)
        consume(j, s0_ref, mx0_ref)
        scores(j + 2, s0_ref, mx0_ref)
        consume(j + 1, s1_ref, mx1_ref)
        return c

    lax.fori_loop(0, qi // 2, pair, 0)

    key_chunk = lax.broadcasted_iota(jnp.int32, (ATT_K, ATT_Q), 0) // CHUNK
    qry_chunk = lax.broadcasted_iota(jnp.int32, (ATT_K, ATT_Q), 1) // CHUNK
    mask = key_chunk <= qry_chunk

    @pl.when(qi % 2 == 0)
    def _():
        consume(qi, s0_ref, mx0_ref, mask)

    @pl.when(qi % 2 == 1)
    def _():
        scores(qi, s1_ref, mx1_ref)
        consume(qi - 1, s0_ref, mx0_ref)
        consume(qi, s1_ref, mx1_ref, mask)

    for hh in heads:
        o_ref[0, :, hh * MLA_DV:(hh + 1) * MLA_DV] = (acc_ref[hh] / l_ref[hh]).T.astype(BF16)


def _mla_attn(qcat, kcat, vt):
    b, s, _ = qcat.shape
    assert ATT_Q == ATT_K
    n_kb = s // ATT_K
    return pl.pallas_call(
        _mla_attn_kernel,
        name="mla_attn",
        grid=(b, HEADS // ATT_HEADS, s // ATT_Q),
        in_specs=[
            pl.BlockSpec((1, ATT_Q, ATT_HEADS * MLA_QK), lambda i, h, j: (i, j, h)),
            pl.BlockSpec((1, s, ATT_HEADS * MLA_QK), lambda i, h, j: (i, 0, h)),
            pl.BlockSpec((1, n_kb, ATT_HEADS * MLA_DV, ATT_K), lambda i, h, j: (i, 0, h, 0)),
        ],
        out_specs=pl.BlockSpec((1, ATT_Q, ATT_HEADS * MLA_DV), lambda i, h, j: (i, j, h)),
        out_shape=jax.ShapeDtypeStruct((b, s, HEADS * MLA_DV), BF16),
        scratch_shapes=[
            pltpu.VMEM((ATT_HEADS, ATT_K, ATT_Q), F32),
            pltpu.VMEM((ATT_HEADS, ATT_K, ATT_Q), F32),
            pltpu.VMEM((ATT_HEADS, 1, ATT_Q), F32),
            pltpu.VMEM((ATT_HEADS, 1, ATT_Q), F32),
            pltpu.VMEM((ATT_HEADS, 1, ATT_Q), F32),
            pltpu.VMEM((ATT_HEADS, 1, ATT_Q), F32),
            pltpu.VMEM((ATT_HEADS, MLA_DV, ATT_Q), F32),
        ],
        compiler_params=_params("parallel", "parallel", "arbitrary"),
    )(qcat, kcat, vt)


def _merge_ln_kernel(yr_ref, om_ref, h_ref, wg_ref, wro_ref, wmo_ref, wout_ref,
                     g_ref, b_ref, o_ref):
    h = h_ref[...]
    hb = h.astype(BF16)
    mix = (jax.nn.sigmoid(_dot(hb, wg_ref[:, 0:D_MODEL])) * _dot(yr_ref[...], wro_ref[...])
           + jax.nn.sigmoid(_dot(hb, wg_ref[:, D_MODEL:2 * D_MODEL])) * _dot(om_ref[...], wmo_ref[...]))
    mixed = _dot(mix.astype(BF16), wout_ref[...])
    o_ref[...] = _layer_norm(ALPHA * h + mixed, g_ref[...], b_ref[...])


def _merge_ln(yr, om, h, wg, wro, wmo, wout, g, b):
    t = h.shape[0]
    return pl.pallas_call(
        _merge_ln_kernel,
        name="merge_ln",
        grid=(t // TOKEN_TILE,),
        in_specs=[
            _row_spec(TOKEN_TILE, RET_V),
            _row_spec(TOKEN_TILE, HEADS * MLA_DV),
            _row_spec(TOKEN_TILE, D_MODEL),
            _const_spec((D_MODEL, 2 * D_MODEL)),
            _const_spec((RET_V, D_MODEL)),
            _const_spec((HEADS * MLA_DV, D_MODEL)),
            _const_spec((D_MODEL, D_MODEL)),
            _const_spec((1, D_MODEL)),
            _const_spec((1, D_MODEL)),
        ],
        out_specs=_row_spec(TOKEN_TILE, D_MODEL),
        out_shape=jax.ShapeDtypeStruct((t, D_MODEL), F32),
        compiler_params=_params("parallel"),
    )(yr, om, h, wg, wro, wmo, wout, g, b)


def _retention_tables():
    log_gamma = jnp.log(1.0 - 2.0 ** (-5.0 - jnp.arange(HEADS, dtype=F32)))
    idx = jnp.arange(RET_BLOCK, dtype=F32)
    chunk = jnp.arange(RET_BLOCK) // CHUNK
    visible = chunk[None, :] <= chunk[:, None]
    dist = jnp.abs(idx[:, None] - idx[None, :])
    dmask = jnp.where(visible[None], jnp.exp(log_gamma[:, None, None] * dist[None]), 0.0)
    xi = jnp.exp(log_gamma[:, None] * (idx + 1.0))
    zeta = jnp.exp(log_gamma[:, None] * (RET_BLOCK - 1.0 - idx))
    cdec = jnp.exp(log_gamma * RET_BLOCK)
    def per_token_rows(tab):
        rows = jnp.broadcast_to(tab.T[:, :, None], (RET_BLOCK, HEADS, RET_DK))
        return jnp.tile(rows.reshape(RET_BLOCK, RET_QK), (TOKEN_TILE // RET_BLOCK, 1))

    return (dmask, per_token_rows(xi), per_token_rows(zeta),
            jnp.broadcast_to(cdec[:, None, None], (HEADS, 1, RET_DV)))


W_PREP_ROWS = 128


def _w_in_prep_kernel(w_ref, wret_ref, wmla_ref, wg_ref):
    x = w_ref[0]
    c0 = 2 * RET_QK + 2 * RET_V
    c1 = c0 + Q_LORA + KV_LORA
    wret_ref[...] = x[:, :c0].astype(BF16)
    wmla_ref[:, 0:Q_LORA + KV_LORA] = x[:, c0:c1].astype(BF16)
    lane = lax.broadcasted_iota(jnp.int32, (1, LANES), 1)
    kk = x[:, c1:c1 + LANES]
    kpe2 = jnp.where(lane < MLA_ROPE, kk, pltpu.roll(kk, MLA_ROPE, 1))
    sign = jnp.where(lane % MLA_ROPE < MLA_ROPE // 2, -1.0, 1.0)
    wmla_ref[:, Q_LORA + KV_LORA:Q_LORA + KV_LORA + LANES] = kpe2.astype(BF16)
    wmla_ref[:, Q_LORA + KV_LORA + LANES:MLA_LAT] = (
        pltpu.roll(kpe2, MLA_ROPE // 2, 1) * sign).astype(BF16)
    wg_ref[...] = x[:, c1 + MLA_ROPE:].astype(BF16)


def _w_in_prep(w, layer):
    _, d, cols = w.shape
    c0 = 2 * RET_QK + 2 * RET_V
    assert cols == c0 + Q_LORA + KV_LORA + MLA_ROPE + 2 * D_MODEL
    return pl.pallas_call(
        _w_in_prep_kernel,
        name="w_in_prep",
        grid=(d // W_PREP_ROWS,),
        in_specs=[pl.BlockSpec((1, W_PREP_ROWS, cols), lambda i: (layer, i, 0))],
        out_specs=[_row_spec(W_PREP_ROWS, c0), _row_spec(W_PREP_ROWS, MLA_LAT),
                   _row_spec(W_PREP_ROWS, 2 * D_MODEL)],
        out_shape=[jax.ShapeDtypeStruct((d, c0), BF16),
                   jax.ShapeDtypeStruct((d, MLA_LAT), BF16),
                   jax.ShapeDtypeStruct((d, 2 * D_MODEL), BF16)],
        compiler_params=_params("parallel"),
    )(w)


def _rope_consts(half):
    invf = ROPE_BASE ** (-jnp.arange(half, dtype=F32) / half)
    invf = jnp.tile(invf, LANES // half).reshape(1, LANES)
    delta = jnp.arange(TOKEN_TILE, dtype=F32)[:, None] * invf
    return invf, jnp.cos(delta), jnp.sin(delta)


def kernel(x, p, positions, ln_g, ln_b, ffn1_w_in, ffn1_w_out, w_in, ret_gn_g, w_ret_o,
           q_norm_g, kv_norm_g, w_uq, w_ukv, w_mla_o, w_out, ffn2_w_in, ffn2_w_out,
           ple_w_gate, ple_w_proj):
    b, s, d = x.shape
    t = b * s
    h = x.reshape(t, d)
    pos_b = jnp.broadcast_to(positions.astype(F32).reshape(t, 1), (t, LANES))
    row = lambda v: v.reshape(1, -1)

    for i in range(ln_g.shape[0]):
        w_ret, w_mla, w_gates = _w_in_prep(w_in, i)

        q_scale = (MLA_NOPE + MLA_ROPE) ** -0.5 * math.log2(math.e)
        uq = w_uq[i].reshape(Q_LORA, HEADS, MLA_NOPE + MLA_ROPE) * q_scale
        uq_nope = uq[:, :, :MLA_NOPE].reshape(Q_LORA, HEADS * MLA_NOPE)
        uq_rope = uq[:, :, MLA_NOPE:]
        uq_rot = jnp.concatenate(
            [-uq_rope[:, :, MLA_ROPE // 2:], uq_rope[:, :, :MLA_ROPE // 2]], axis=2)
        wuq = jnp.concatenate(
            [uq_nope, uq_rope.reshape(Q_LORA, -1), uq_rot.reshape(Q_LORA, -1)], axis=1).astype(BF16)
        ukv = w_ukv[i].reshape(KV_LORA, HEADS, MLA_NOPE + MLA_DV)
        wuk = ukv[:, :, :MLA_NOPE].reshape(KV_LORA, -1).astype(BF16)
        wuvt = ukv[:, :, MLA_NOPE:].reshape(KV_LORA, -1).T.astype(BF16)

        rope_ret = _rope_consts(RET_DK // 2)
        rope_mla = _rope_consts(MLA_ROPE // 2)

        h = _ffn_ln(h, ffn1_w_in[i].astype(BF16), ffn1_w_out[i].astype(BF16),
                    row(ln_g[i, 0]), row(ln_b[i, 0]))

        dmask, xi, zeta, cdec = _retention_tables()
        y_ret = _retention(h, pos_b, rope_ret, xi, zeta, dmask, cdec, w_ret, s)
        qcat, kcat, vt = _mla_proj(
            h, pos_b, rope_mla, w_mla, row(q_norm_g[i]), row(kv_norm_g[i]), wuq, wuk, wuvt)
        o_mla = _mla_attn(qcat.reshape(b, s, -1), kcat.reshape(b, s, -1),
                          vt.reshape(b, s // ATT_K, HEADS * MLA_DV, ATT_K))
        w_ro = (ret_gn_g[i][:, None] * w_ret_o[i]).astype(BF16)
        h = _merge_ln(y_ret, o_mla.reshape(t, -1), h, w_gates,
                      w_ro, w_mla_o[i].astype(BF16), w_out[i].astype(BF16),
                      row(ln_g[i, 1]), row(ln_b[i, 1]))

        h = _ffn_ple_ln(h, p[i].reshape(t, -1), ffn2_w_in[i].astype(BF16),
                        ffn2_w_out[i].astype(BF16), ple_w_gate[i].astype(BF16),
                        ple_w_proj[i].astype(BF16), row(ln_g[i, 2]), row(ln_b[i, 2]),
                        row(ln_g[i, 3]), row(ln_b[i, 3]))
    return h.reshape(b, s, d)
```

```python
import functools
import math

import jax
import jax.numpy as jnp
from jax import lax
from jax.experimental import pallas as pl
from jax.experimental.pallas import tpu as pltpu

F32 = jnp.float32
BF16 = jnp.bfloat16

D_MODEL = 1024
CHUNK = 64
D_PLE = 256
D_FF = 2816
HEADS = 8
RET_DK = 128
RET_DV = 256
RET_QK = HEADS * RET_DK
RET_V = HEADS * RET_DV
MLA_NOPE = 128
MLA_ROPE = 64
MLA_DV = 128
Q_LORA = 256
KV_LORA = 256
ROPE_BASE = 10000.0
EPS = 1e-5
ALPHA = 2.0 ** 0.25

LANES = 128
FF_CHUNK = 256
TOKEN_TILE = 512
RET_BLOCK = 256
ATT_Q = 512
ATT_K = 512
ATT_HEADS = 2
MLA_QK = 256
MLA_DVA = MLA_DV + 16
VMEM_LIMIT = 52 * 1024 * 1024
NEG_BIG = -1e30


def _const_spec(shape):
    nd = len(shape)
    return pl.BlockSpec(shape, lambda *_: (0,) * nd, pipeline_mode=pl.Buffered(1))


def _row_spec(tile, width):
    return pl.BlockSpec((tile, width), lambda i: (i, 0))


def _params(*sem):
    return pltpu.CompilerParams(dimension_semantics=sem, vmem_limit_bytes=VMEM_LIMIT)


def _dot(a, b):
    return jnp.dot(a, b, preferred_element_type=F32)


def _dot_nt(a, bt):
    return lax.dot_general(a, bt, (((1,), (1,)), ((), ())), preferred_element_type=F32)


def _layer_norm(y, g, b):
    mu = jnp.mean(y, axis=-1, keepdims=True)
    d = y - mu
    var = jnp.mean(d * d, axis=-1, keepdims=True)
    return d * lax.rsqrt(var + EPS) * g + b


def _rms_norm(x, g):
    return x * lax.rsqrt(jnp.mean(x * x, axis=-1, keepdims=True) + EPS) * g


def _swiglu(xb, w_in_ref, w_out_ref, act_ref):
    for c in range(D_FF // FF_CHUNK):
        lo = c * FF_CHUNK
        g = _dot(xb, w_in_ref[:, lo:lo + FF_CHUNK])
        u = _dot(xb, w_in_ref[:, D_FF + lo:D_FF + lo + FF_CHUNK])
        act_ref[:, lo:lo + FF_CHUNK] = (g * jax.nn.sigmoid(g) * u).astype(BF16)
    return _dot(act_ref[...], w_out_ref[...])


def _ffn_ln_kernel(x_ref, w_in_ref, w_out_ref, g_ref, b_ref, o_ref, act_ref):
    x = x_ref[...]
    f = _swiglu(x.astype(BF16), w_in_ref, w_out_ref, act_ref)
    o_ref[...] = _layer_norm(ALPHA * x + 0.5 * f, g_ref[...], b_ref[...])


def _ffn_ln(x, w_in, w_out, g, b):
    t = x.shape[0]
    return pl.pallas_call(
        _ffn_ln_kernel,
        name="ffn_ln",
        grid=(t // TOKEN_TILE,),
        in_specs=[
            _row_spec(TOKEN_TILE, D_MODEL),
            _const_spec((D_MODEL, 2 * D_FF)),
            _const_spec((D_FF, D_MODEL)),
            _const_spec((1, D_MODEL)),
            _const_spec((1, D_MODEL)),
        ],
        out_specs=_row_spec(TOKEN_TILE, D_MODEL),
        out_shape=jax.ShapeDtypeStruct((t, D_MODEL), F32),
        scratch_shapes=[pltpu.VMEM((TOKEN_TILE, D_FF), BF16)],
        compiler_params=_params("parallel"),
    )(x, w_in, w_out, g, b)


def _ffn_ple_ln_kernel(h_ref, p_ref, w_in_ref, w_out_ref, wg_ref, wp_ref,
                       g2_ref, b2_ref, g3_ref, b3_ref, o_ref, act_ref):
    h = h_ref[...]
    f = _swiglu(h.astype(BF16), w_in_ref, w_out_ref, act_ref)
    h2 = _layer_norm(ALPHA * h + 0.5 * f, g2_ref[...], b2_ref[...])
    gate = jax.nn.sigmoid(_dot(h2.astype(BF16), wg_ref[...]))
    proj = _dot(p_ref[...].astype(BF16), wp_ref[...])
    o_ref[...] = _layer_norm(ALPHA * h2 + gate * proj, g3_ref[...], b3_ref[...])


def _ffn_ple_ln(h, p, w_in, w_out, wg, wp, g2, b2, g3, b3):
    t = h.shape[0]
    return pl.pallas_call(
        _ffn_ple_ln_kernel,
        name="ffn_ple_ln",
        grid=(t // TOKEN_TILE,),
        in_specs=[
            _row_spec(TOKEN_TILE, D_MODEL),
            _row_spec(TOKEN_TILE, D_PLE),
            _const_spec((D_MODEL, 2 * D_FF)),
            _const_spec((D_FF, D_MODEL)),
            _const_spec((D_MODEL, D_MODEL)),
            _const_spec((D_PLE, D_MODEL)),
            _const_spec((1, D_MODEL)),
            _const_spec((1, D_MODEL)),
            _const_spec((1, D_MODEL)),
            _const_spec((1, D_MODEL)),
        ],
        out_specs=_row_spec(TOKEN_TILE, D_MODEL),
        out_shape=jax.ShapeDtypeStruct((t, D_MODEL), F32),
        scratch_shapes=[pltpu.VMEM((TOKEN_TILE, D_FF), BF16)],
        compiler_params=_params("parallel"),
    )(h, p, w_in, w_out, wg, wp, g2, b2, g3, b3)


def _rope_tables(pos_ref, invf_ref, dcos_ref, dsin_ref, cos_ref, sin_ref):
    pos = pos_ref[...]
    invf = invf_ref[...]
    base = pos[0:1, :]
    row = lax.broadcasted_iota(jnp.int32, pos.shape, 0).astype(F32)
    exact_f32_int = 2.0 ** 24 - pos.shape[0]
    consecutive = jnp.logical_and(jnp.all(pos == base + row),
                                  jnp.all(jnp.abs(base) < exact_f32_int))

    @pl.when(consecutive)
    def _():
        a = base * invf
        ca = jnp.cos(a)
        sa = jnp.sin(a)
        cos_ref[...] = ca * dcos_ref[...] - sa * dsin_ref[...]
        sin_ref[...] = sa * dcos_ref[...] + ca * dsin_ref[...]

    @pl.when(jnp.logical_not(consecutive))
    def _():
        ang = pos * invf
        cos_ref[...] = jnp.cos(ang)
        sin_ref[...] = jnp.sin(ang)


def _retention_kernel(h_ref, pos_ref, invf_ref, dcos_ref, dsin_ref, xi_ref, zeta_ref, dmask_ref,
                      cdec_ref, w_ref, o_ref, rq_ref, rqx_ref, rk_ref, rkz_ref, rv_ref, sg_ref,
                      state_ref, cos_ref, sin_ref, *, tiles_per_seq):
    @pl.when(pl.program_id(0) % tiles_per_seq == 0)
    def _():
        state_ref[...] = jnp.zeros_like(state_ref)

    _rope_tables(pos_ref, invf_ref, dcos_ref, dsin_ref, cos_ref, sin_ref)
    hb = h_ref[...].astype(BF16)
    cos = cos_ref[...]
    lane = lax.broadcasted_iota(jnp.int32, (1, LANES), 1)
    sin = sin_ref[...] * jnp.where(lane < RET_DK // 2, -1.0, 1.0)
    k_scale = RET_DK ** -0.5
    g = _dot_nt(hb, w_ref[2 * RET_QK + RET_V:2 * RET_QK + 2 * RET_V, :])
    sg_ref[...] = (g * jax.nn.sigmoid(g)).astype(BF16)
    q = _dot_nt(hb, w_ref[0:RET_QK, :])
    k = _dot_nt(hb, w_ref[RET_QK:2 * RET_QK, :])
    for h in range(HEADS):
        sl = slice(h * RET_DK, (h + 1) * RET_DK)
        tq = q[:, sl]
        tq = tq * cos + pltpu.roll(tq, RET_DK // 2, 1) * sin
        rq_ref[:, sl] = tq.astype(BF16)
        rqx_ref[:, sl] = (tq * xi_ref[:, sl]).astype(BF16)
        tk = k[:, sl]
        tk = (tk * cos + pltpu.roll(tk, RET_DK // 2, 1) * sin) * k_scale
        rk_ref[:, sl] = tk.astype(BF16)
        rkz_ref[:, sl] = (tk * zeta_ref[:, sl]).astype(BF16)
    rv_ref[...] = _dot_nt(hb, w_ref[2 * RET_QK:2 * RET_QK + RET_V, :]).astype(BF16)

    def qk(rows, h):
        qs = slice(h * RET_DK, (h + 1) * RET_DK)
        return lax.dot_general(rq_ref[rows, qs], rk_ref[rows, qs], (((1,), (1,)), ((), ())),
                               preferred_element_type=F32)

    for blk in range(TOKEN_TILE // RET_BLOCK):
        rows = slice(blk * RET_BLOCK, (blk + 1) * RET_BLOCK)
        s_next = qk(rows, 0)
        for h in range(HEADS):
            qs = slice(h * RET_DK, (h + 1) * RET_DK)
            vs = slice(h * RET_DV, (h + 1) * RET_DV)
            v = rv_ref[rows, vs]
            state = state_ref[h]
            s = s_next
            if h + 1 < HEADS:
                s_next = qk(rows, h + 1)
            cross = _dot(rqx_ref[rows, qs], state.astype(BF16))
            upd = lax.dot_general(rkz_ref[rows, qs], v, (((0,), (0,)), ((), ())),
                                  preferred_element_type=F32)
            state_ref[h] = state * cdec_ref[h] + upd
            y = _dot((s * dmask_ref[h]).astype(BF16), v) + cross
            mu = jnp.mean(y, axis=-1, keepdims=True)
            d = y - mu
            var = jnp.mean(d * d, axis=-1, keepdims=True)
            o_ref[rows, vs] = (d * lax.rsqrt(var + EPS) * sg_ref[rows, vs].astype(F32)).astype(BF16)


def _retention(h, pos_b, rope, xi, zeta, dmask, cdec, w, seq_len):
    t = h.shape[0]
    return pl.pallas_call(
        functools.partial(_retention_kernel, tiles_per_seq=seq_len // TOKEN_TILE),
        name="retention",
        grid=(t // TOKEN_TILE,),
        in_specs=[
            _row_spec(TOKEN_TILE, D_MODEL),
            _row_spec(TOKEN_TILE, LANES),
            _const_spec((1, LANES)),
            _const_spec((TOKEN_TILE, LANES)),
            _const_spec((TOKEN_TILE, LANES)),
            _const_spec((TOKEN_TILE, RET_QK)),
            _const_spec((TOKEN_TILE, RET_QK)),
            _const_spec((HEADS, RET_BLOCK, RET_BLOCK)),
            _const_spec((HEADS, 1, RET_DV)),
            _const_spec((2 * RET_QK + 2 * RET_V, D_MODEL)),
        ],
        out_specs=_row_spec(TOKEN_TILE, RET_V),
        out_shape=jax.ShapeDtypeStruct((t, RET_V), BF16),
        scratch_shapes=[pltpu.VMEM((TOKEN_TILE, RET_QK), BF16)] * 4
        + [pltpu.VMEM((TOKEN_TILE, RET_V), BF16)] * 2
        + [pltpu.VMEM((HEADS, RET_DK, RET_DV), F32)]
        + [pltpu.VMEM((TOKEN_TILE, LANES), F32)] * 2,
        compiler_params=_params("arbitrary"),
    )(h, pos_b, *rope, xi, zeta, dmask, cdec, w)


MLA_LAT = Q_LORA + KV_LORA + LANES


def _mla_proj_kernel(h_ref, pos_ref, invf_ref, dcos_ref, dsin_ref, w_ref, qg_ref, kvg_ref,
                     wuq_ref, wuk_ref, wuvt_ref, qcat_ref, kcat_ref, vt_ref, cos_ref, sin_ref):
    _rope_tables(pos_ref, invf_ref, dcos_ref, dsin_ref, cos_ref, sin_ref)
    hb = h_ref[...].astype(BF16)
    cos = cos_ref[...]
    sin = sin_ref[...]
    lane = lax.broadcasted_iota(jnp.int32, (1, LANES), 1)
    first_half = lane < MLA_ROPE

    lat = _dot_nt(hb, w_ref[...])
    cq = lat[:, 0:Q_LORA]
    ckv = lat[:, Q_LORA:Q_LORA + KV_LORA]
    kpe = lat[:, Q_LORA + KV_LORA:MLA_LAT]
    kpe = kpe + pltpu.roll(kpe, MLA_ROPE, 1)
    kpe_rot = pltpu.roll(kpe, MLA_ROPE // 2, 1) * jnp.where(
        lane % MLA_ROPE < MLA_ROPE // 2, -1.0, 1.0)

    qf = _dot(_rms_norm(cq, qg_ref[...]).astype(BF16), wuq_ref[...])
    nope_w = HEADS * MLA_NOPE
    rope_w = HEADS * MLA_ROPE
    for j in range(HEADS // 2):
        r = (qf[:, nope_w + j * LANES:nope_w + (j + 1) * LANES] * cos
             + qf[:, nope_w + rope_w + j * LANES:nope_w + rope_w + (j + 1) * LANES] * sin)
        base = 2 * j * MLA_QK
        qcat_ref[:, base + MLA_NOPE:base + MLA_QK] = jnp.where(first_half, r, 0.0).astype(BF16)
        base += MLA_QK
        qcat_ref[:, base + MLA_NOPE:base + MLA_QK] = jnp.where(first_half, 0.0, r).astype(BF16)
    for h in range(HEADS):
        qcat_ref[:, h * MLA_QK:h * MLA_QK + MLA_NOPE] = (
            qf[:, h * MLA_NOPE:(h + 1) * MLA_NOPE].astype(BF16))

    ckvn = _rms_norm(ckv, kvg_ref[...])
    kv = _dot(ckvn.astype(BF16), wuk_ref[...])
    vt = _dot(wuvt_ref[...], ckvn.T.astype(BF16)).astype(BF16)
    ones = jnp.ones((MLA_DVA - MLA_DV, vt.shape[1]), BF16)
    for h in range(HEADS):
        vt_ref[0, h * MLA_DVA:h * MLA_DVA + MLA_DV, :] = vt[h * MLA_DV:(h + 1) * MLA_DV, :]
        vt_ref[0, h * MLA_DVA + MLA_DV:(h + 1) * MLA_DVA, :] = ones
    kr = kpe * cos + kpe_rot * sin
    kr_even = jnp.where(first_half, kr, 0.0).astype(BF16)
    kr_odd = jnp.where(first_half, 0.0, kr).astype(BF16)
    for h in range(HEADS):
        kcat_ref[:, h * MLA_QK:h * MLA_QK + MLA_NOPE] = (
            kv[:, h * MLA_NOPE:(h + 1) * MLA_NOPE].astype(BF16))
        kcat_ref[:, h * MLA_QK + MLA_NOPE:(h + 1) * MLA_QK] = kr_even if h % 2 == 0 else kr_odd


def _mla_proj(h, pos_b, rope, w, qg, kvg, wuq, wuk, wuvt):
    t = h.shape[0]
    uq_cols = HEADS * (MLA_NOPE + 2 * MLA_ROPE)
    assert TOKEN_TILE == ATT_K
    return pl.pallas_call(
        _mla_proj_kernel,
        name="mla_proj",
        grid=(t // TOKEN_TILE,),
        in_specs=[
            _row_spec(TOKEN_TILE, D_MODEL),
            _row_spec(TOKEN_TILE, LANES),
            _const_spec((1, LANES)),
            _const_spec((TOKEN_TILE, LANES)),
            _const_spec((TOKEN_TILE, LANES)),
            _const_spec((MLA_LAT, D_MODEL)),
            _const_spec((1, Q_LORA)),
            _const_spec((1, KV_LORA)),
            _const_spec((Q_LORA, uq_cols)),
            _const_spec((KV_LORA, HEADS * MLA_NOPE)),
            _const_spec((HEADS * MLA_DV, KV_LORA)),
        ],
        out_specs=[
            _row_spec(TOKEN_TILE, HEADS * MLA_QK),
            _row_spec(TOKEN_TILE, HEADS * MLA_QK),
            pl.BlockSpec((1, HEADS * MLA_DVA, TOKEN_TILE), lambda i: (i, 0, 0)),
        ],
        out_shape=[
            jax.ShapeDtypeStruct((t, HEADS * MLA_QK), BF16),
            jax.ShapeDtypeStruct((t, HEADS * MLA_QK), BF16),
            jax.ShapeDtypeStruct((t // TOKEN_TILE, HEADS * MLA_DVA, TOKEN_TILE), BF16),
        ],
        scratch_shapes=[pltpu.VMEM((TOKEN_TILE, LANES), F32)] * 2,
        compiler_params=_params("parallel"),
    )(h, pos_b, *rope, w, qg, kvg, wuq, wuk, wuvt)


def _softmax_step(st, mx, vt, stats):
    m, acc = stats
    m_new = jnp.maximum(m, mx)
    a = jnp.exp2(m - m_new)
    p = jnp.exp2(st - m_new)
    acc = a * acc + _dot(vt, p.astype(BF16))
    return m_new, acc


def _mla_attn_kernel(q_ref, k_ref, vt_ref, o_ref, s0_ref, s1_ref, mx0_ref, mx1_ref,
                     m_ref, acc_ref):
    qi = pl.program_id(2)
    heads = range(ATT_HEADS)

    def scores(j, s_ref, mx_ref):
        off = pl.multiple_of(j * ATT_K, ATT_K)
        for hh in heads:
            st = lax.dot_general(
                k_ref[0, pl.ds(off, ATT_K), hh * MLA_QK:(hh + 1) * MLA_QK],
                q_ref[0, :, hh * MLA_QK:(hh + 1) * MLA_QK],
                (((1,), (1,)), ((), ())), preferred_element_type=F32)
            s_ref[hh] = st
            mx_ref[hh] = jnp.max(st, axis=0, keepdims=True)

    def consume(j, s_ref, mx_ref, mask=None):
        for hh in heads:
            st = s_ref[hh]
            if mask is None:
                mx = mx_ref[hh]
            else:
                st = jnp.where(mask, st, -jnp.inf)
                mx = jnp.max(st, axis=0, keepdims=True)
            m, acc = _softmax_step(st, mx, vt_ref[0, j, hh * MLA_DVA:(hh + 1) * MLA_DVA, :],
                                   (m_ref[hh], acc_ref[hh]))
            m_ref[hh] = m
            acc_ref[hh] = acc

    m_ref[...] = jnp.full_like(m_ref, NEG_BIG)
    acc_ref[...] = jnp.zeros_like(acc_ref)

    scores(0, s0_ref, mx0_ref)

    def pair(jj, c):
        j = 2 * jj
        scores(j + 1, s1_ref, mx1_ref)
        consume(j, s0_ref, mx0_ref)
        scores(j + 2, s0_ref, mx0_ref)
        consume(j + 1, s1_ref, mx1_ref)
        return c

    lax.fori_loop(0, qi // 2, pair, 0)

    key_chunk = lax.broadcasted_iota(jnp.int32, (ATT_K, ATT_Q), 0) // CHUNK
    qry_chunk = lax.broadcasted_iota(jnp.int32, (ATT_K, ATT_Q), 1) // CHUNK
    mask = key_chunk <= qry_chunk

    @pl.when(qi % 2 == 0)
    def _():
        consume(qi, s0_ref, mx0_ref, mask)

    @pl.when(qi % 2 == 1)
    def _():
        scores(qi, s1_ref, mx1_ref)
        consume(qi - 1, s0_ref, mx0_ref)
        consume(qi, s1_ref, mx1_ref, mask)

    for hh in heads:
        l = acc_ref[hh, MLA_DV:MLA_DV + 1, :]
        o_ref[0, :, hh * MLA_DV:(hh + 1) * MLA_DV] = (
            acc_ref[hh, 0:MLA_DV, :] / l).T.astype(BF16)


def _mla_attn(qcat, kcat, vt):
    b, s, _ = qcat.shape
    assert ATT_Q == ATT_K
    n_kb = s // ATT_K
    return pl.pallas_call(
        _mla_attn_kernel,
        name="mla_attn",
        grid=(b, HEADS // ATT_HEADS, s // ATT_Q),
        in_specs=[
            pl.BlockSpec((1, ATT_Q, ATT_HEADS * MLA_QK), lambda i, h, j: (i, j, h)),
            pl.BlockSpec((1, s, ATT_HEADS * MLA_QK), lambda i, h, j: (i, 0, h)),
            pl.BlockSpec((1, n_kb, ATT_HEADS * MLA_DVA, ATT_K), lambda i, h, j: (i, 0, h, 0)),
        ],
        out_specs=pl.BlockSpec((1, ATT_Q, ATT_HEADS * MLA_DV), lambda i, h, j: (i, j, h)),
        out_shape=jax.ShapeDtypeStruct((b, s, HEADS * MLA_DV), BF16),
        scratch_shapes=[
            pltpu.VMEM((ATT_HEADS, ATT_K, ATT_Q), F32),
            pltpu.VMEM((ATT_HEADS, ATT_K, ATT_Q), F32),
            pltpu.VMEM((ATT_HEADS, 1, ATT_Q), F32),
            pltpu.VMEM((ATT_HEADS, 1, ATT_Q), F32),
            pltpu.VMEM((ATT_HEADS, 1, ATT_Q), F32),
            pltpu.VMEM((ATT_HEADS, MLA_DVA, ATT_Q), F32),
        ],
        compiler_params=_params("parallel", "parallel", "arbitrary"),
    )(qcat, kcat, vt)


def _merge_ln_kernel(yr_ref, om_ref, h_ref, wg_ref, wro_ref, wmo_ref, wout_ref,
                     g_ref, b_ref, o_ref):
    h = h_ref[...]
    hb = h.astype(BF16)
    mix = (jax.nn.sigmoid(_dot_nt(hb, wg_ref[0:D_MODEL, :])) * _dot(yr_ref[...], wro_ref[...])
           + jax.nn.sigmoid(_dot_nt(hb, wg_ref[D_MODEL:2 * D_MODEL, :]))
           * _dot(om_ref[...], wmo_ref[...]))
    mixed = _dot(mix.astype(BF16), wout_ref[...])
    o_ref[...] = _layer_norm(ALPHA * h + mixed, g_ref[...], b_ref[...])


def _merge_ln(yr, om, h, wg, wro, wmo, wout, g, b):
    t = h.shape[0]
    return pl.pallas_call(
        _merge_ln_kernel,
        name="merge_ln",
        grid=(t // TOKEN_TILE,),
        in_specs=[
            _row_spec(TOKEN_TILE, RET_V),
            _row_spec(TOKEN_TILE, HEADS * MLA_DV),
            _row_spec(TOKEN_TILE, D_MODEL),
            _const_spec((2 * D_MODEL, D_MODEL)),
            _const_spec((RET_V, D_MODEL)),
            _const_spec((HEADS * MLA_DV, D_MODEL)),
            _const_spec((D_MODEL, D_MODEL)),
            _const_spec((1, D_MODEL)),
            _const_spec((1, D_MODEL)),
        ],
        out_specs=_row_spec(TOKEN_TILE, D_MODEL),
        out_shape=jax.ShapeDtypeStruct((t, D_MODEL), F32),
        compiler_params=_params("parallel"),
    )(yr, om, h, wg, wro, wmo, wout, g, b)


def _retention_tables():
    log_gamma = jnp.log(1.0 - 2.0 ** (-5.0 - jnp.arange(HEADS, dtype=F32)))
    idx = jnp.arange(RET_BLOCK, dtype=F32)
    chunk = jnp.arange(RET_BLOCK) // CHUNK
    visible = chunk[None, :] <= chunk[:, None]
    dist = jnp.abs(idx[:, None] - idx[None, :])
    dmask = jnp.where(visible[None], jnp.exp(log_gamma[:, None, None] * dist[None]), 0.0)
    xi = jnp.exp(log_gamma[:, None] * (idx + 1.0))
    zeta = jnp.exp(log_gamma[:, None] * (RET_BLOCK - 1.0 - idx))
    cdec = jnp.exp(log_gamma * RET_BLOCK)
    def per_token_rows(tab):
        rows = jnp.broadcast_to(tab.T[:, :, None], (RET_BLOCK, HEADS, RET_DK))
        return jnp.tile(rows.reshape(RET_BLOCK, RET_QK), (TOKEN_TILE // RET_BLOCK, 1))

    return (dmask, per_token_rows(xi), per_token_rows(zeta),
            jnp.broadcast_to(cdec[:, None, None], (HEADS, 1, RET_DV)))


def _rope_consts(half):
    invf = ROPE_BASE ** (-jnp.arange(half, dtype=F32) / half)
    invf = jnp.tile(invf, LANES // half).reshape(1, LANES)
    delta = jnp.arange(TOKEN_TILE, dtype=F32)[:, None] * invf
    return invf, jnp.cos(delta), jnp.sin(delta)


def kernel(x, p, positions, ln_g, ln_b, ffn1_w_in, ffn1_w_out, w_in, ret_gn_g, w_ret_o,
           q_norm_g, kv_norm_g, w_uq, w_ukv, w_mla_o, w_out, ffn2_w_in, ffn2_w_out,
           ple_w_gate, ple_w_proj):
    b, s, d = x.shape
    t = b * s
    h = x.reshape(t, d)
    pos_b = jnp.broadcast_to(positions.astype(F32).reshape(t, 1), (t, LANES))
    row = lambda v: v.reshape(1, -1)

    for i in range(ln_g.shape[0]):
        w_in_t = jnp.swapaxes(w_in[i], 0, 1)
        c0 = 2 * RET_QK + 2 * RET_V
        c1 = c0 + Q_LORA + KV_LORA + MLA_ROPE
        w_ret = w_in_t[:c0].astype(BF16)
        w_mla = jnp.pad(w_in_t[c0:c1], ((0, LANES - MLA_ROPE), (0, 0))).astype(BF16)
        w_gates = w_in_t[c1:].astype(BF16)

        q_scale = (MLA_NOPE + MLA_ROPE) ** -0.5 * math.log2(math.e)
        uq = w_uq[i].reshape(Q_LORA, HEADS, MLA_NOPE + MLA_ROPE) * q_scale
        uq_nope = uq[:, :, :MLA_NOPE].reshape(Q_LORA, HEADS * MLA_NOPE)
        uq_rope = uq[:, :, MLA_NOPE:]
        uq_rot = jnp.concatenate(
            [-uq_rope[:, :, MLA_ROPE // 2:], uq_rope[:, :, :MLA_ROPE // 2]], axis=2)
        wuq = jnp.concatenate(
            [uq_nope, uq_rope.reshape(Q_LORA, -1), uq_rot.reshape(Q_LORA, -1)], axis=1).astype(BF16)
        ukv = w_ukv[i].reshape(KV_LORA, HEADS, MLA_NOPE + MLA_DV)
        wuk = ukv[:, :, :MLA_NOPE].reshape(KV_LORA, -1).astype(BF16)
        wuvt = ukv[:, :, MLA_NOPE:].reshape(KV_LORA, -1).T.astype(BF16)

        rope_ret = _rope_consts(RET_DK // 2)
        rope_mla = _rope_consts(MLA_ROPE // 2)

        h = _ffn_ln(h, ffn1_w_in[i].astype(BF16), ffn1_w_out[i].astype(BF16),
                    row(ln_g[i, 0]), row(ln_b[i, 0]))

        dmask, xi, zeta, cdec = _retention_tables()
        y_ret = _retention(h, pos_b, rope_ret, xi, zeta, dmask, cdec, w_ret, s)
        qcat, kcat, vt = _mla_proj(
            h, pos_b, rope_mla, w_mla, row(q_norm_g[i]), row(kv_norm_g[i]), wuq, wuk, wuvt)
        o_mla = _mla_attn(qcat.reshape(b, s, -1), kcat.reshape(b, s, -1),
                          vt.reshape(b, s // ATT_K, HEADS * MLA_DVA, ATT_K))
        w_ro = (ret_gn_g[i][:, None] * w_ret_o[i]).astype(BF16)
        h = _merge_ln(y_ret, o_mla.reshape(t, -1), h, w_gates,
                      w_ro, w_mla_o[i].astype(BF16), w_out[i].astype(BF16),
                      row(ln_g[i, 1]), row(ln_b[i, 1]))

        h = _ffn_ple_ln(h, p[i].reshape(t, -1), ffn2_w_in[i].astype(BF16),
                        ffn2_w_out[i].astype(BF16), ple_w_gate[i].astype(BF16),
                        ple_w_proj[i].astype(BF16), row(ln_g[i, 2]), row(ln_b[i, 2]),
                        row(ln_g[i, 3]), row(ln_b[i, 3]))
    return h.reshape(b, s, d)
```

```python
import functools
import math

import jax
import jax.numpy as jnp
from jax import lax
from jax.experimental import pallas as pl
from jax.experimental.pallas import tpu as pltpu

F32 = jnp.float32
BF16 = jnp.bfloat16

D_MODEL = 1024
CHUNK = 64
D_PLE = 256
D_FF = 2816
HEADS = 8
RET_DK = 128
RET_DV = 256
RET_QK = HEADS * RET_DK
RET_V = HEADS * RET_DV
MLA_NOPE = 128
MLA_ROPE = 64
MLA_DV = 128
Q_LORA = 256
KV_LORA = 256
ROPE_BASE = 10000.0
EPS = 1e-5
ALPHA = 2.0 ** 0.25

LANES = 128
FF_CHUNK = 256
TOKEN_TILE = 512
FFN_TILE = 1024
FFN_CHAIN = 512
RET_BLOCK = 256
ATT_Q = 512
ATT_K = 512
ATT_HEADS = 2
MLA_QK = 256
MLA_DVA = MLA_DV + 16
VMEM_LIMIT = 52 * 1024 * 1024
NEG_BIG = -1e30


def _const_spec(shape):
    nd = len(shape)
    return pl.BlockSpec(shape, lambda *_: (0,) * nd, pipeline_mode=pl.Buffered(1))


def _row_spec(tile, width):
    return pl.BlockSpec((tile, width), lambda i: (i, 0))


def _params(*sem):
    return pltpu.CompilerParams(dimension_semantics=sem, vmem_limit_bytes=VMEM_LIMIT)


def _dot(a, b):
    return jnp.dot(a, b, preferred_element_type=F32)


def _dot_nt(a, bt):
    return lax.dot_general(a, bt, (((1,), (1,)), ((), ())), preferred_element_type=F32)


def _layer_norm(y, g, b):
    mu = jnp.mean(y, axis=-1, keepdims=True)
    d = y - mu
    var = jnp.mean(d * d, axis=-1, keepdims=True)
    return d * lax.rsqrt(var + EPS) * g + b


def _rms_norm(x, g):
    return x * lax.rsqrt(jnp.mean(x * x, axis=-1, keepdims=True) + EPS) * g


def _swiglu(xb, w_in_ref, w_out_ref, act_ref):
    for c in range(D_FF // FF_CHUNK):
        lo = c * FF_CHUNK
        g = _dot(xb, w_in_ref[:, lo:lo + FF_CHUNK])
        u = _dot(xb, w_in_ref[:, D_FF + lo:D_FF + lo + FF_CHUNK])
        act_ref[:, lo:lo + FF_CHUNK] = (g * jax.nn.sigmoid(g) * u).astype(BF16)
    return _dot(act_ref[...], w_out_ref[...])


def _row_chains(tile):
    return [slice(r, r + FFN_CHAIN) for r in range(0, tile, FFN_CHAIN)]


def _ffn_ln_kernel(x_ref, w_in_ref, w_out_ref, g_ref, b_ref, o_ref, act_ref):
    for c, rows in enumerate(_row_chains(x_ref.shape[0])):
        x = x_ref[rows, :]
        f = _swiglu(x.astype(BF16), w_in_ref, w_out_ref, act_ref.at[c])
        o_ref[rows, :] = _layer_norm(ALPHA * x + 0.5 * f, g_ref[...], b_ref[...])


def _ffn_ln(x, w_in, w_out, g, b):
    t = x.shape[0]
    return pl.pallas_call(
        _ffn_ln_kernel,
        name="ffn_ln",
        grid=(t // FFN_TILE,),
        in_specs=[
            _row_spec(FFN_TILE, D_MODEL),
            _const_spec((D_MODEL, 2 * D_FF)),
            _const_spec((D_FF, D_MODEL)),
            _const_spec((1, D_MODEL)),
            _const_spec((1, D_MODEL)),
        ],
        out_specs=_row_spec(FFN_TILE, D_MODEL),
        out_shape=jax.ShapeDtypeStruct((t, D_MODEL), F32),
        scratch_shapes=[pltpu.VMEM((FFN_TILE // FFN_CHAIN, FFN_CHAIN, D_FF), BF16)],
        compiler_params=_params("parallel"),
    )(x, w_in, w_out, g, b)


def _ffn_ple_ln_kernel(h_ref, p_ref, w_in_ref, w_out_ref, wg_ref, wp_ref,
                       g2_ref, b2_ref, g3_ref, b3_ref, o_ref, act_ref):
    for c, rows in enumerate(_row_chains(h_ref.shape[0])):
        h = h_ref[rows, :]
        f = _swiglu(h.astype(BF16), w_in_ref, w_out_ref, act_ref.at[c])
        h2 = _layer_norm(ALPHA * h + 0.5 * f, g2_ref[...], b2_ref[...])
        gate = jax.nn.sigmoid(_dot(h2.astype(BF16), wg_ref[...]))
        proj = _dot(p_ref[rows, :].astype(BF16), wp_ref[...])
        o_ref[rows, :] = _layer_norm(ALPHA * h2 + gate * proj, g3_ref[...], b3_ref[...])


def _ffn_ple_ln(h, p, w_in, w_out, wg, wp, g2, b2, g3, b3):
    t = h.shape[0]
    return pl.pallas_call(
        _ffn_ple_ln_kernel,
        name="ffn_ple_ln",
        grid=(t // FFN_TILE,),
        in_specs=[
            _row_spec(FFN_TILE, D_MODEL),
            _row_spec(FFN_TILE, D_PLE),
            _const_spec((D_MODEL, 2 * D_FF)),
            _const_spec((D_FF, D_MODEL)),
            _const_spec((D_MODEL, D_MODEL)),
            _const_spec((D_PLE, D_MODEL)),
            _const_spec((1, D_MODEL)),
            _const_spec((1, D_MODEL)),
            _const_spec((1, D_MODEL)),
            _const_spec((1, D_MODEL)),
        ],
        out_specs=_row_spec(FFN_TILE, D_MODEL),
        out_shape=jax.ShapeDtypeStruct((t, D_MODEL), F32),
        scratch_shapes=[pltpu.VMEM((FFN_TILE // FFN_CHAIN, FFN_CHAIN, D_FF), BF16)],
        compiler_params=_params("parallel"),
    )(h, p, w_in, w_out, wg, wp, g2, b2, g3, b3)


def _rope_tables(pos_ref, invf_ref, dcos_ref, dsin_ref, cos_ref, sin_ref):
    pos = pos_ref[...]
    invf = invf_ref[...]
    base = pos[0:1, :]
    row = lax.broadcasted_iota(jnp.int32, pos.shape, 0).astype(F32)
    exact_f32_int = 2.0 ** 24 - pos.shape[0]
    consecutive = jnp.logical_and(jnp.all(pos == base + row),
                                  jnp.all(jnp.abs(base) < exact_f32_int))
    a = base * invf
    ca = jnp.cos(a)
    sa = jnp.sin(a)
    cos_ref[...] = ca * dcos_ref[...] - sa * dsin_ref[...]
    sin_ref[...] = sa * dcos_ref[...] + ca * dsin_ref[...]
    return consecutive


def _rope_tables_direct(valid, pos_ref, invf_ref, cos_ref, sin_ref):
    @pl.when(jnp.logical_not(valid))
    def _():
        ang = pos_ref[...] * invf_ref[...]
        cos_ref[...] = jnp.cos(ang)
        sin_ref[...] = jnp.sin(ang)


def _retention_kernel(h_ref, pos_ref, invf_ref, dcos_ref, dsin_ref, xi_ref, zeta_ref, dmask_ref,
                      cdec_ref, w_ref, o_ref, rq_ref, rqx_ref, rk_ref, rkz_ref, rv_ref, sg_ref,
                      state_ref, cos_ref, sin_ref, *, tiles_per_seq):
    @pl.when(pl.program_id(0) % tiles_per_seq == 0)
    def _():
        state_ref[...] = jnp.zeros_like(state_ref)

    rope_ok = _rope_tables(pos_ref, invf_ref, dcos_ref, dsin_ref, cos_ref, sin_ref)
    _rope_tables_direct(rope_ok, pos_ref, invf_ref, cos_ref, sin_ref)
    hb = h_ref[...].astype(BF16)
    cos = cos_ref[...]
    lane = lax.broadcasted_iota(jnp.int32, (1, LANES), 1)
    sin = sin_ref[...] * jnp.where(lane < RET_DK // 2, -1.0, 1.0)
    k_scale = RET_DK ** -0.5
    g = _dot_nt(hb, w_ref[2 * RET_QK + RET_V:2 * RET_QK + 2 * RET_V, :])
    sg_ref[...] = (g * jax.nn.sigmoid(g)).astype(BF16)
    q = _dot_nt(hb, w_ref[0:RET_QK, :])
    k = _dot_nt(hb, w_ref[RET_QK:2 * RET_QK, :])
    for h in range(HEADS):
        sl = slice(h * RET_DK, (h + 1) * RET_DK)
        tq = q[:, sl]
        tq = tq * cos + pltpu.roll(tq, RET_DK // 2, 1) * sin
        rq_ref[:, sl] = tq.astype(BF16)
        rqx_ref[:, sl] = (tq * xi_ref[:, sl]).astype(BF16)
        tk = k[:, sl]
        tk = (tk * cos + pltpu.roll(tk, RET_DK // 2, 1) * sin) * k_scale
        rk_ref[:, sl] = tk.astype(BF16)
        rkz_ref[:, sl] = (tk * zeta_ref[:, sl]).astype(BF16)
    rv_ref[...] = _dot_nt(hb, w_ref[2 * RET_QK:2 * RET_QK + RET_V, :]).astype(BF16)

    def qk(rows, h):
        qs = slice(h * RET_DK, (h + 1) * RET_DK)
        return lax.dot_general(rq_ref[rows, qs], rk_ref[rows, qs], (((1,), (1,)), ((), ())),
                               preferred_element_type=F32)

    for blk in range(TOKEN_TILE // RET_BLOCK):
        rows = slice(blk * RET_BLOCK, (blk + 1) * RET_BLOCK)
        s_next = qk(rows, 0)
        for h in range(HEADS):
            qs = slice(h * RET_DK, (h + 1) * RET_DK)
            vs = slice(h * RET_DV, (h + 1) * RET_DV)
            v = rv_ref[rows, vs]
            state = state_ref[h]
            s = s_next
            if h + 1 < HEADS:
                s_next = qk(rows, h + 1)
            cross = _dot(rqx_ref[rows, qs], state.astype(BF16))
            upd = lax.dot_general(rkz_ref[rows, qs], v, (((0,), (0,)), ((), ())),
                                  preferred_element_type=F32)
            state_ref[h] = state * cdec_ref[h] + upd
            y = _dot((s * dmask_ref[h]).astype(BF16), v) + cross
            mu = jnp.mean(y, axis=-1, keepdims=True)
            d = y - mu
            var = jnp.mean(d * d, axis=-1, keepdims=True)
            o_ref[rows, vs] = (d * lax.rsqrt(var + EPS) * sg_ref[rows, vs].astype(F32)).astype(BF16)


def _retention(h, pos_b, rope, xi, zeta, dmask, cdec, w, seq_len):
    t = h.shape[0]
    return pl.pallas_call(
        functools.partial(_retention_kernel, tiles_per_seq=seq_len // TOKEN_TILE),
        name="retention",
        grid=(t // TOKEN_TILE,),
        in_specs=[
            _row_spec(TOKEN_TILE, D_MODEL),
            _row_spec(TOKEN_TILE, LANES),
            _const_spec((1, LANES)),
            _const_spec((TOKEN_TILE, LANES)),
            _const_spec((TOKEN_TILE, LANES)),
            _const_spec((TOKEN_TILE, RET_QK)),
            _const_spec((TOKEN_TILE, RET_QK)),
            _const_spec((HEADS, RET_BLOCK, RET_BLOCK)),
            _const_spec((HEADS, 1, RET_DV)),
            _const_spec((2 * RET_QK + 2 * RET_V, D_MODEL)),
        ],
        out_specs=_row_spec(TOKEN_TILE, RET_V),
        out_shape=jax.ShapeDtypeStruct((t, RET_V), BF16),
        scratch_shapes=[pltpu.VMEM((TOKEN_TILE, RET_QK), BF16)] * 4
        + [pltpu.VMEM((TOKEN_TILE, RET_V), BF16)] * 2
        + [pltpu.VMEM((HEADS, RET_DK, RET_DV), F32)]
        + [pltpu.VMEM((TOKEN_TILE, LANES), F32)] * 2,
        compiler_params=_params("arbitrary"),
    )(h, pos_b, *rope, xi, zeta, dmask, cdec, w)


MLA_LAT = Q_LORA + KV_LORA + LANES


def _mla_proj_kernel(h_ref, pos_ref, invf_ref, dcos_ref, dsin_ref, w_ref, qg_ref, kvg_ref,
                     wuq_ref, wuk_ref, wuvt_ref, qcat_ref, kcat_ref, vt_ref, cos_ref, sin_ref):
    rope_ok = _rope_tables(pos_ref, invf_ref, dcos_ref, dsin_ref, cos_ref, sin_ref)
    _rope_tables_direct(rope_ok, pos_ref, invf_ref, cos_ref, sin_ref)
    hb = h_ref[...].astype(BF16)
    lane = lax.broadcasted_iota(jnp.int32, (1, LANES), 1)
    first_half = lane < MLA_ROPE

    lat = _dot_nt(hb, w_ref[...])
    cq = lat[:, 0:Q_LORA]
    ckv = lat[:, Q_LORA:Q_LORA + KV_LORA]
    kpe = lat[:, Q_LORA + KV_LORA:MLA_LAT]
    kpe = kpe + pltpu.roll(kpe, MLA_ROPE, 1)
    kpe_rot = pltpu.roll(kpe, MLA_ROPE // 2, 1) * jnp.where(
        lane % MLA_ROPE < MLA_ROPE // 2, -1.0, 1.0)

    qf = _dot(_rms_norm(cq, qg_ref[...]).astype(BF16), wuq_ref[...])
    nope_w = HEADS * MLA_NOPE
    rope_w = HEADS * MLA_ROPE
    for h in range(HEADS):
        qcat_ref[:, h * MLA_QK:h * MLA_QK + MLA_NOPE] = (
            qf[:, h * MLA_NOPE:(h + 1) * MLA_NOPE].astype(BF16))

    ckvn = _rms_norm(ckv, kvg_ref[...])
    kv = _dot(ckvn.astype(BF16), wuk_ref[...])
    vt = _dot(wuvt_ref[...], ckvn.T.astype(BF16)).astype(BF16)
    ones = jnp.ones((MLA_DVA - MLA_DV, vt.shape[1]), BF16)
    for h in range(HEADS):
        vt_ref[0, h * MLA_DVA:h * MLA_DVA + MLA_DV, :] = vt[h * MLA_DV:(h + 1) * MLA_DV, :]
        vt_ref[0, h * MLA_DVA + MLA_DV:(h + 1) * MLA_DVA, :] = ones
        kcat_ref[:, h * MLA_QK:h * MLA_QK + MLA_NOPE] = (
            kv[:, h * MLA_NOPE:(h + 1) * MLA_NOPE].astype(BF16))

    cos = cos_ref[...]
    sin = sin_ref[...]
    for j in range(HEADS // 2):
        r = (qf[:, nope_w + j * LANES:nope_w + (j + 1) * LANES] * cos
             + qf[:, nope_w + rope_w + j * LANES:nope_w + rope_w + (j + 1) * LANES] * sin)
        base = 2 * j * MLA_QK
        qcat_ref[:, base + MLA_NOPE:base + MLA_QK] = jnp.where(first_half, r, 0.0).astype(BF16)
        base += MLA_QK
        qcat_ref[:, base + MLA_NOPE:base + MLA_QK] = jnp.where(first_half, 0.0, r).astype(BF16)
    kr = kpe * cos + kpe_rot * sin
    kr_even = jnp.where(first_half, kr, 0.0).astype(BF16)
    kr_odd = jnp.where(first_half, 0.0, kr).astype(BF16)
    for h in range(HEADS):
        kcat_ref[:, h * MLA_QK + MLA_NOPE:(h + 1) * MLA_QK] = kr_even if h % 2 == 0 else kr_odd


def _mla_proj(h, pos_b, rope, w, qg, kvg, wuq, wuk, wuvt):
    t = h.shape[0]
    uq_cols = HEADS * (MLA_NOPE + 2 * MLA_ROPE)
    assert TOKEN_TILE == ATT_K
    return pl.pallas_call(
        _mla_proj_kernel,
        name="mla_proj",
        grid=(t // TOKEN_TILE,),
        in_specs=[
            _row_spec(TOKEN_TILE, D_MODEL),
            _row_spec(TOKEN_TILE, LANES),
            _const_spec((1, LANES)),
            _const_spec((TOKEN_TILE, LANES)),
            _const_spec((TOKEN_TILE, LANES)),
            _const_spec((MLA_LAT, D_MODEL)),
            _const_spec((1, Q_LORA)),
            _const_spec((1, KV_LORA)),
            _const_spec((Q_LORA, uq_cols)),
            _const_spec((KV_LORA, HEADS * MLA_NOPE)),
            _const_spec((HEADS * MLA_DV, KV_LORA)),
        ],
        out_specs=[
            _row_spec(TOKEN_TILE, HEADS * MLA_QK),
            _row_spec(TOKEN_TILE, HEADS * MLA_QK),
            pl.BlockSpec((1, HEADS * MLA_DVA, TOKEN_TILE), lambda i: (i, 0, 0)),
        ],
        out_shape=[
            jax.ShapeDtypeStruct((t, HEADS * MLA_QK), BF16),
            jax.ShapeDtypeStruct((t, HEADS * MLA_QK), BF16),
            jax.ShapeDtypeStruct((t // TOKEN_TILE, HEADS * MLA_DVA, TOKEN_TILE), BF16),
        ],
        scratch_shapes=[pltpu.VMEM((TOKEN_TILE, LANES), F32)] * 2,
        compiler_params=_params("parallel"),
    )(h, pos_b, *rope, w, qg, kvg, wuq, wuk, wuvt)


def _softmax_step(st, mx, vt, stats):
    m, acc = stats
    m_new = jnp.maximum(m, mx)
    a = jnp.exp2(m - m_new)
    p = jnp.exp2(st - m_new)
    acc = a * acc + _dot(vt, p.astype(BF16))
    return m_new, acc


def _mla_attn_kernel(q_ref, qn_ref, k_ref, vt_ref, o_ref, s0_ref, s1_ref, mx0_ref, mx1_ref,
                     m_ref, acc_ref):
    qi = pl.program_id(2)
    heads = range(ATT_HEADS)

    def scores(qr, j, buf):
        s_ref, mx_ref = buf
        off = pl.multiple_of(j * ATT_K, ATT_K)
        for hh in heads:
            st = lax.dot_general(
                k_ref[0, pl.ds(off, ATT_K), hh * MLA_QK:(hh + 1) * MLA_QK],
                qr[0, :, hh * MLA_QK:(hh + 1) * MLA_QK],
                (((1,), (1,)), ((), ())), preferred_element_type=F32)
            s_ref[hh] = st
            mx_ref[hh] = jnp.max(st, axis=0, keepdims=True)

    def consume(j, buf, mask=None):
        s_ref, mx_ref = buf
        for hh in heads:
            st = s_ref[hh]
            if mask is None:
                mx = mx_ref[hh]
            else:
                st = jnp.where(mask, st, -jnp.inf)
                mx = jnp.max(st, axis=0, keepdims=True)
            m, acc = _softmax_step(st, mx, vt_ref[0, j, hh * MLA_DVA:(hh + 1) * MLA_DVA, :],
                                   (m_ref[hh], acc_ref[hh]))
            m_ref[hh] = m
            acc_ref[hh] = acc

    m_ref[...] = jnp.full_like(m_ref, NEG_BIG)
    acc_ref[...] = jnp.zeros_like(acc_ref)

    bufs = ((s0_ref, mx0_ref), (s1_ref, mx1_ref))

    @pl.when(qi == 0)
    def _():
        scores(q_ref, 0, bufs[0])

    key_chunk = lax.broadcasted_iota(jnp.int32, (ATT_K, ATT_Q), 0) // CHUNK
    qry_chunk = lax.broadcasted_iota(jnp.int32, (ATT_K, ATT_Q), 1) // CHUNK
    mask = key_chunk <= qry_chunk

    def step(first, odd):
        cur, oth = bufs[first], bufs[1 - first]

        def pair(jj, c):
            j = 2 * jj
            scores(q_ref, j + 1, oth)
            consume(j, cur)
            scores(q_ref, j + 2, cur)
            consume(j + 1, oth)
            return c

        lax.fori_loop(0, qi // 2, pair, 0)
        if odd:
            scores(q_ref, qi, oth)
            consume(qi - 1, cur)
            scores(qn_ref, 0, cur)
            consume(qi, oth, mask)
        else:
            scores(qn_ref, 0, oth)
            consume(qi, cur, mask)

    for c in range(4):
        pl.when(qi % 4 == c)(functools.partial(step, ((c + 1) // 2) % 2, c % 2 == 1))

    for hh in heads:
        l = acc_ref[hh, MLA_DV:MLA_DV + 1, :]
        o_ref[0, :, hh * MLA_DV:(hh + 1) * MLA_DV] = (
            acc_ref[hh, 0:MLA_DV, :] / l).T.astype(BF16)


def _mla_attn(qcat, kcat, vt):
    b, s, _ = qcat.shape
    assert ATT_Q == ATT_K
    n_kb = s // ATT_K
    last_q = s // ATT_Q - 1
    return pl.pallas_call(
        _mla_attn_kernel,
        name="mla_attn",
        grid=(b, HEADS // ATT_HEADS, s // ATT_Q),
        in_specs=[
            pl.BlockSpec((1, ATT_Q, ATT_HEADS * MLA_QK), lambda i, h, j: (i, j, h)),
            pl.BlockSpec((1, ATT_Q, ATT_HEADS * MLA_QK),
                         lambda i, h, j: (i, jnp.minimum(j + 1, last_q), h)),
            pl.BlockSpec((1, s, ATT_HEADS * MLA_QK), lambda i, h, j: (i, 0, h)),
            pl.BlockSpec((1, n_kb, ATT_HEADS * MLA_DVA, ATT_K), lambda i, h, j: (i, 0, h, 0)),
        ],
        out_specs=pl.BlockSpec((1, ATT_Q, ATT_HEADS * MLA_DV), lambda i, h, j: (i, j, h)),
        out_shape=jax.ShapeDtypeStruct((b, s, HEADS * MLA_DV), BF16),
        scratch_shapes=[
            pltpu.VMEM((ATT_HEADS, ATT_K, ATT_Q), F32),
            pltpu.VMEM((ATT_HEADS, ATT_K, ATT_Q), F32),
            pltpu.VMEM((ATT_HEADS, 1, ATT_Q), F32),
            pltpu.VMEM((ATT_HEADS, 1, ATT_Q), F32),
            pltpu.VMEM((ATT_HEADS, 1, ATT_Q), F32),
            pltpu.VMEM((ATT_HEADS, MLA_DVA, ATT_Q), F32),
        ],
        compiler_params=_params("parallel", "parallel", "arbitrary"),
    )(qcat, qcat, kcat, vt)


def _merge_ln_kernel(yr_ref, om_ref, h_ref, wg_ref, wro_ref, wmo_ref, wout_ref,
                     g_ref, b_ref, o_ref):
    h = h_ref[...]
    hb = h.astype(BF16)
    mix = (jax.nn.sigmoid(_dot_nt(hb, wg_ref[0:D_MODEL, :])) * _dot(yr_ref[...], wro_ref[...])
           + jax.nn.sigmoid(_dot_nt(hb, wg_ref[D_MODEL:2 * D_MODEL, :]))
           * _dot(om_ref[...], wmo_ref[...]))
    mixed = _dot(mix.astype(BF16), wout_ref[...])
    o_ref[...] = _layer_norm(ALPHA * h + mixed, g_ref[...], b_ref[...])


def _merge_ln(yr, om, h, wg, wro, wmo, wout, g, b):
    t = h.shape[0]
    return pl.pallas_call(
        _merge_ln_kernel,
        name="merge_ln",
        grid=(t // TOKEN_TILE,),
        in_specs=[
            _row_spec(TOKEN_TILE, RET_V),
            _row_spec(TOKEN_TILE, HEADS * MLA_DV),
            _row_spec(TOKEN_TILE, D_MODEL),
            _const_spec((2 * D_MODEL, D_MODEL)),
            _const_spec((RET_V, D_MODEL)),
            _const_spec((HEADS * MLA_DV, D_MODEL)),
            _const_spec((D_MODEL, D_MODEL)),
            _const_spec((1, D_MODEL)),
            _const_spec((1, D_MODEL)),
        ],
        out_specs=_row_spec(TOKEN_TILE, D_MODEL),
        out_shape=jax.ShapeDtypeStruct((t, D_MODEL), F32),
        compiler_params=_params("parallel"),
    )(yr, om, h, wg, wro, wmo, wout, g, b)


def _retention_tables():
    log_gamma = jnp.log(1.0 - 2.0 ** (-5.0 - jnp.arange(HEADS, dtype=F32)))
    idx = jnp.arange(RET_BLOCK, dtype=F32)
    chunk = jnp.arange(RET_BLOCK) // CHUNK
    visible = chunk[None, :] <= chunk[:, None]
    dist = jnp.abs(idx[:, None] - idx[None, :])
    dmask = jnp.where(visible[None], jnp.exp(log_gamma[:, None, None] * dist[None]), 0.0)
    xi = jnp.exp(log_gamma[:, None] * (idx + 1.0))
    zeta = jnp.exp(log_gamma[:, None] * (RET_BLOCK - 1.0 - idx))
    cdec = jnp.exp(log_gamma * RET_BLOCK)
    def per_token_rows(tab):
        rows = jnp.broadcast_to(tab.T[:, :, None], (RET_BLOCK, HEADS, RET_DK))
        return jnp.tile(rows.reshape(RET_BLOCK, RET_QK), (TOKEN_TILE // RET_BLOCK, 1))

    return (dmask, per_token_rows(xi), per_token_rows(zeta),
            jnp.broadcast_to(cdec[:, None, None], (HEADS, 1, RET_DV)))


def _rope_consts(half):
    invf = ROPE_BASE ** (-jnp.arange(half, dtype=F32) / half)
    invf = jnp.tile(invf, LANES // half).reshape(1, LANES)
    delta = jnp.arange(TOKEN_TILE, dtype=F32)[:, None] * invf
    return invf, jnp.cos(delta), jnp.sin(delta)


def kernel(x, p, positions, ln_g, ln_b, ffn1_w_in, ffn1_w_out, w_in, ret_gn_g, w_ret_o,
           q_norm_g, kv_norm_g, w_uq, w_ukv, w_mla_o, w_out, ffn2_w_in, ffn2_w_out,
           ple_w_gate, ple_w_proj):
    b, s, d = x.shape
    t = b * s
    h = x.reshape(t, d)
    pos_b = jnp.broadcast_to(positions.astype(F32).reshape(t, 1), (t, LANES))
    row = lambda v: v.reshape(1, -1)

    for i in range(ln_g.shape[0]):
        w_in_t = jnp.swapaxes(w_in[i], 0, 1)
        c0 = 2 * RET_QK + 2 * RET_V
        c1 = c0 + Q_LORA + KV_LORA + MLA_ROPE
        w_ret = w_in_t[:c0].astype(BF16)
        w_mla = jnp.pad(w_in_t[c0:c1], ((0, LANES - MLA_ROPE), (0, 0))).astype(BF16)
        w_gates = w_in_t[c1:].astype(BF16)

        q_scale = (MLA_NOPE + MLA_ROPE) ** -0.5 * math.log2(math.e)
        uq = w_uq[i].reshape(Q_LORA, HEADS, MLA_NOPE + MLA_ROPE) * q_scale
        uq_nope = uq[:, :, :MLA_NOPE].reshape(Q_LORA, HEADS * MLA_NOPE)
        uq_rope = uq[:, :, MLA_NOPE:]
        uq_rot = jnp.concatenate(
            [-uq_rope[:, :, MLA_ROPE // 2:], uq_rope[:, :, :MLA_ROPE // 2]], axis=2)
        wuq = jnp.concatenate(
            [uq_nope, uq_rope.reshape(Q_LORA, -1), uq_rot.reshape(Q_LORA, -1)], axis=1).astype(BF16)
        ukv = w_ukv[i].reshape(KV_LORA, HEADS, MLA_NOPE + MLA_DV)
        wuk = ukv[:, :, :MLA_NOPE].reshape(KV_LORA, -1).astype(BF16)
        wuvt = ukv[:, :, MLA_NOPE:].reshape(KV_LORA, -1).T.astype(BF16)

        rope_ret = _rope_consts(RET_DK // 2)
        rope_mla = _rope_consts(MLA_ROPE // 2)

        h = _ffn_ln(h, ffn1_w_in[i].astype(BF16), ffn1_w_out[i].astype(BF16),
                    row(ln_g[i, 0]), row(ln_b[i, 0]))

        dmask, xi, zeta, cdec = _retention_tables()
        y_ret = _retention(h, pos_b, rope_ret, xi, zeta, dmask, cdec, w_ret, s)
        qcat, kcat, vt = _mla_proj(
            h, pos_b, rope_mla, w_mla, row(q_norm_g[i]), row(kv_norm_g[i]), wuq, wuk, wuvt)
        o_mla = _mla_attn(qcat.reshape(b, s, -1), kcat.reshape(b, s, -1),
                          vt.reshape(b, s // ATT_K, HEADS * MLA_DVA, ATT_K))
        w_ro = (ret_gn_g[i][:, None] * w_ret_o[i]).astype(BF16)
        h = _merge_ln(y_ret, o_mla.reshape(t, -1), h, w_gates,
                      w_ro, w_mla_o[i].astype(BF16), w_out[i].astype(BF16),
                      row(ln_g[i, 1]), row(ln_b[i, 1]))

        h = _ffn_ple_ln(h, p[i].reshape(t, -1), ffn2_w_in[i].astype(BF16),
                        ffn2_w_out[i].astype(BF16), ple_w_gate[i].astype(BF16),
                        ple_w_proj[i].astype(BF16), row(ln_g[i, 2]), row(ln_b[i, 2]),
                        row(ln_g[i, 3]), row(ln_b[i, 3]))
    return h.reshape(b, s, d)
```

```python
import functools
import math

import jax
import jax.numpy as jnp
from jax import lax
from jax.experimental import pallas as pl
from jax.experimental.pallas import tpu as pltpu

F32 = jnp.float32
BF16 = jnp.bfloat16

D_MODEL = 1024
CHUNK = 64
D_PLE = 256
D_FF = 2816
HEADS = 8
RET_DK = 128
RET_DV = 256
RET_QK = HEADS * RET_DK
RET_V = HEADS * RET_DV
MLA_NOPE = 128
MLA_ROPE = 64
MLA_DV = 128
Q_LORA = 256
KV_LORA = 256
ROPE_BASE = 10000.0
EPS = 1e-5
ALPHA = 2.0 ** 0.25

LANES = 128
FF_CHUNK = 256
TOKEN_TILE = 512
FFN_TILE = 1024
FFN_CHAIN = 512
RET_BLOCK = 256
ATT_Q = 512
ATT_K = 512
ATT_HEADS = 2
MLA_QK = 256
MLA_DVA = MLA_DV + 16
VMEM_LIMIT = 52 * 1024 * 1024
NEG_BIG = -1e30


def _const_spec(shape):
    nd = len(shape)
    return pl.BlockSpec(shape, lambda *_: (0,) * nd, pipeline_mode=pl.Buffered(1))


def _row_spec(tile, width):
    return pl.BlockSpec((tile, width), lambda i: (i, 0))


def _params(*sem):
    return pltpu.CompilerParams(dimension_semantics=sem, vmem_limit_bytes=VMEM_LIMIT)


def _dot(a, b):
    return jnp.dot(a, b, preferred_element_type=F32)


def _dot_nt(a, bt):
    return lax.dot_general(a, bt, (((1,), (1,)), ((), ())), preferred_element_type=F32)


def _layer_norm(y, g, b):
    mu = jnp.mean(y, axis=-1, keepdims=True)
    d = y - mu
    var = jnp.mean(d * d, axis=-1, keepdims=True)
    return d * lax.rsqrt(var + EPS) * g + b


def _rms_norm(x, g):
    return x * lax.rsqrt(jnp.mean(x * x, axis=-1, keepdims=True) + EPS) * g


def _swiglu(xb, w_in_ref, w_out_ref, act_ref):
    for c in range(D_FF // FF_CHUNK):
        lo = c * FF_CHUNK
        g = _dot(xb, w_in_ref[:, lo:lo + FF_CHUNK])
        u = _dot(xb, w_in_ref[:, D_FF + lo:D_FF + lo + FF_CHUNK])
        act_ref[:, lo:lo + FF_CHUNK] = (g * jax.nn.sigmoid(g) * u).astype(BF16)
    return _dot(act_ref[...], w_out_ref[...])


def _row_chains(tile):
    return [slice(r, r + FFN_CHAIN) for r in range(0, tile, FFN_CHAIN)]


def _ffn_ln_mla_kernel(x_ref, pos_ref, invf_ref, dcos_ref, dsin_ref, w_in_ref, w_out_ref,
                       g_ref, b_ref, wlat_ref, qg_ref, kvg_ref, wuq_ref, wuk_ref, wuvt_ref,
                       o_ref, qcat_ref, kcat_ref, vt_ref, act_ref, cos_ref, sin_ref):
    rope_ok = _rope_tables(pos_ref, invf_ref, dcos_ref, dsin_ref, cos_ref, sin_ref)
    _rope_tables_direct(rope_ok, pos_ref, invf_ref, cos_ref, sin_ref)
    x = x_ref[...]
    f = _swiglu(x.astype(BF16), w_in_ref, w_out_ref, act_ref)
    h = _layer_norm(ALPHA * x + 0.5 * f, g_ref[...], b_ref[...])
    o_ref[...] = h
    _mla_proj(h.astype(BF16), wlat_ref, qg_ref, kvg_ref, wuq_ref, wuk_ref, wuvt_ref,
              qcat_ref, kcat_ref, vt_ref, cos_ref, sin_ref)


def _ffn_ln_mla(x, pos_b, rope, w_in, w_out, g, b, wlat, qg, kvg, wuq, wuk, wuvt):
    t = x.shape[0]
    uq_cols = HEADS * (MLA_NOPE + 2 * MLA_ROPE)
    assert TOKEN_TILE == ATT_K
    return pl.pallas_call(
        _ffn_ln_mla_kernel,
        name="ffn_ln_mla",
        grid=(t // TOKEN_TILE,),
        in_specs=[
            _row_spec(TOKEN_TILE, D_MODEL),
            _row_spec(TOKEN_TILE, LANES),
            _const_spec((1, LANES)),
            _const_spec((TOKEN_TILE, LANES)),
            _const_spec((TOKEN_TILE, LANES)),
            _const_spec((D_MODEL, 2 * D_FF)),
            _const_spec((D_FF, D_MODEL)),
            _const_spec((1, D_MODEL)),
            _const_spec((1, D_MODEL)),
            _const_spec((MLA_LAT, D_MODEL)),
            _const_spec((1, Q_LORA)),
            _const_spec((1, KV_LORA)),
            _const_spec((Q_LORA, uq_cols)),
            _const_spec((KV_LORA, HEADS * MLA_NOPE)),
            _const_spec((HEADS * MLA_DV, KV_LORA)),
        ],
        out_specs=[
            _row_spec(TOKEN_TILE, D_MODEL),
            _row_spec(TOKEN_TILE, HEADS * MLA_QK),
            _row_spec(TOKEN_TILE, HEADS * MLA_QK),
            pl.BlockSpec((1, HEADS * MLA_DVA, TOKEN_TILE), lambda i: (i, 0, 0)),
        ],
        out_shape=[
            jax.ShapeDtypeStruct((t, D_MODEL), F32),
            jax.ShapeDtypeStruct((t, HEADS * MLA_QK), BF16),
            jax.ShapeDtypeStruct((t, HEADS * MLA_QK), BF16),
            jax.ShapeDtypeStruct((t // TOKEN_TILE, HEADS * MLA_DVA, TOKEN_TILE), BF16),
        ],
        scratch_shapes=[pltpu.VMEM((TOKEN_TILE, D_FF), BF16)]
        + [pltpu.VMEM((TOKEN_TILE, LANES), F32)] * 2,
        compiler_params=_params("parallel"),
    )(x, pos_b, *rope, w_in, w_out, g, b, wlat, qg, kvg, wuq, wuk, wuvt)


def _ffn_ple_ln_kernel(h_ref, p_ref, w_in_ref, w_out_ref, wg_ref, wp_ref,
                       g2_ref, b2_ref, g3_ref, b3_ref, o_ref, act_ref):
    for c, rows in enumerate(_row_chains(h_ref.shape[0])):
        h = h_ref[rows, :]
        f = _swiglu(h.astype(BF16), w_in_ref, w_out_ref, act_ref.at[c])
        h2 = _layer_norm(ALPHA * h + 0.5 * f, g2_ref[...], b2_ref[...])
        gate = jax.nn.sigmoid(_dot(h2.astype(BF16), wg_ref[...]))
        proj = _dot(p_ref[rows, :].astype(BF16), wp_ref[...])
        o_ref[rows, :] = _layer_norm(ALPHA * h2 + gate * proj, g3_ref[...], b3_ref[...])


def _ffn_ple_ln(h, p, w_in, w_out, wg, wp, g2, b2, g3, b3):
    t = h.shape[0]
    return pl.pallas_call(
        _ffn_ple_ln_kernel,
        name="ffn_ple_ln",
        grid=(t // FFN_TILE,),
        in_specs=[
            _row_spec(FFN_TILE, D_MODEL),
            _row_spec(FFN_TILE, D_PLE),
            _const_spec((D_MODEL, 2 * D_FF)),
            _const_spec((D_FF, D_MODEL)),
            _const_spec((D_MODEL, D_MODEL)),
            _const_spec((D_PLE, D_MODEL)),
            _const_spec((1, D_MODEL)),
            _const_spec((1, D_MODEL)),
            _const_spec((1, D_MODEL)),
            _const_spec((1, D_MODEL)),
        ],
        out_specs=_row_spec(FFN_TILE, D_MODEL),
        out_shape=jax.ShapeDtypeStruct((t, D_MODEL), F32),
        scratch_shapes=[pltpu.VMEM((FFN_TILE // FFN_CHAIN, FFN_CHAIN, D_FF), BF16)],
        compiler_params=_params("parallel"),
    )(h, p, w_in, w_out, wg, wp, g2, b2, g3, b3)


def _rope_tables(pos_ref, invf_ref, dcos_ref, dsin_ref, cos_ref, sin_ref):
    pos = pos_ref[...]
    invf = invf_ref[...]
    base = pos[0:1, :]
    row = lax.broadcasted_iota(jnp.int32, pos.shape, 0).astype(F32)
    exact_f32_int = 2.0 ** 24 - pos.shape[0]
    consecutive = jnp.logical_and(jnp.all(pos == base + row),
                                  jnp.all(jnp.abs(base) < exact_f32_int))
    a = base * invf
    ca = jnp.cos(a)
    sa = jnp.sin(a)
    cos_ref[...] = ca * dcos_ref[...] - sa * dsin_ref[...]
    sin_ref[...] = sa * dcos_ref[...] + ca * dsin_ref[...]
    return consecutive


def _rope_tables_direct(valid, pos_ref, invf_ref, cos_ref, sin_ref):
    @pl.when(jnp.logical_not(valid))
    def _():
        ang = pos_ref[...] * invf_ref[...]
        cos_ref[...] = jnp.cos(ang)
        sin_ref[...] = jnp.sin(ang)


def _retention_kernel(h_ref, pos_ref, invf_ref, dcos_ref, dsin_ref, xi_ref, zeta_ref, dmask_ref,
                      cdec_ref, w_ref, o_ref, rq_ref, rqx_ref, rk_ref, rkz_ref, rv_ref, sg_ref,
                      state_ref, cos_ref, sin_ref, *, tiles_per_seq):
    @pl.when(pl.program_id(0) % tiles_per_seq == 0)
    def _():
        state_ref[...] = jnp.zeros_like(state_ref)

    rope_ok = _rope_tables(pos_ref, invf_ref, dcos_ref, dsin_ref, cos_ref, sin_ref)
    _rope_tables_direct(rope_ok, pos_ref, invf_ref, cos_ref, sin_ref)
    hb = h_ref[...].astype(BF16)
    cos = cos_ref[...]
    lane = lax.broadcasted_iota(jnp.int32, (1, LANES), 1)
    sin = sin_ref[...] * jnp.where(lane < RET_DK // 2, -1.0, 1.0)
    k_scale = RET_DK ** -0.5
    g = _dot_nt(hb, w_ref[2 * RET_QK + RET_V:2 * RET_QK + 2 * RET_V, :])
    sg_ref[...] = (g * jax.nn.sigmoid(g)).astype(BF16)
    q = _dot_nt(hb, w_ref[0:RET_QK, :])
    k = _dot_nt(hb, w_ref[RET_QK:2 * RET_QK, :])
    for h in range(HEADS):
        sl = slice(h * RET_DK, (h + 1) * RET_DK)
        tq = q[:, sl]
        tq = tq * cos + pltpu.roll(tq, RET_DK // 2, 1) * sin
        rq_ref[:, sl] = tq.astype(BF16)
        rqx_ref[:, sl] = (tq * xi_ref[:, sl]).astype(BF16)
        tk = k[:, sl]
        tk = (tk * cos + pltpu.roll(tk, RET_DK // 2, 1) * sin) * k_scale
        rk_ref[:, sl] = tk.astype(BF16)
        rkz_ref[:, sl] = (tk * zeta_ref[:, sl]).astype(BF16)
    rv_ref[...] = _dot_nt(hb, w_ref[2 * RET_QK:2 * RET_QK + RET_V, :]).astype(BF16)

    def qk(rows, h):
        qs = slice(h * RET_DK, (h + 1) * RET_DK)
        return lax.dot_general(rq_ref[rows, qs], rk_ref[rows, qs], (((1,), (1,)), ((), ())),
                               preferred_element_type=F32)

    for blk in range(TOKEN_TILE // RET_BLOCK):
        rows = slice(blk * RET_BLOCK, (blk + 1) * RET_BLOCK)
        s_next = qk(rows, 0)
        for h in range(HEADS):
            qs = slice(h * RET_DK, (h + 1) * RET_DK)
            vs = slice(h * RET_DV, (h + 1) * RET_DV)
            v = rv_ref[rows, vs]
            state = state_ref[h]
            s = s_next
            if h + 1 < HEADS:
                s_next = qk(rows, h + 1)
            cross = _dot(rqx_ref[rows, qs], state.astype(BF16))
            upd = lax.dot_general(rkz_ref[rows, qs], v, (((0,), (0,)), ((), ())),
                                  preferred_element_type=F32)
            state_ref[h] = state * cdec_ref[h] + upd
            y = _dot((s * dmask_ref[h]).astype(BF16), v) + cross
            mu = jnp.mean(y, axis=-1, keepdims=True)
            d = y - mu
            var = jnp.mean(d * d, axis=-1, keepdims=True)
            o_ref[rows, vs] = (d * lax.rsqrt(var + EPS) * sg_ref[rows, vs].astype(F32)).astype(BF16)


def _retention(h, pos_b, rope, xi, zeta, dmask, cdec, w, seq_len):
    t = h.shape[0]
    return pl.pallas_call(
        functools.partial(_retention_kernel, tiles_per_seq=seq_len // TOKEN_TILE),
        name="retention",
        grid=(t // TOKEN_TILE,),
        in_specs=[
            _row_spec(TOKEN_TILE, D_MODEL),
            _row_spec(TOKEN_TILE, LANES),
            _const_spec((1, LANES)),
            _const_spec((TOKEN_TILE, LANES)),
            _const_spec((TOKEN_TILE, LANES)),
            _const_spec((TOKEN_TILE, RET_QK)),
            _const_spec((TOKEN_TILE, RET_QK)),
            _const_spec((HEADS, RET_BLOCK, RET_BLOCK)),
            _const_spec((HEADS, 1, RET_DV)),
            _const_spec((2 * RET_QK + 2 * RET_V, D_MODEL)),
        ],
        out_specs=_row_spec(TOKEN_TILE, RET_V),
        out_shape=jax.ShapeDtypeStruct((t, RET_V), BF16),
        scratch_shapes=[pltpu.VMEM((TOKEN_TILE, RET_QK), BF16)] * 4
        + [pltpu.VMEM((TOKEN_TILE, RET_V), BF16)] * 2
        + [pltpu.VMEM((HEADS, RET_DK, RET_DV), F32)]
        + [pltpu.VMEM((TOKEN_TILE, LANES), F32)] * 2,
        compiler_params=_params("arbitrary"),
    )(h, pos_b, *rope, xi, zeta, dmask, cdec, w)


MLA_LAT = Q_LORA + KV_LORA + LANES


def _mla_proj(hb, w_ref, qg_ref, kvg_ref, wuq_ref, wuk_ref, wuvt_ref,
              qcat_ref, kcat_ref, vt_ref, cos_ref, sin_ref):
    lane = lax.broadcasted_iota(jnp.int32, (1, LANES), 1)
    first_half = lane < MLA_ROPE

    lat = _dot_nt(hb, w_ref[...])
    cq = lat[:, 0:Q_LORA]
    ckv = lat[:, Q_LORA:Q_LORA + KV_LORA]
    kpe = lat[:, Q_LORA + KV_LORA:MLA_LAT]
    kpe = kpe + pltpu.roll(kpe, MLA_ROPE, 1)
    kpe_rot = pltpu.roll(kpe, MLA_ROPE // 2, 1) * jnp.where(
        lane % MLA_ROPE < MLA_ROPE // 2, -1.0, 1.0)

    qf = _dot(_rms_norm(cq, qg_ref[...]).astype(BF16), wuq_ref[...])
    nope_w = HEADS * MLA_NOPE
    rope_w = HEADS * MLA_ROPE
    for h in range(HEADS):
        qcat_ref[:, h * MLA_QK:h * MLA_QK + MLA_NOPE] = (
            qf[:, h * MLA_NOPE:(h + 1) * MLA_NOPE].astype(BF16))

    ckvn = _rms_norm(ckv, kvg_ref[...])
    kv = _dot(ckvn.astype(BF16), wuk_ref[...])
    vt = _dot(wuvt_ref[...], ckvn.T.astype(BF16)).astype(BF16)
    ones = jnp.ones((MLA_DVA - MLA_DV, vt.shape[1]), BF16)
    for h in range(HEADS):
        vt_ref[0, h * MLA_DVA:h * MLA_DVA + MLA_DV, :] = vt[h * MLA_DV:(h + 1) * MLA_DV, :]
        vt_ref[0, h * MLA_DVA + MLA_DV:(h + 1) * MLA_DVA, :] = ones
        kcat_ref[:, h * MLA_QK:h * MLA_QK + MLA_NOPE] = (
            kv[:, h * MLA_NOPE:(h + 1) * MLA_NOPE].astype(BF16))

    cos = cos_ref[...]
    sin = sin_ref[...]
    for j in range(HEADS // 2):
        r = (qf[:, nope_w + j * LANES:nope_w + (j + 1) * LANES] * cos
             + qf[:, nope_w + rope_w + j * LANES:nope_w + rope_w + (j + 1) * LANES] * sin)
        base = 2 * j * MLA_QK
        qcat_ref[:, base + MLA_NOPE:base + MLA_QK] = jnp.where(first_half, r, 0.0).astype(BF16)
        base += MLA_QK
        qcat_ref[:, base + MLA_NOPE:base + MLA_QK] = jnp.where(first_half, 0.0, r).astype(BF16)
    kr = kpe * cos + kpe_rot * sin
    kr_even = jnp.where(first_half, kr, 0.0).astype(BF16)
    kr_odd = jnp.where(first_half, 0.0, kr).astype(BF16)
    for h in range(HEADS):
        kcat_ref[:, h * MLA_QK + MLA_NOPE:(h + 1) * MLA_QK] = kr_even if h % 2 == 0 else kr_odd


def _softmax_step(st, mx, vt, stats):
    m, acc = stats
    m_new = jnp.maximum(m, mx)
    a = jnp.exp2(m - m_new)
    p = jnp.exp2(st - m_new)
    acc = a * acc + _dot(vt, p.astype(BF16))
    return m_new, acc


def _mla_attn_kernel(q_ref, qn_ref, k_ref, vt_ref, o_ref, s0_ref, s1_ref, mx0_ref, mx1_ref,
                     m_ref, acc_ref):
    qi = pl.program_id(2)
    heads = range(ATT_HEADS)

    def scores(qr, j, buf):
        s_ref, mx_ref = buf
        off = pl.multiple_of(j * ATT_K, ATT_K)
        for hh in heads:
            st = lax.dot_general(
                k_ref[0, pl.ds(off, ATT_K), hh * MLA_QK:(hh + 1) * MLA_QK],
                qr[0, :, hh * MLA_QK:(hh + 1) * MLA_QK],
                (((1,), (1,)), ((), ())), preferred_element_type=F32)
            s_ref[hh] = st
            mx_ref[hh] = jnp.max(st, axis=0, keepdims=True)

    def consume(j, buf, mask=None):
        s_ref, mx_ref = buf
        for hh in heads:
            st = s_ref[hh]
            if mask is None:
                mx = mx_ref[hh]
            else:
                st = jnp.where(mask, st, -jnp.inf)
                mx = jnp.max(st, axis=0, keepdims=True)
            m, acc = _softmax_step(st, mx, vt_ref[0, j, hh * MLA_DVA:(hh + 1) * MLA_DVA, :],
                                   (m_ref[hh], acc_ref[hh]))
            m_ref[hh] = m
            acc_ref[hh] = acc

    m_ref[...] = jnp.full_like(m_ref, NEG_BIG)
    acc_ref[...] = jnp.zeros_like(acc_ref)

    bufs = ((s0_ref, mx0_ref), (s1_ref, mx1_ref))

    @pl.when(qi == 0)
    def _():
        scores(q_ref, 0, bufs[0])

    key_chunk = lax.broadcasted_iota(jnp.int32, (ATT_K, ATT_Q), 0) // CHUNK
    qry_chunk = lax.broadcasted_iota(jnp.int32, (ATT_K, ATT_Q), 1) // CHUNK
    mask = key_chunk <= qry_chunk

    def step(first, odd):
        cur, oth = bufs[first], bufs[1 - first]

        def pair(jj, c):
            j = 2 * jj
            scores(q_ref, j + 1, oth)
            consume(j, cur)
            scores(q_ref, j + 2, cur)
            consume(j + 1, oth)
            return c

        lax.fori_loop(0, qi // 2, pair, 0)
        if odd:
            scores(q_ref, qi, oth)
            consume(qi - 1, cur)
            scores(qn_ref, 0, cur)
            consume(qi, oth, mask)
        else:
            scores(qn_ref, 0, oth)
            consume(qi, cur, mask)

    for c in range(4):
        pl.when(qi % 4 == c)(functools.partial(step, ((c + 1) // 2) % 2, c % 2 == 1))

    for hh in heads:
        l = acc_ref[hh, MLA_DV:MLA_DV + 1, :]
        o_ref[0, :, hh * MLA_DV:(hh + 1) * MLA_DV] = (
            acc_ref[hh, 0:MLA_DV, :] / l).T.astype(BF16)


def _mla_attn(qcat, kcat, vt):
    b, s, _ = qcat.shape
    assert ATT_Q == ATT_K
    n_kb = s // ATT_K
    last_q = s // ATT_Q - 1
    return pl.pallas_call(
        _mla_attn_kernel,
        name="mla_attn",
        grid=(b, HEADS // ATT_HEADS, s // ATT_Q),
        in_specs=[
            pl.BlockSpec((1, ATT_Q, ATT_HEADS * MLA_QK), lambda i, h, j: (i, j, h)),
            pl.BlockSpec((1, ATT_Q, ATT_HEADS * MLA_QK),
                         lambda i, h, j: (i, jnp.minimum(j + 1, last_q), h)),
            pl.BlockSpec((1, s, ATT_HEADS * MLA_QK), lambda i, h, j: (i, 0, h)),
            pl.BlockSpec((1, n_kb, ATT_HEADS * MLA_DVA, ATT_K), lambda i, h, j: (i, 0, h, 0)),
        ],
        out_specs=pl.BlockSpec((1, ATT_Q, ATT_HEADS * MLA_DV), lambda i, h, j: (i, j, h)),
        out_shape=jax.ShapeDtypeStruct((b, s, HEADS * MLA_DV), BF16),
        scratch_shapes=[
            pltpu.VMEM((ATT_HEADS, ATT_K, ATT_Q), F32),
            pltpu.VMEM((ATT_HEADS, ATT_K, ATT_Q), F32),
            pltpu.VMEM((ATT_HEADS, 1, ATT_Q), F32),
            pltpu.VMEM((ATT_HEADS, 1, ATT_Q), F32),
            pltpu.VMEM((ATT_HEADS, 1, ATT_Q), F32),
            pltpu.VMEM((ATT_HEADS, MLA_DVA, ATT_Q), F32),
        ],
        compiler_params=_params("parallel", "parallel", "arbitrary"),
    )(qcat, qcat, kcat, vt)


def _merge_ln_kernel(yr_ref, om_ref, h_ref, wg_ref, wro_ref, wmo_ref, wout_ref,
                     g_ref, b_ref, o_ref):
    for rows in _row_chains(h_ref.shape[0]):
        h = h_ref[rows, :]
        hb = h.astype(BF16)
        mix = (jax.nn.sigmoid(_dot_nt(hb, wg_ref[0:D_MODEL, :]))
               * _dot(yr_ref[rows, :], wro_ref[...])
               + jax.nn.sigmoid(_dot_nt(hb, wg_ref[D_MODEL:2 * D_MODEL, :]))
               * _dot(om_ref[rows, :], wmo_ref[...]))
        mixed = _dot(mix.astype(BF16), wout_ref[...])
        o_ref[rows, :] = _layer_norm(ALPHA * h + mixed, g_ref[...], b_ref[...])


def _merge_ln(yr, om, h, wg, wro, wmo, wout, g, b):
    t = h.shape[0]
    return pl.pallas_call(
        _merge_ln_kernel,
        name="merge_ln",
        grid=(t // FFN_TILE,),
        in_specs=[
            _row_spec(FFN_TILE, RET_V),
            _row_spec(FFN_TILE, HEADS * MLA_DV),
            _row_spec(FFN_TILE, D_MODEL),
            _const_spec((2 * D_MODEL, D_MODEL)),
            _const_spec((RET_V, D_MODEL)),
            _const_spec((HEADS * MLA_DV, D_MODEL)),
            _const_spec((D_MODEL, D_MODEL)),
            _const_spec((1, D_MODEL)),
            _const_spec((1, D_MODEL)),
        ],
        out_specs=_row_spec(FFN_TILE, D_MODEL),
        out_shape=jax.ShapeDtypeStruct((t, D_MODEL), F32),
        compiler_params=_params("parallel"),
    )(yr, om, h, wg, wro, wmo, wout, g, b)


def _retention_tables():
    log_gamma = jnp.log(1.0 - 2.0 ** (-5.0 - jnp.arange(HEADS, dtype=F32)))
    idx = jnp.arange(RET_BLOCK, dtype=F32)
    chunk = jnp.arange(RET_BLOCK) // CHUNK
    visible = chunk[None, :] <= chunk[:, None]
    dist = jnp.abs(idx[:, None] - idx[None, :])
    dmask = jnp.where(visible[None], jnp.exp(log_gamma[:, None, None] * dist[None]), 0.0)
    xi = jnp.exp(log_gamma[:, None] * (idx + 1.0))
    zeta = jnp.exp(log_gamma[:, None] * (RET_BLOCK - 1.0 - idx))
    cdec = jnp.exp(log_gamma * RET_BLOCK)
    def per_token_rows(tab):
        rows = jnp.broadcast_to(tab.T[:, :, None], (RET_BLOCK, HEADS, RET_DK))
        return jnp.tile(rows.reshape(RET_BLOCK, RET_QK), (TOKEN_TILE // RET_BLOCK, 1))

    return (dmask, per_token_rows(xi), per_token_rows(zeta),
            jnp.broadcast_to(cdec[:, None, None], (HEADS, 1, RET_DV)))


def _rope_consts(half):
    invf = ROPE_BASE ** (-jnp.arange(half, dtype=F32) / half)
    invf = jnp.tile(invf, LANES // half).reshape(1, LANES)
    delta = jnp.arange(TOKEN_TILE, dtype=F32)[:, None] * invf
    return invf, jnp.cos(delta), jnp.sin(delta)


def kernel(x, p, positions, ln_g, ln_b, ffn1_w_in, ffn1_w_out, w_in, ret_gn_g, w_ret_o,
           q_norm_g, kv_norm_g, w_uq, w_ukv, w_mla_o, w_out, ffn2_w_in, ffn2_w_out,
           ple_w_gate, ple_w_proj):
    b, s, d = x.shape
    t = b * s
    h = x.reshape(t, d)
    pos_b = jnp.broadcast_to(positions.astype(F32).reshape(t, 1), (t, LANES))
    row = lambda v: v.reshape(1, -1)

    for i in range(ln_g.shape[0]):
        w_in_t = jnp.swapaxes(w_in[i], 0, 1)
        c0 = 2 * RET_QK + 2 * RET_V
        c1 = c0 + Q_LORA + KV_LORA + MLA_ROPE
        w_ret = w_in_t[:c0].astype(BF16)
        w_mla = jnp.pad(w_in_t[c0:c1], ((0, LANES - MLA_ROPE), (0, 0))).astype(BF16)
        w_gates = w_in_t[c1:].astype(BF16)

        q_scale = (MLA_NOPE + MLA_ROPE) ** -0.5 * math.log2(math.e)
        uq = w_uq[i].reshape(Q_LORA, HEADS, MLA_NOPE + MLA_ROPE) * q_scale
        uq_nope = uq[:, :, :MLA_NOPE].reshape(Q_LORA, HEADS * MLA_NOPE)
        uq_rope = uq[:, :, MLA_NOPE:]
        uq_rot = jnp.concatenate(
            [-uq_rope[:, :, MLA_ROPE // 2:], uq_rope[:, :, :MLA_ROPE // 2]], axis=2)
        wuq = jnp.concatenate(
            [uq_nope, uq_rope.reshape(Q_LORA, -1), uq_rot.reshape(Q_LORA, -1)], axis=1).astype(BF16)
        ukv = w_ukv[i].reshape(KV_LORA, HEADS, MLA_NOPE + MLA_DV)
        wuk = ukv[:, :, :MLA_NOPE].reshape(KV_LORA, -1).astype(BF16)
        wuvt = ukv[:, :, MLA_NOPE:].reshape(KV_LORA, -1).T.astype(BF16)

        rope_ret = _rope_consts(RET_DK // 2)
        rope_mla = _rope_consts(MLA_ROPE // 2)

        h, qcat, kcat, vt = _ffn_ln_mla(
            h, pos_b, rope_mla, ffn1_w_in[i].astype(BF16), ffn1_w_out[i].astype(BF16),
            row(ln_g[i, 0]), row(ln_b[i, 0]),
            w_mla, row(q_norm_g[i]), row(kv_norm_g[i]), wuq, wuk, wuvt)

        dmask, xi, zeta, cdec = _retention_tables()
        y_ret = _retention(h, pos_b, rope_ret, xi, zeta, dmask, cdec, w_ret, s)
        o_mla = _mla_attn(qcat.reshape(b, s, -1), kcat.reshape(b, s, -1),
                          vt.reshape(b, s // ATT_K, HEADS * MLA_DVA, ATT_K))
        w_ro = (ret_gn_g[i][:, None] * w_ret_o[i]).astype(BF16)
        h = _merge_ln(y_ret, o_mla.reshape(t, -1), h, w_gates,
                      w_ro, w_mla_o[i].astype(BF16), w_out[i].astype(BF16),
                      row(ln_g[i, 1]), row(ln_b[i, 1]))

        h = _ffn_ple_ln(h, p[i].reshape(t, -1), ffn2_w_in[i].astype(BF16),
                        ffn2_w_out[i].astype(BF16), ple_w_gate[i].astype(BF16),
                        ple_w_proj[i].astype(BF16), row(ln_g[i, 2]), row(ln_b[i, 2]),
                        row(ln_g[i, 3]), row(ln_b[i, 3]))
    return h.reshape(b, s, d)
```

```python
import functools
import math

import jax
import jax.numpy as jnp
from jax import lax
from jax.experimental import pallas as pl
from jax.experimental.pallas import tpu as pltpu

F32 = jnp.float32
BF16 = jnp.bfloat16

D_MODEL = 1024
CHUNK = 64
D_PLE = 256
D_FF = 2816
HEADS = 8
RET_DK = 128
RET_DV = 256
RET_QK = HEADS * RET_DK
RET_V = HEADS * RET_DV
MLA_NOPE = 128
MLA_ROPE = 64
MLA_DV = 128
Q_LORA = 256
KV_LORA = 256
ROPE_BASE = 10000.0
EPS = 1e-5
ALPHA = 2.0 ** 0.25

LANES = 128
FF_CHUNK = 256
TOKEN_TILE = 512
FFN_TILE = 1024
FFN_CHAIN = 512
RET_BLOCK = 256
ATT_Q = 512
ATT_K = 512
ATT_HEADS = 2
MLA_QK = 256
MLA_DVA = MLA_DV + 16
VMEM_LIMIT = 52 * 1024 * 1024
NEG_BIG = -1e30


def _const_spec(shape):
    nd = len(shape)
    return pl.BlockSpec(shape, lambda *_: (0,) * nd, pipeline_mode=pl.Buffered(1))


def _row_spec(tile, width):
    return pl.BlockSpec((tile, width), lambda i: (i, 0))


def _params(*sem):
    return pltpu.CompilerParams(dimension_semantics=sem, vmem_limit_bytes=VMEM_LIMIT)


def _dot(a, b):
    return jnp.dot(a, b, preferred_element_type=F32)


def _dot_nt(a, bt):
    return lax.dot_general(a, bt, (((1,), (1,)), ((), ())), preferred_element_type=F32)


def _layer_norm(y, g, b):
    mu = jnp.mean(y, axis=-1, keepdims=True)
    d = y - mu
    var = jnp.mean(d * d, axis=-1, keepdims=True)
    return d * lax.rsqrt(var + EPS) * g + b


def _rms_norm(x, g):
    return x * lax.rsqrt(jnp.mean(x * x, axis=-1, keepdims=True) + EPS) * g


def _swiglu(xb, w_in_ref, w_out_ref, act_ref):
    for c in range(D_FF // FF_CHUNK):
        lo = c * FF_CHUNK
        g = _dot(xb, w_in_ref[:, lo:lo + FF_CHUNK])
        u = _dot(xb, w_in_ref[:, D_FF + lo:D_FF + lo + FF_CHUNK])
        act_ref[:, lo:lo + FF_CHUNK] = (g * jax.nn.sigmoid(g) * u).astype(BF16)
    return _dot(act_ref[...], w_out_ref[...])


def _row_chains(tile):
    return [slice(r, r + FFN_CHAIN) for r in range(0, tile, FFN_CHAIN)]


def _ffn_ln_mla_kernel(x_ref, pos_ref, invf_ref, dcos_ref, dsin_ref, w_in_ref, w_out_ref,
                       g_ref, b_ref, wlat_ref, qg_ref, kvg_ref, wuq_ref, wuk_ref, wuvt_ref,
                       o_ref, qcat_ref, kcat_ref, vt_ref, act_ref, cos_ref, sin_ref):
    rope_ok = _rope_tables(pos_ref, invf_ref, dcos_ref, dsin_ref, cos_ref, sin_ref)
    _rope_tables_direct(rope_ok, pos_ref, invf_ref, cos_ref, sin_ref)
    x = x_ref[...]
    f = _swiglu(x.astype(BF16), w_in_ref, w_out_ref, act_ref)
    h = _layer_norm(ALPHA * x + 0.5 * f, g_ref[...], b_ref[...])
    o_ref[...] = h
    _mla_proj(h.astype(BF16), wlat_ref, qg_ref, kvg_ref, wuq_ref, wuk_ref, wuvt_ref,
              qcat_ref, kcat_ref, vt_ref, cos_ref, sin_ref)


class _CastRiders:
    def __init__(self, steps, riders):
        self.arrays = [a for a, _, _ in riders]
        self.in_specs, self.out_specs, self.out_shapes = [], [], []
        for a, first, count in riders:
            rows, cols = count // steps, a.shape[1]
            assert rows * steps == count and rows % 16 == 0 and first % rows == 0
            self.in_specs.append(
                pl.BlockSpec((rows, cols), lambda i, o=first // rows: (i + o, 0)))
            self.out_specs.append(pl.BlockSpec((rows, cols), lambda i: (i, 0)))
            self.out_shapes.append(jax.ShapeDtypeStruct((count, cols), BF16))

    def wrap(self, body, n_in, n_out):
        n = len(self.arrays)

        def kern(*refs):
            ins, rin = refs[:n_in], refs[n_in:n_in + n]
            outs = refs[n_in + n:n_in + n + n_out]
            rout = refs[n_in + n + n_out:n_in + 2 * n + n_out]
            for src, dst in zip(rin, rout):
                dst[...] = src[...].astype(BF16)
            body(*ins, *outs, *refs[n_in + 2 * n + n_out:])
        return kern


def _ffn_ln_mla(x, pos_b, rope, w_in, w_out, g, b, wlat, qg, kvg, wuq, wuk, wuvt, riders):
    t = x.shape[0]
    uq_cols = HEADS * (MLA_NOPE + 2 * MLA_ROPE)
    assert TOKEN_TILE == ATT_K
    cast = _CastRiders(t // TOKEN_TILE, riders)
    return pl.pallas_call(
        cast.wrap(_ffn_ln_mla_kernel, 15, 4),
        name="ffn_ln_mla",
        grid=(t // TOKEN_TILE,),
        in_specs=[
            _row_spec(TOKEN_TILE, D_MODEL),
            _row_spec(TOKEN_TILE, LANES),
            _const_spec((1, LANES)),
            _const_spec((TOKEN_TILE, LANES)),
            _const_spec((TOKEN_TILE, LANES)),
            _const_spec((D_MODEL, 2 * D_FF)),
            _const_spec((D_FF, D_MODEL)),
            _const_spec((1, D_MODEL)),
            _const_spec((1, D_MODEL)),
            _const_spec((MLA_LAT, D_MODEL)),
            _const_spec((1, Q_LORA)),
            _const_spec((1, KV_LORA)),
            _const_spec((Q_LORA, uq_cols)),
            _const_spec((KV_LORA, HEADS * MLA_NOPE)),
            _const_spec((HEADS * MLA_DV, KV_LORA)),
        ] + cast.in_specs,
        out_specs=[
            _row_spec(TOKEN_TILE, D_MODEL),
            _row_spec(TOKEN_TILE, HEADS * MLA_QK),
            _row_spec(TOKEN_TILE, HEADS * MLA_QK),
            pl.BlockSpec((1, HEADS * MLA_DVA, TOKEN_TILE), lambda i: (i, 0, 0)),
        ] + cast.out_specs,
        out_shape=[
            jax.ShapeDtypeStruct((t, D_MODEL), F32),
            jax.ShapeDtypeStruct((t, HEADS * MLA_QK), BF16),
            jax.ShapeDtypeStruct((t, HEADS * MLA_QK), BF16),
            jax.ShapeDtypeStruct((t // TOKEN_TILE, HEADS * MLA_DVA, TOKEN_TILE), BF16),
        ] + cast.out_shapes,
        scratch_shapes=[pltpu.VMEM((TOKEN_TILE, D_FF), BF16)]
        + [pltpu.VMEM((TOKEN_TILE, LANES), F32)] * 2,
        compiler_params=_params("parallel"),
    )(x, pos_b, *rope, w_in, w_out, g, b, wlat, qg, kvg, wuq, wuk, wuvt, *cast.arrays)


def _ffn_ple_ln_kernel(h_ref, p_ref, w_in_ref, w_out_ref, wg_ref, wp_ref,
                       g2_ref, b2_ref, g3_ref, b3_ref, o_ref, act_ref):
    chains = _row_chains(h_ref.shape[0])
    h2 = []
    for c, rows in enumerate(chains):
        h = h_ref[rows, :]
        f = _swiglu(h.astype(BF16), w_in_ref, w_out_ref, act_ref.at[c])
        h2.append(_layer_norm(ALPHA * h + 0.5 * f, g2_ref[...], b2_ref[...]))
    for c, rows in enumerate(chains):
        gate = jax.nn.sigmoid(_dot(h2[c].astype(BF16), wg_ref[...]))
        proj = _dot(p_ref[rows, :].astype(BF16), wp_ref[...])
        o_ref[rows, :] = _layer_norm(ALPHA * h2[c] + gate * proj, g3_ref[...], b3_ref[...])


def _ffn_ple_ln(h, p, w_in, w_out, wg, wp, g2, b2, g3, b3):
    t = h.shape[0]
    return pl.pallas_call(
        _ffn_ple_ln_kernel,
        name="ffn_ple_ln",
        grid=(t // FFN_TILE,),
        in_specs=[
            _row_spec(FFN_TILE, D_MODEL),
            _row_spec(FFN_TILE, D_PLE),
            _const_spec((D_MODEL, 2 * D_FF)),
            _const_spec((D_FF, D_MODEL)),
            _const_spec((D_MODEL, D_MODEL)),
            _const_spec((D_PLE, D_MODEL)),
            _const_spec((1, D_MODEL)),
            _const_spec((1, D_MODEL)),
            _const_spec((1, D_MODEL)),
            _const_spec((1, D_MODEL)),
        ],
        out_specs=_row_spec(FFN_TILE, D_MODEL),
        out_shape=jax.ShapeDtypeStruct((t, D_MODEL), F32),
        scratch_shapes=[pltpu.VMEM((FFN_TILE // FFN_CHAIN, FFN_CHAIN, D_FF), BF16)],
        compiler_params=_params("parallel"),
    )(h, p, w_in, w_out, wg, wp, g2, b2, g3, b3)


def _rope_tables(pos_ref, invf_ref, dcos_ref, dsin_ref, cos_ref, sin_ref):
    pos = pos_ref[...]
    invf = invf_ref[...]
    base = pos[0:1, :]
    row = lax.broadcasted_iota(jnp.int32, pos.shape, 0).astype(F32)
    exact_f32_int = 2.0 ** 24 - pos.shape[0]
    consecutive = jnp.logical_and(jnp.all(pos == base + row),
                                  jnp.all(jnp.abs(base) < exact_f32_int))
    a = base * invf
    ca = jnp.cos(a)
    sa = jnp.sin(a)
    cos_ref[...] = ca * dcos_ref[...] - sa * dsin_ref[...]
    sin_ref[...] = sa * dcos_ref[...] + ca * dsin_ref[...]
    return consecutive


def _rope_tables_direct(valid, pos_ref, invf_ref, cos_ref, sin_ref):
    @pl.when(jnp.logical_not(valid))
    def _():
        ang = pos_ref[...] * invf_ref[...]
        cos_ref[...] = jnp.cos(ang)
        sin_ref[...] = jnp.sin(ang)


def _retention_kernel(h_ref, pos_ref, invf_ref, dcos_ref, dsin_ref, xi_ref, zeta_ref, dmask_ref,
                      cdec_ref, w_ref, o_ref, rq_ref, rqx_ref, rk_ref, rkz_ref, rv_ref, sg_ref,
                      state_ref, cos_ref, sin_ref, *, tiles_per_seq):
    @pl.when(pl.program_id(0) % tiles_per_seq == 0)
    def _():
        state_ref[...] = jnp.zeros_like(state_ref)

    rope_ok = _rope_tables(pos_ref, invf_ref, dcos_ref, dsin_ref, cos_ref, sin_ref)
    _rope_tables_direct(rope_ok, pos_ref, invf_ref, cos_ref, sin_ref)
    hb = h_ref[...].astype(BF16)
    cos = cos_ref[...]
    lane = lax.broadcasted_iota(jnp.int32, (1, LANES), 1)
    sin = sin_ref[...] * jnp.where(lane < RET_DK // 2, -1.0, 1.0)
    k_scale = RET_DK ** -0.5
    g = _dot_nt(hb, w_ref[2 * RET_QK + RET_V:2 * RET_QK + 2 * RET_V, :])
    sg_ref[...] = (g * jax.nn.sigmoid(g)).astype(BF16)
    q = _dot_nt(hb, w_ref[0:RET_QK, :])
    k = _dot_nt(hb, w_ref[RET_QK:2 * RET_QK, :])
    for h in range(HEADS):
        sl = slice(h * RET_DK, (h + 1) * RET_DK)
        tq = q[:, sl]
        tq = tq * cos + pltpu.roll(tq, RET_DK // 2, 1) * sin
        rq_ref[:, sl] = tq.astype(BF16)
        rqx_ref[:, sl] = (tq * xi_ref[:, sl]).astype(BF16)
        tk = k[:, sl]
        tk = (tk * cos + pltpu.roll(tk, RET_DK // 2, 1) * sin) * k_scale
        rk_ref[:, sl] = tk.astype(BF16)
        rkz_ref[:, sl] = (tk * zeta_ref[:, sl]).astype(BF16)
    rv_ref[...] = _dot_nt(hb, w_ref[2 * RET_QK:2 * RET_QK + RET_V, :]).astype(BF16)

    def qk(rows, h):
        qs = slice(h * RET_DK, (h + 1) * RET_DK)
        return lax.dot_general(rq_ref[rows, qs], rk_ref[rows, qs], (((1,), (1,)), ((), ())),
                               preferred_element_type=F32)

    for blk in range(TOKEN_TILE // RET_BLOCK):
        rows = slice(blk * RET_BLOCK, (blk + 1) * RET_BLOCK)
        s_next = qk(rows, 0)
        for h in range(HEADS):
            qs = slice(h * RET_DK, (h + 1) * RET_DK)
            vs = slice(h * RET_DV, (h + 1) * RET_DV)
            v = rv_ref[rows, vs]
            state = state_ref[h]
            s = s_next
            if h + 1 < HEADS:
                s_next = qk(rows, h + 1)
            cross = _dot(rqx_ref[rows, qs], state.astype(BF16))
            upd = lax.dot_general(rkz_ref[rows, qs], v, (((0,), (0,)), ((), ())),
                                  preferred_element_type=F32)
            state_ref[h] = state * cdec_ref[h] + upd
            y = _dot((s * dmask_ref[h]).astype(BF16), v) + cross
            mu = jnp.mean(y, axis=-1, keepdims=True)
            d = y - mu
            var = jnp.mean(d * d, axis=-1, keepdims=True)
            o_ref[rows, vs] = (d * lax.rsqrt(var + EPS) * sg_ref[rows, vs].astype(F32)).astype(BF16)


def _retention(h, pos_b, rope, xi, zeta, dmask, cdec, w, seq_len):
    t = h.shape[0]
    return pl.pallas_call(
        functools.partial(_retention_kernel, tiles_per_seq=seq_len // TOKEN_TILE),
        name="retention",
        grid=(t // TOKEN_TILE,),
        in_specs=[
            _row_spec(TOKEN_TILE, D_MODEL),
            _row_spec(TOKEN_TILE, LANES),
            _const_spec((1, LANES)),
            _const_spec((TOKEN_TILE, LANES)),
            _const_spec((TOKEN_TILE, LANES)),
            _const_spec((TOKEN_TILE, RET_QK)),
            _const_spec((TOKEN_TILE, RET_QK)),
            _const_spec((HEADS, RET_BLOCK, RET_BLOCK)),
            _const_spec((HEADS, 1, RET_DV)),
            _const_spec((2 * RET_QK + 2 * RET_V, D_MODEL)),
        ],
        out_specs=_row_spec(TOKEN_TILE, RET_V),
        out_shape=jax.ShapeDtypeStruct((t, RET_V), BF16),
        scratch_shapes=[pltpu.VMEM((TOKEN_TILE, RET_QK), BF16)] * 4
        + [pltpu.VMEM((TOKEN_TILE, RET_V), BF16)] * 2
        + [pltpu.VMEM((HEADS, RET_DK, RET_DV), F32)]
        + [pltpu.VMEM((TOKEN_TILE, LANES), F32)] * 2,
        compiler_params=_params("arbitrary"),
    )(h, pos_b, *rope, xi, zeta, dmask, cdec, w)


MLA_LAT = Q_LORA + KV_LORA + LANES


def _mla_proj(hb, w_ref, qg_ref, kvg_ref, wuq_ref, wuk_ref, wuvt_ref,
              qcat_ref, kcat_ref, vt_ref, cos_ref, sin_ref):
    lane = lax.broadcasted_iota(jnp.int32, (1, LANES), 1)
    first_half = lane < MLA_ROPE

    lat = _dot_nt(hb, w_ref[...])
    cq = lat[:, 0:Q_LORA]
    ckv = lat[:, Q_LORA:Q_LORA + KV_LORA]
    kpe = lat[:, Q_LORA + KV_LORA:MLA_LAT]
    kpe = kpe + pltpu.roll(kpe, MLA_ROPE, 1)
    kpe_rot = pltpu.roll(kpe, MLA_ROPE // 2, 1) * jnp.where(
        lane % MLA_ROPE < MLA_ROPE // 2, -1.0, 1.0)

    qf = _dot(_rms_norm(cq, qg_ref[...]).astype(BF16), wuq_ref[...])
    nope_w = HEADS * MLA_NOPE
    rope_w = HEADS * MLA_ROPE
    for h in range(HEADS):
        qcat_ref[:, h * MLA_QK:h * MLA_QK + MLA_NOPE] = (
            qf[:, h * MLA_NOPE:(h + 1) * MLA_NOPE].astype(BF16))

    ckvn = _rms_norm(ckv, kvg_ref[...])
    kv = _dot(ckvn.astype(BF16), wuk_ref[...])
    vt = _dot(wuvt_ref[...], ckvn.T.astype(BF16)).astype(BF16)
    ones = jnp.ones((MLA_DVA - MLA_DV, vt.shape[1]), BF16)
    for h in range(HEADS):
        vt_ref[0, h * MLA_DVA:h * MLA_DVA + MLA_DV, :] = vt[h * MLA_DV:(h + 1) * MLA_DV, :]
        vt_ref[0, h * MLA_DVA + MLA_DV:(h + 1) * MLA_DVA, :] = ones
        kcat_ref[:, h * MLA_QK:h * MLA_QK + MLA_NOPE] = (
            kv[:, h * MLA_NOPE:(h + 1) * MLA_NOPE].astype(BF16))

    cos = cos_ref[...]
    sin = sin_ref[...]
    for j in range(HEADS // 2):
        r = (qf[:, nope_w + j * LANES:nope_w + (j + 1) * LANES] * cos
             + qf[:, nope_w + rope_w + j * LANES:nope_w + rope_w + (j + 1) * LANES] * sin)
        base = 2 * j * MLA_QK
        qcat_ref[:, base + MLA_NOPE:base + MLA_QK] = jnp.where(first_half, r, 0.0).astype(BF16)
        base += MLA_QK
        qcat_ref[:, base + MLA_NOPE:base + MLA_QK] = jnp.where(first_half, 0.0, r).astype(BF16)
    kr = kpe * cos + kpe_rot * sin
    kr_even = jnp.where(first_half, kr, 0.0).astype(BF16)
    kr_odd = jnp.where(first_half, 0.0, kr).astype(BF16)
    for h in range(HEADS):
        kcat_ref[:, h * MLA_QK + MLA_NOPE:(h + 1) * MLA_QK] = kr_even if h % 2 == 0 else kr_odd


def _softmax_step(st, mx, vt, stats):
    m, acc = stats
    m_new = jnp.maximum(m, mx)
    a = jnp.exp2(m - m_new)
    p = jnp.exp2(st - m_new)
    acc = a * acc + _dot(vt, p.astype(BF16))
    return m_new, acc


def _mla_attn_kernel(q_ref, qn_ref, k_ref, vt_ref, o_ref, s0_ref, s1_ref, mx0_ref, mx1_ref,
                     m_ref, acc_ref):
    qi = pl.program_id(2)
    heads = range(ATT_HEADS)

    def scores(qr, j, buf):
        s_ref, mx_ref = buf
        off = pl.multiple_of(j * ATT_K, ATT_K)
        for hh in heads:
            st = lax.dot_general(
                k_ref[0, pl.ds(off, ATT_K), hh * MLA_QK:(hh + 1) * MLA_QK],
                qr[0, :, hh * MLA_QK:(hh + 1) * MLA_QK],
                (((1,), (1,)), ((), ())), preferred_element_type=F32)
            s_ref[hh] = st
            mx_ref[hh] = jnp.max(st, axis=0, keepdims=True)

    def consume(j, buf, mask=None):
        s_ref, mx_ref = buf
        for hh in heads:
            st = s_ref[hh]
            if mask is None:
                mx = mx_ref[hh]
            else:
                st = jnp.where(mask, st, -jnp.inf)
                mx = jnp.max(st, axis=0, keepdims=True)
            m, acc = _softmax_step(st, mx, vt_ref[0, j, hh * MLA_DVA:(hh + 1) * MLA_DVA, :],
                                   (m_ref[hh], acc_ref[hh]))
            m_ref[hh] = m
            acc_ref[hh] = acc

    m_ref[...] = jnp.full_like(m_ref, NEG_BIG)
    acc_ref[...] = jnp.zeros_like(acc_ref)

    bufs = ((s0_ref, mx0_ref), (s1_ref, mx1_ref))

    @pl.when(qi == 0)
    def _():
        scores(q_ref, 0, bufs[0])

    key_chunk = lax.broadcasted_iota(jnp.int32, (ATT_K, ATT_Q), 0) // CHUNK
    qry_chunk = lax.broadcasted_iota(jnp.int32, (ATT_K, ATT_Q), 1) // CHUNK
    mask = key_chunk <= qry_chunk

    def step(first, odd):
        cur, oth = bufs[first], bufs[1 - first]

        def pair(jj, c):
            j = 2 * jj
            scores(q_ref, j + 1, oth)
            consume(j, cur)
            scores(q_ref, j + 2, cur)
            consume(j + 1, oth)
            return c

        lax.fori_loop(0, qi // 2, pair, 0)
        if odd:
            scores(q_ref, qi, oth)
            consume(qi - 1, cur)
            scores(qn_ref, 0, cur)
            consume(qi, oth, mask)
        else:
            scores(qn_ref, 0, oth)
            consume(qi, cur, mask)

    for c in range(4):
        pl.when(qi % 4 == c)(functools.partial(step, ((c + 1) // 2) % 2, c % 2 == 1))

    for hh in heads:
        l = acc_ref[hh, MLA_DV:MLA_DV + 1, :]
        o_ref[0, :, hh * MLA_DV:(hh + 1) * MLA_DV] = (
            acc_ref[hh, 0:MLA_DV, :] / l).T.astype(BF16)


def _mla_attn(qcat, kcat, vt):
    b, s, _ = qcat.shape
    assert ATT_Q == ATT_K
    n_kb = s // ATT_K
    last_q = s // ATT_Q - 1
    return pl.pallas_call(
        _mla_attn_kernel,
        name="mla_attn",
        grid=(b, HEADS // ATT_HEADS, s // ATT_Q),
        in_specs=[
            pl.BlockSpec((1, ATT_Q, ATT_HEADS * MLA_QK), lambda i, h, j: (i, j, h)),
            pl.BlockSpec((1, ATT_Q, ATT_HEADS * MLA_QK),
                         lambda i, h, j: (i, jnp.minimum(j + 1, last_q), h)),
            pl.BlockSpec((1, s, ATT_HEADS * MLA_QK), lambda i, h, j: (i, 0, h)),
            pl.BlockSpec((1, n_kb, ATT_HEADS * MLA_DVA, ATT_K), lambda i, h, j: (i, 0, h, 0)),
        ],
        out_specs=pl.BlockSpec((1, ATT_Q, ATT_HEADS * MLA_DV), lambda i, h, j: (i, j, h)),
        out_shape=jax.ShapeDtypeStruct((b, s, HEADS * MLA_DV), BF16),
        scratch_shapes=[
            pltpu.VMEM((ATT_HEADS, ATT_K, ATT_Q), F32),
            pltpu.VMEM((ATT_HEADS, ATT_K, ATT_Q), F32),
            pltpu.VMEM((ATT_HEADS, 1, ATT_Q), F32),
            pltpu.VMEM((ATT_HEADS, 1, ATT_Q), F32),
            pltpu.VMEM((ATT_HEADS, 1, ATT_Q), F32),
            pltpu.VMEM((ATT_HEADS, MLA_DVA, ATT_Q), F32),
        ],
        compiler_params=_params("parallel", "parallel", "arbitrary"),
    )(qcat, qcat, kcat, vt)


def _merge_ln_kernel(yr_ref, om_ref, h_ref, wg_ref, wro_ref, wmo_ref, wout_ref,
                     g_ref, b_ref, o_ref):
    for rows in _row_chains(h_ref.shape[0]):
        h = h_ref[rows, :]
        hb = h.astype(BF16)
        mix = (jax.nn.sigmoid(_dot_nt(hb, wg_ref[0:D_MODEL, :]))
               * _dot(yr_ref[rows, :], wro_ref[...])
               + jax.nn.sigmoid(_dot_nt(hb, wg_ref[D_MODEL:2 * D_MODEL, :]))
               * _dot(om_ref[rows, :], wmo_ref[...]))
        mixed = _dot(mix.astype(BF16), wout_ref[...])
        o_ref[rows, :] = _layer_norm(ALPHA * h + mixed, g_ref[...], b_ref[...])


def _merge_ln(yr, om, h, wg, wro, wmo, wout, g, b, riders):
    t = h.shape[0]
    cast = _CastRiders(t // FFN_TILE, riders)
    return pl.pallas_call(
        cast.wrap(_merge_ln_kernel, 9, 1),
        name="merge_ln",
        grid=(t // FFN_TILE,),
        in_specs=[
            _row_spec(FFN_TILE, RET_V),
            _row_spec(FFN_TILE, HEADS * MLA_DV),
            _row_spec(FFN_TILE, D_MODEL),
            _const_spec((2 * D_MODEL, D_MODEL)),
            _const_spec((RET_V, D_MODEL)),
            _const_spec((HEADS * MLA_DV, D_MODEL)),
            _const_spec((D_MODEL, D_MODEL)),
            _const_spec((1, D_MODEL)),
            _const_spec((1, D_MODEL)),
        ] + cast.in_specs,
        out_specs=[_row_spec(FFN_TILE, D_MODEL)] + cast.out_specs,
        out_shape=[jax.ShapeDtypeStruct((t, D_MODEL), F32)] + cast.out_shapes,
        compiler_params=_params("parallel"),
    )(yr, om, h, wg, wro, wmo, wout, g, b, *cast.arrays)


def _retention_tables():
    log_gamma = jnp.log(1.0 - 2.0 ** (-5.0 - jnp.arange(HEADS, dtype=F32)))
    idx = jnp.arange(RET_BLOCK, dtype=F32)
    chunk = jnp.arange(RET_BLOCK) // CHUNK
    visible = chunk[None, :] <= chunk[:, None]
    dist = jnp.abs(idx[:, None] - idx[None, :])
    dmask = jnp.where(visible[None], jnp.exp(log_gamma[:, None, None] * dist[None]), 0.0)
    xi = jnp.exp(log_gamma[:, None] * (idx + 1.0))
    zeta = jnp.exp(log_gamma[:, None] * (RET_BLOCK - 1.0 - idx))
    cdec = jnp.exp(log_gamma * RET_BLOCK)
    def per_token_rows(tab):
        rows = jnp.broadcast_to(tab.T[:, :, None], (RET_BLOCK, HEADS, RET_DK))
        return jnp.tile(rows.reshape(RET_BLOCK, RET_QK), (TOKEN_TILE // RET_BLOCK, 1))

    return (dmask, per_token_rows(xi), per_token_rows(zeta),
            jnp.broadcast_to(cdec[:, None, None], (HEADS, 1, RET_DV)))


def _rope_consts(half):
    invf = ROPE_BASE ** (-jnp.arange(half, dtype=F32) / half)
    invf = jnp.tile(invf, LANES // half).reshape(1, LANES)
    delta = jnp.arange(TOKEN_TILE, dtype=F32)[:, None] * invf
    return invf, jnp.cos(delta), jnp.sin(delta)


def kernel(x, p, positions, ln_g, ln_b, ffn1_w_in, ffn1_w_out, w_in, ret_gn_g, w_ret_o,
           q_norm_g, kv_norm_g, w_uq, w_ukv, w_mla_o, w_out, ffn2_w_in, ffn2_w_out,
           ple_w_gate, ple_w_proj):
    b, s, d = x.shape
    t = b * s
    h = x.reshape(t, d)
    pos_b = jnp.broadcast_to(positions.astype(F32).reshape(t, 1), (t, LANES))
    row = lambda v: v.reshape(1, -1)

    for i in range(ln_g.shape[0]):
        w_in_t = jnp.swapaxes(w_in[i], 0, 1)
        c0 = 2 * RET_QK + 2 * RET_V
        c1 = c0 + Q_LORA + KV_LORA + MLA_ROPE
        w_mla = jnp.pad(w_in_t[c0:c1], ((0, LANES - MLA_ROPE), (0, 0))).astype(BF16)

        q_scale = (MLA_NOPE + MLA_ROPE) ** -0.5 * math.log2(math.e)
        uq = w_uq[i].reshape(Q_LORA, HEADS, MLA_NOPE + MLA_ROPE) * q_scale
        uq_nope = uq[:, :, :MLA_NOPE].reshape(Q_LORA, HEADS * MLA_NOPE)
        uq_rope = uq[:, :, MLA_NOPE:]
        uq_rot = jnp.concatenate(
            [-uq_rope[:, :, MLA_ROPE // 2:], uq_rope[:, :, :MLA_ROPE // 2]], axis=2)
        wuq = jnp.concatenate(
            [uq_nope, uq_rope.reshape(Q_LORA, -1), uq_rot.reshape(Q_LORA, -1)], axis=1).astype(BF16)
        ukv = w_ukv[i].reshape(KV_LORA, HEADS, MLA_NOPE + MLA_DV)
        wuk = ukv[:, :, :MLA_NOPE].reshape(KV_LORA, -1).astype(BF16)
        wuvt = ukv[:, :, MLA_NOPE:].reshape(KV_LORA, -1).T.astype(BF16)

        rope_ret = _rope_consts(RET_DK // 2)
        rope_mla = _rope_consts(MLA_ROPE // 2)

        (h, qcat, kcat, vt,
         w_ret, w_gates, w_ffn2_in, w_mo, w_o, w_pg) = _ffn_ln_mla(
            h, pos_b, rope_mla, ffn1_w_in[i].astype(BF16), ffn1_w_out[i].astype(BF16),
            row(ln_g[i, 0]), row(ln_b[i, 0]),
            w_mla, row(q_norm_g[i]), row(kv_norm_g[i]), wuq, wuk, wuvt,
            riders=[(w_in_t, 0, c0), (w_in_t, c1, 2 * D_MODEL), (ffn2_w_in[i], 0, D_MODEL),
                    (w_mla_o[i], 0, HEADS * MLA_DV), (w_out[i], 0, D_MODEL),
                    (ple_w_gate[i], 0, D_MODEL)])

        dmask, xi, zeta, cdec = _retention_tables()
        y_ret = _retention(h, pos_b, rope_ret, xi, zeta, dmask, cdec, w_ret, s)
        o_mla = _mla_attn(qcat.reshape(b, s, -1), kcat.reshape(b, s, -1),
                          vt.reshape(b, s // ATT_K, HEADS * MLA_DVA, ATT_K))
        w_ro = (ret_gn_g[i][:, None] * w_ret_o[i]).astype(BF16)
        h, w_ffn2_out, w_pp = _merge_ln(
            y_ret, o_mla.reshape(t, -1), h, w_gates, w_ro, w_mo, w_o,
            row(ln_g[i, 1]), row(ln_b[i, 1]),
            riders=[(ffn2_w_out[i], 0, D_FF), (ple_w_proj[i], 0, D_PLE)])

        h = _ffn_ple_ln(h, p[i].reshape(t, -1), w_ffn2_in, w_ffn2_out, w_pg, w_pp,
                        row(ln_g[i, 2]), row(ln_b[i, 2]), row(ln_g[i, 3]), row(ln_b[i, 3]))
    return h.reshape(b, s, d)
```

```python
import functools
import math

import jax
import jax.numpy as jnp
from jax import lax
from jax.experimental import pallas as pl
from jax.experimental.pallas import tpu as pltpu

F32 = jnp.float32
BF16 = jnp.bfloat16

D_MODEL = 1024
CHUNK = 64
D_PLE = 256
D_FF = 2816
HEADS = 8
RET_DK = 128
RET_DV = 256
RET_QK = HEADS * RET_DK
RET_V = HEADS * RET_DV
MLA_NOPE = 128
MLA_ROPE = 64
MLA_DV = 128
Q_LORA = 256
KV_LORA = 256
ROPE_BASE = 10000.0
EPS = 1e-5
ALPHA = 2.0 ** 0.25

LANES = 128
FF_CHUNK = 256
TOKEN_TILE = 512
FFN_TILE = 1024
FFN_CHAIN = 512
RET_BLOCK = 256
ATT_Q = 512
ATT_K = 512
ATT_HEADS = 2
MLA_QK = 256
BF16_ROWS = 16
MLA_DVA = MLA_DV + BF16_ROWS
VMEM_LIMIT = 52 * 1024 * 1024
NEG_BIG = -1e30


def _const_spec(shape):
    nd = len(shape)
    return pl.BlockSpec(shape, lambda *_: (0,) * nd, pipeline_mode=pl.Buffered(1))


def _row_spec(tile, width):
    return pl.BlockSpec((tile, width), lambda i: (i, 0))


def _params(*sem):
    return pltpu.CompilerParams(dimension_semantics=sem, vmem_limit_bytes=VMEM_LIMIT)


def _dot(a, b):
    return jnp.dot(a, b, preferred_element_type=F32)


def _dot_nt(a, bt):
    return lax.dot_general(a, bt, (((1,), (1,)), ((), ())), preferred_element_type=F32)


def _layer_norm(y, g, b):
    mu = jnp.mean(y, axis=-1, keepdims=True)
    d = y - mu
    var = jnp.mean(d * d, axis=-1, keepdims=True)
    return d * lax.rsqrt(var + EPS) * g + b


def _rms_norm(x, g):
    return x * lax.rsqrt(jnp.mean(x * x, axis=-1, keepdims=True) + EPS) * g


def _swiglu(xb, w_in_ref, w_out_ref, act_ref):
    for c in range(D_FF // FF_CHUNK):
        lo = c * FF_CHUNK
        g = _dot(xb, w_in_ref[:, lo:lo + FF_CHUNK])
        u = _dot(xb, w_in_ref[:, D_FF + lo:D_FF + lo + FF_CHUNK])
        act_ref[:, lo:lo + FF_CHUNK] = (g * jax.nn.sigmoid(g) * u).astype(BF16)
    return _dot(act_ref[...], w_out_ref[...])


def _row_chains(tile):
    return [slice(r, r + FFN_CHAIN) for r in range(0, tile, FFN_CHAIN)]


def _ffn_ln_mla_kernel(x_ref, pos_ref, invf_ref, dcos_ref, dsin_ref, w_in_ref, w_out_ref,
                       g_ref, b_ref, wlat_ref, qg_ref, kvg_ref, wuq_ref, wuk_ref, wuvt_ref,
                       o_ref, qcat_ref, kcat_ref, vt_ref, act_ref, cos_ref, sin_ref):
    rope_ok = _rope_tables(pos_ref, invf_ref, dcos_ref, dsin_ref, cos_ref, sin_ref)
    _rope_tables_direct(rope_ok, pos_ref, invf_ref, cos_ref, sin_ref)
    x = x_ref[...]
    f = _swiglu(x.astype(BF16), w_in_ref, w_out_ref, act_ref)
    h = _layer_norm(ALPHA * x + 0.5 * f, g_ref[...], b_ref[...])
    o_ref[...] = h
    _mla_proj(h.astype(BF16), wlat_ref, qg_ref, kvg_ref, wuq_ref, wuk_ref, wuvt_ref,
              qcat_ref, kcat_ref, vt_ref, cos_ref, sin_ref)


class _CastRiders:
    def __init__(self, steps, riders):
        self.arrays, self.scaled = [], []
        self.in_specs, self.out_specs, self.out_shapes = [], [], []
        for a, first, count, scale in riders:
            rows, cols = count // steps, a.shape[1]
            assert rows * steps == count and rows % BF16_ROWS == 0 and first % rows == 0
            self.arrays.append(a)
            self.in_specs.append(
                pl.BlockSpec((rows, cols), lambda i, o=first // rows: (i + o, 0)))
            self.scaled.append(scale is not None)
            if scale is not None:
                self.arrays.append(scale.reshape(count, 1))
                self.in_specs.append(pl.BlockSpec((rows, 1), lambda i: (i, 0)))
            self.out_specs.append(pl.BlockSpec((rows, cols), lambda i: (i, 0)))
            self.out_shapes.append(jax.ShapeDtypeStruct((count, cols), BF16))

    def wrap(self, body, n_in, n_out):
        n_src, n_dst = len(self.arrays), len(self.scaled)

        def kern(*refs):
            ins, rin = refs[:n_in], list(refs[n_in:n_in + n_src])
            outs = refs[n_in + n_src:n_in + n_src + n_out]
            rout = refs[n_in + n_src + n_out:n_in + n_src + n_out + n_dst]
            for dst, scaled in zip(rout, self.scaled):
                src = rin.pop(0)[...]
                if scaled:
                    src = src * rin.pop(0)[...]
                dst[...] = src.astype(BF16)
            body(*ins, *outs, *refs[n_in + n_src + n_out + n_dst:])
        return kern


def _ffn_ln_mla(x, pos_b, rope, w_in, w_out, g, b, wlat, qg, kvg, wuq, wuk, wuvt, riders):
    t = x.shape[0]
    uq_cols = HEADS * (MLA_NOPE + 2 * MLA_ROPE)
    assert TOKEN_TILE == ATT_K
    cast = _CastRiders(t // TOKEN_TILE, riders)
    return pl.pallas_call(
        cast.wrap(_ffn_ln_mla_kernel, 15, 4),
        name="ffn_ln_mla",
        grid=(t // TOKEN_TILE,),
        in_specs=[
            _row_spec(TOKEN_TILE, D_MODEL),
            pl.BlockSpec((1, 1, TOKEN_TILE), lambda i: (i, 0, 0)),
            _const_spec((1, LANES)),
            _const_spec((TOKEN_TILE, LANES)),
            _const_spec((TOKEN_TILE, LANES)),
            _const_spec((D_MODEL, 2 * D_FF)),
            _const_spec((D_FF, D_MODEL)),
            _const_spec((1, D_MODEL)),
            _const_spec((1, D_MODEL)),
            _const_spec((MLA_LAT, D_MODEL)),
            _const_spec((1, Q_LORA)),
            _const_spec((1, KV_LORA)),
            _const_spec((Q_LORA, uq_cols)),
            _const_spec((KV_LORA, HEADS * MLA_NOPE)),
            _const_spec((HEADS * MLA_DV, KV_LORA)),
        ] + cast.in_specs,
        out_specs=[
            _row_spec(TOKEN_TILE, D_MODEL),
            _row_spec(TOKEN_TILE, HEADS * MLA_QK),
            _row_spec(TOKEN_TILE, HEADS * MLA_QK),
            pl.BlockSpec((1, HEADS * MLA_DVA, TOKEN_TILE), lambda i: (i, 0, 0)),
        ] + cast.out_specs,
        out_shape=[
            jax.ShapeDtypeStruct((t, D_MODEL), F32),
            jax.ShapeDtypeStruct((t, HEADS * MLA_QK), BF16),
            jax.ShapeDtypeStruct((t, HEADS * MLA_QK), BF16),
            jax.ShapeDtypeStruct((t // TOKEN_TILE, HEADS * MLA_DVA, TOKEN_TILE), BF16),
        ] + cast.out_shapes,
        scratch_shapes=[pltpu.VMEM((TOKEN_TILE, D_FF), BF16)]
        + [pltpu.VMEM((TOKEN_TILE, LANES), F32)] * 2,
        compiler_params=_params("parallel"),
    )(x, pos_b, *rope, w_in, w_out, g, b, wlat, qg, kvg, wuq, wuk, wuvt, *cast.arrays)


def _ffn_ple_ln_kernel(h_ref, p_ref, w_in_ref, w_out_ref, wg_ref, wp_ref,
                       g2_ref, b2_ref, g3_ref, b3_ref, o_ref, act_ref):
    chains = _row_chains(h_ref.shape[0])
    h2 = []
    for c, rows in enumerate(chains):
        h = h_ref[rows, :]
        f = _swiglu(h.astype(BF16), w_in_ref, w_out_ref, act_ref.at[c])
        h2.append(_layer_norm(ALPHA * h + 0.5 * f, g2_ref[...], b2_ref[...]))
    for c, rows in enumerate(chains):
        gate = jax.nn.sigmoid(_dot(h2[c].astype(BF16), wg_ref[...]))
        proj = _dot(p_ref[rows, :].astype(BF16), wp_ref[...])
        o_ref[rows, :] = _layer_norm(ALPHA * h2[c] + gate * proj, g3_ref[...], b3_ref[...])


def _ffn_ple_ln(h, p, w_in, w_out, wg, wp, g2, b2, g3, b3):
    t = h.shape[0]
    return pl.pallas_call(
        _ffn_ple_ln_kernel,
        name="ffn_ple_ln",
        grid=(t // FFN_TILE,),
        in_specs=[
            _row_spec(FFN_TILE, D_MODEL),
            _row_spec(FFN_TILE, D_PLE),
            _const_spec((D_MODEL, 2 * D_FF)),
            _const_spec((D_FF, D_MODEL)),
            _const_spec((D_MODEL, D_MODEL)),
            _const_spec((D_PLE, D_MODEL)),
            _const_spec((1, D_MODEL)),
            _const_spec((1, D_MODEL)),
            _const_spec((1, D_MODEL)),
            _const_spec((1, D_MODEL)),
        ],
        out_specs=_row_spec(FFN_TILE, D_MODEL),
        out_shape=jax.ShapeDtypeStruct((t, D_MODEL), F32),
        scratch_shapes=[pltpu.VMEM((FFN_TILE // FFN_CHAIN, FFN_CHAIN, D_FF), BF16)],
        compiler_params=_params("parallel"),
    )(h, p, w_in, w_out, wg, wp, g2, b2, g3, b3)


def _rope_tables(pos_ref, invf_ref, dcos_ref, dsin_ref, cos_ref, sin_ref):
    pos = pos_ref[0]
    base = pos[:, 0:1]
    offs = lax.broadcasted_iota(jnp.int32, pos.shape, 1).astype(F32)
    exact_f32_int = 2.0 ** 24 - pos.shape[1]
    consecutive = jnp.logical_and(jnp.all(pos == base + offs),
                                  jnp.all(jnp.abs(base) < exact_f32_int))
    a = base * invf_ref[...]
    ca = jnp.cos(a)
    sa = jnp.sin(a)
    cos_ref[...] = ca * dcos_ref[...] - sa * dsin_ref[...]
    sin_ref[...] = sa * dcos_ref[...] + ca * dsin_ref[...]
    return consecutive


def _rope_tables_direct(valid, pos_ref, invf_ref, cos_ref, sin_ref):
    @pl.when(jnp.logical_not(valid))
    def _():
        pos = jnp.broadcast_to(pos_ref[0], (LANES, pos_ref.shape[2])).T
        ang = pos * invf_ref[...]
        cos_ref[...] = jnp.cos(ang)
        sin_ref[...] = jnp.sin(ang)


def _retention_kernel(h_ref, pos_ref, invf_ref, dcos_ref, dsin_ref, xi_ref, zeta_ref, dmask_ref,
                      cdec_ref, w_ref, o_ref, rq_ref, rqx_ref, rk_ref, rkz_ref, rv_ref, sg_ref,
                      state_ref, cos_ref, sin_ref, *, tiles_per_seq):
    @pl.when(pl.program_id(0) % tiles_per_seq == 0)
    def _():
        state_ref[...] = jnp.zeros_like(state_ref)

    rope_ok = _rope_tables(pos_ref, invf_ref, dcos_ref, dsin_ref, cos_ref, sin_ref)
    _rope_tables_direct(rope_ok, pos_ref, invf_ref, cos_ref, sin_ref)
    hb = h_ref[...].astype(BF16)
    cos = cos_ref[...]
    lane = lax.broadcasted_iota(jnp.int32, (1, LANES), 1)
    sin = sin_ref[...] * jnp.where(lane < RET_DK // 2, -1.0, 1.0)
    k_scale = RET_DK ** -0.5
    g = _dot_nt(hb, w_ref[2 * RET_QK + RET_V:2 * RET_QK + 2 * RET_V, :])
    sg_ref[...] = (g * jax.nn.sigmoid(g)).astype(BF16)
    q = _dot_nt(hb, w_ref[0:RET_QK, :])
    k = _dot_nt(hb, w_ref[RET_QK:2 * RET_QK, :])
    for h in range(HEADS):
        sl = slice(h * RET_DK, (h + 1) * RET_DK)
        tq = q[:, sl]
        tq = tq * cos + pltpu.roll(tq, RET_DK // 2, 1) * sin
        rq_ref[:, sl] = tq.astype(BF16)
        rqx_ref[:, sl] = (tq * xi_ref[:, sl]).astype(BF16)
        tk = k[:, sl]
        tk = (tk * cos + pltpu.roll(tk, RET_DK // 2, 1) * sin) * k_scale
        rk_ref[:, sl] = tk.astype(BF16)
        rkz_ref[:, sl] = (tk * zeta_ref[:, sl]).astype(BF16)
    rv_ref[...] = _dot_nt(hb, w_ref[2 * RET_QK:2 * RET_QK + RET_V, :]).astype(BF16)

    def qk(rows, h):
        qs = slice(h * RET_DK, (h + 1) * RET_DK)
        return lax.dot_general(rq_ref[rows, qs], rk_ref[rows, qs], (((1,), (1,)), ((), ())),
                               preferred_element_type=F32)

    for blk in range(TOKEN_TILE // RET_BLOCK):
        rows = slice(blk * RET_BLOCK, (blk + 1) * RET_BLOCK)
        s_next = qk(rows, 0)
        for h in range(HEADS):
            qs = slice(h * RET_DK, (h + 1) * RET_DK)
            vs = slice(h * RET_DV, (h + 1) * RET_DV)
            v = rv_ref[rows, vs]
            state = state_ref[h]
            s = s_next
            if h + 1 < HEADS:
                s_next = qk(rows, h + 1)
            cross = _dot(rqx_ref[rows, qs], state.astype(BF16))
            upd = lax.dot_general(rkz_ref[rows, qs], v, (((0,), (0,)), ((), ())),
                                  preferred_element_type=F32)
            state_ref[h] = state * cdec_ref[h] + upd
            y = _dot((s * dmask_ref[h]).astype(BF16), v) + cross
            mu = jnp.mean(y, axis=-1, keepdims=True)
            d = y - mu
            var = jnp.mean(d * d, axis=-1, keepdims=True)
            o_ref[rows, vs] = (d * lax.rsqrt(var + EPS) * sg_ref[rows, vs].astype(F32)).astype(BF16)


def _retention(h, pos_b, rope, xi, zeta, dmask, cdec, w, seq_len):
    t = h.shape[0]
    return pl.pallas_call(
        functools.partial(_retention_kernel, tiles_per_seq=seq_len // TOKEN_TILE),
        name="retention",
        grid=(t // TOKEN_TILE,),
        in_specs=[
            _row_spec(TOKEN_TILE, D_MODEL),
            pl.BlockSpec((1, 1, TOKEN_TILE), lambda i: (i, 0, 0)),
            _const_spec((1, LANES)),
            _const_spec((TOKEN_TILE, LANES)),
            _const_spec((TOKEN_TILE, LANES)),
            _const_spec((TOKEN_TILE, RET_QK)),
            _const_spec((TOKEN_TILE, RET_QK)),
            _const_spec((HEADS, RET_BLOCK, RET_BLOCK)),
            _const_spec((HEADS, 1, RET_DV)),
            _const_spec((2 * RET_QK + 2 * RET_V, D_MODEL)),
        ],
        out_specs=_row_spec(TOKEN_TILE, RET_V),
        out_shape=jax.ShapeDtypeStruct((t, RET_V), BF16),
        scratch_shapes=[pltpu.VMEM((TOKEN_TILE, RET_QK), BF16)] * 4
        + [pltpu.VMEM((TOKEN_TILE, RET_V), BF16)] * 2
        + [pltpu.VMEM((HEADS, RET_DK, RET_DV), F32)]
        + [pltpu.VMEM((TOKEN_TILE, LANES), F32)] * 2,
        compiler_params=_params("arbitrary"),
    )(h, pos_b, *rope, xi, zeta, dmask, cdec, w)


MLA_LAT = Q_LORA + KV_LORA + LANES


def _mla_proj(hb, w_ref, qg_ref, kvg_ref, wuq_ref, wuk_ref, wuvt_ref,
              qcat_ref, kcat_ref, vt_ref, cos_ref, sin_ref):
    lane = lax.broadcasted_iota(jnp.int32, (1, LANES), 1)
    first_half = lane < MLA_ROPE

    lat = _dot_nt(hb, w_ref[...])
    cq = lat[:, 0:Q_LORA]
    ckv = lat[:, Q_LORA:Q_LORA + KV_LORA]
    kpe = lat[:, Q_LORA + KV_LORA:MLA_LAT]
    kpe = kpe + pltpu.roll(kpe, MLA_ROPE, 1)
    kpe_rot = pltpu.roll(kpe, MLA_ROPE // 2, 1) * jnp.where(
        lane % MLA_ROPE < MLA_ROPE // 2, -1.0, 1.0)

    qf = _dot(_rms_norm(cq, qg_ref[...]).astype(BF16), wuq_ref[...])
    nope_w = HEADS * MLA_NOPE
    rope_w = HEADS * MLA_ROPE
    for h in range(HEADS):
        qcat_ref[:, h * MLA_QK:h * MLA_QK + MLA_NOPE] = (
            qf[:, h * MLA_NOPE:(h + 1) * MLA_NOPE].astype(BF16))

    ckvn = _rms_norm(ckv, kvg_ref[...])
    kv = _dot(ckvn.astype(BF16), wuk_ref[...])
    vt = _dot(wuvt_ref[...], ckvn.T.astype(BF16)).astype(BF16)
    ones = jnp.ones((MLA_DVA - MLA_DV, vt.shape[1]), BF16)
    for h in range(HEADS):
        vt_ref[0, h * MLA_DVA:h * MLA_DVA + MLA_DV, :] = vt[h * MLA_DV:(h + 1) * MLA_DV, :]
        vt_ref[0, h * MLA_DVA + MLA_DV:(h + 1) * MLA_DVA, :] = ones
        kcat_ref[:, h * MLA_QK:h * MLA_QK + MLA_NOPE] = (
            kv[:, h * MLA_NOPE:(h + 1) * MLA_NOPE].astype(BF16))

    cos = cos_ref[...]
    sin = sin_ref[...]
    for j in range(HEADS // 2):
        r = (qf[:, nope_w + j * LANES:nope_w + (j + 1) * LANES] * cos
             + qf[:, nope_w + rope_w + j * LANES:nope_w + rope_w + (j + 1) * LANES] * sin)
        base = 2 * j * MLA_QK
        qcat_ref[:, base + MLA_NOPE:base + MLA_QK] = jnp.where(first_half, r, 0.0).astype(BF16)
        base += MLA_QK
        qcat_ref[:, base + MLA_NOPE:base + MLA_QK] = jnp.where(first_half, 0.0, r).astype(BF16)
    kr = kpe * cos + kpe_rot * sin
    kr_even = jnp.where(first_half, kr, 0.0).astype(BF16)
    kr_odd = jnp.where(first_half, 0.0, kr).astype(BF16)
    for h in range(HEADS):
        kcat_ref[:, h * MLA_QK + MLA_NOPE:(h + 1) * MLA_QK] = kr_even if h % 2 == 0 else kr_odd


def _softmax_step(st, mx, vt, stats):
    m, acc = stats
    m_new = jnp.maximum(m, mx)
    a = jnp.exp2(m - m_new)
    p = jnp.exp2(st - m_new)
    acc = a * acc + _dot(vt, p.astype(BF16))
    return m_new, acc


def _mla_attn_kernel(q_ref, qn_ref, k_ref, vt_ref, o_ref, s0_ref, s1_ref, mx0_ref, mx1_ref,
                     m_ref, acc_ref):
    qi = pl.program_id(2)
    heads = range(ATT_HEADS)

    def scores(qr, j, buf):
        s_ref, mx_ref = buf
        off = pl.multiple_of(j * ATT_K, ATT_K)
        for hh in heads:
            st = lax.dot_general(
                k_ref[0, pl.ds(off, ATT_K), hh * MLA_QK:(hh + 1) * MLA_QK],
                qr[0, :, hh * MLA_QK:(hh + 1) * MLA_QK],
                (((1,), (1,)), ((), ())), preferred_element_type=F32)
            s_ref[hh] = st
            mx_ref[hh] = jnp.max(st, axis=0, keepdims=True)

    def consume(j, buf, mask=None):
        s_ref, mx_ref = buf
        for hh in heads:
            st = s_ref[hh]
            if mask is None:
                mx = mx_ref[hh]
            else:
                st = jnp.where(mask, st, -jnp.inf)
                mx = jnp.max(st, axis=0, keepdims=True)
            m, acc = _softmax_step(st, mx, vt_ref[0, j, hh * MLA_DVA:(hh + 1) * MLA_DVA, :],
                                   (m_ref[hh], acc_ref[hh]))
            m_ref[hh] = m
            acc_ref[hh] = acc

    m_ref[...] = jnp.full_like(m_ref, NEG_BIG)
    acc_ref[...] = jnp.zeros_like(acc_ref)

    bufs = ((s0_ref, mx0_ref), (s1_ref, mx1_ref))

    @pl.when(qi == 0)
    def _():
        scores(q_ref, 0, bufs[0])

    key_chunk = lax.broadcasted_iota(jnp.int32, (ATT_K, ATT_Q), 0) // CHUNK
    qry_chunk = lax.broadcasted_iota(jnp.int32, (ATT_K, ATT_Q), 1) // CHUNK
    mask = key_chunk <= qry_chunk

    def step(first, odd):
        cur, oth = bufs[first], bufs[1 - first]

        def pair(jj, c):
            j = 2 * jj
            scores(q_ref, j + 1, oth)
            consume(j, cur)
            scores(q_ref, j + 2, cur)
            consume(j + 1, oth)
            return c

        lax.fori_loop(0, qi // 2, pair, 0)
        if odd:
            scores(q_ref, qi, oth)
            consume(qi - 1, cur)
            scores(qn_ref, 0, cur)
            consume(qi, oth, mask)
        else:
            scores(qn_ref, 0, oth)
            consume(qi, cur, mask)

    for c in range(4):
        pl.when(qi % 4 == c)(functools.partial(step, ((c + 1) // 2) % 2, c % 2 == 1))

    for hh in heads:
        l = acc_ref[hh, MLA_DV:MLA_DV + 1, :]
        o_ref[0, :, hh * MLA_DV:(hh + 1) * MLA_DV] = (
            acc_ref[hh, 0:MLA_DV, :] / l).T.astype(BF16)


def _mla_attn(qcat, kcat, vt):
    b, s, _ = qcat.shape
    assert ATT_Q == ATT_K
    n_kb = s // ATT_K
    last_q = s // ATT_Q - 1
    return pl.pallas_call(
        _mla_attn_kernel,
        name="mla_attn",
        grid=(b, HEADS // ATT_HEADS, s // ATT_Q),
        in_specs=[
            pl.BlockSpec((1, ATT_Q, ATT_HEADS * MLA_QK), lambda i, h, j: (i, j, h)),
            pl.BlockSpec((1, ATT_Q, ATT_HEADS * MLA_QK),
                         lambda i, h, j: (i, jnp.minimum(j + 1, last_q), h)),
            pl.BlockSpec((1, s, ATT_HEADS * MLA_QK), lambda i, h, j: (i, 0, h)),
            pl.BlockSpec((1, n_kb, ATT_HEADS * MLA_DVA, ATT_K), lambda i, h, j: (i, 0, h, 0)),
        ],
        out_specs=pl.BlockSpec((1, ATT_Q, ATT_HEADS * MLA_DV), lambda i, h, j: (i, j, h)),
        out_shape=jax.ShapeDtypeStruct((b, s, HEADS * MLA_DV), BF16),
        scratch_shapes=[
            pltpu.VMEM((ATT_HEADS, ATT_K, ATT_Q), F32),
            pltpu.VMEM((ATT_HEADS, ATT_K, ATT_Q), F32),
            pltpu.VMEM((ATT_HEADS, 1, ATT_Q), F32),
            pltpu.VMEM((ATT_HEADS, 1, ATT_Q), F32),
            pltpu.VMEM((ATT_HEADS, 1, ATT_Q), F32),
            pltpu.VMEM((ATT_HEADS, MLA_DVA, ATT_Q), F32),
        ],
        compiler_params=_params("parallel", "parallel", "arbitrary"),
    )(qcat, qcat, kcat, vt)


def _merge_ln_kernel(yr_ref, om_ref, h_ref, wg_ref, wro_ref, wmo_ref, wout_ref,
                     g_ref, b_ref, o_ref):
    for rows in _row_chains(h_ref.shape[0]):
        h = h_ref[rows, :]
        hb = h.astype(BF16)
        mix = (jax.nn.sigmoid(_dot_nt(hb, wg_ref[0:D_MODEL, :]))
               * _dot(yr_ref[rows, :], wro_ref[...])
               + jax.nn.sigmoid(_dot_nt(hb, wg_ref[D_MODEL:2 * D_MODEL, :]))
               * _dot(om_ref[rows, :], wmo_ref[...]))
        mixed = _dot(mix.astype(BF16), wout_ref[...])
        o_ref[rows, :] = _layer_norm(ALPHA * h + mixed, g_ref[...], b_ref[...])


def _merge_ln(yr, om, h, wg, wro, wmo, wout, g, b, riders):
    t = h.shape[0]
    cast = _CastRiders(t // FFN_TILE, riders)
    return pl.pallas_call(
        cast.wrap(_merge_ln_kernel, 9, 1),
        name="merge_ln",
        grid=(t // FFN_TILE,),
        in_specs=[
            _row_spec(FFN_TILE, RET_V),
            _row_spec(FFN_TILE, HEADS * MLA_DV),
            _row_spec(FFN_TILE, D_MODEL),
            _const_spec((2 * D_MODEL, D_MODEL)),
            _const_spec((RET_V, D_MODEL)),
            _const_spec((HEADS * MLA_DV, D_MODEL)),
            _const_spec((D_MODEL, D_MODEL)),
            _const_spec((1, D_MODEL)),
            _const_spec((1, D_MODEL)),
        ] + cast.in_specs,
        out_specs=[_row_spec(FFN_TILE, D_MODEL)] + cast.out_specs,
        out_shape=[jax.ShapeDtypeStruct((t, D_MODEL), F32)] + cast.out_shapes,
        compiler_params=_params("parallel"),
    )(yr, om, h, wg, wro, wmo, wout, g, b, *cast.arrays)


def _retention_tables():
    log_gamma = jnp.log(1.0 - 2.0 ** (-5.0 - jnp.arange(HEADS, dtype=F32)))
    idx = jnp.arange(RET_BLOCK, dtype=F32)
    chunk = jnp.arange(RET_BLOCK) // CHUNK
    visible = chunk[None, :] <= chunk[:, None]
    dist = jnp.abs(idx[:, None] - idx[None, :])
    dmask = jnp.where(visible[None], jnp.exp(log_gamma[:, None, None] * dist[None]), 0.0)
    xi = jnp.exp(log_gamma[:, None] * (idx + 1.0))
    zeta = jnp.exp(log_gamma[:, None] * (RET_BLOCK - 1.0 - idx))
    cdec = jnp.exp(log_gamma * RET_BLOCK)
    def per_token_rows(tab):
        rows = jnp.broadcast_to(tab.T[:, :, None], (RET_BLOCK, HEADS, RET_DK))
        return jnp.tile(rows.reshape(RET_BLOCK, RET_QK), (TOKEN_TILE // RET_BLOCK, 1))

    return (dmask, per_token_rows(xi), per_token_rows(zeta),
            jnp.broadcast_to(cdec[:, None, None], (HEADS, 1, RET_DV)))


def _rope_consts(half):
    invf = ROPE_BASE ** (-jnp.arange(half, dtype=F32) / half)
    invf = jnp.tile(invf, LANES // half).reshape(1, LANES)
    delta = jnp.arange(TOKEN_TILE, dtype=F32)[:, None] * invf
    return invf, jnp.cos(delta), jnp.sin(delta)


def kernel(x, p, positions, ln_g, ln_b, ffn1_w_in, ffn1_w_out, w_in, ret_gn_g, w_ret_o,
           q_norm_g, kv_norm_g, w_uq, w_ukv, w_mla_o, w_out, ffn2_w_in, ffn2_w_out,
           ple_w_gate, ple_w_proj):
    b, s, d = x.shape
    t = b * s
    h = x.reshape(t, d)
    pos_b = positions.astype(F32).reshape(t // TOKEN_TILE, 1, TOKEN_TILE)
    row = lambda v: v.reshape(1, -1)

    for i in range(ln_g.shape[0]):
        w_in_t = jnp.swapaxes(w_in[i], 0, 1)
        c0 = 2 * RET_QK + 2 * RET_V
        c1 = c0 + Q_LORA + KV_LORA + MLA_ROPE
        w_mla = jnp.pad(w_in_t[c0:c1], ((0, LANES - MLA_ROPE), (0, 0))).astype(BF16)

        q_scale = (MLA_NOPE + MLA_ROPE) ** -0.5 * math.log2(math.e)
        uq = w_uq[i].reshape(Q_LORA, HEADS, MLA_NOPE + MLA_ROPE) * q_scale
        uq_nope = uq[:, :, :MLA_NOPE].reshape(Q_LORA, HEADS * MLA_NOPE)
        uq_rope = uq[:, :, MLA_NOPE:]
        uq_rot = jnp.concatenate(
            [-uq_rope[:, :, MLA_ROPE // 2:], uq_rope[:, :, :MLA_ROPE // 2]], axis=2)
        wuq = jnp.concatenate(
            [uq_nope, uq_rope.reshape(Q_LORA, -1), uq_rot.reshape(Q_LORA, -1)], axis=1).astype(BF16)
        ukv = w_ukv[i].reshape(KV_LORA, HEADS, MLA_NOPE + MLA_DV)
        wuk = ukv[:, :, :MLA_NOPE].reshape(KV_LORA, -1).astype(BF16)
        wuvt = ukv[:, :, MLA_NOPE:].reshape(KV_LORA, -1).T.astype(BF16)

        rope_ret = _rope_consts(RET_DK // 2)
        rope_mla = _rope_consts(MLA_ROPE // 2)

        (h, qcat, kcat, vt,
         w_ret, w_gates, w_ffn2_in, w_ro, w_mo, w_o, w_pg) = _ffn_ln_mla(
            h, pos_b, rope_mla, ffn1_w_in[i].astype(BF16), ffn1_w_out[i].astype(BF16),
            row(ln_g[i, 0]), row(ln_b[i, 0]),
            w_mla, row(q_norm_g[i]), row(kv_norm_g[i]), wuq, wuk, wuvt,
            riders=[(w_in_t, 0, c0, None), (w_in_t, c1, 2 * D_MODEL, None),
                    (ffn2_w_in[i], 0, D_MODEL, None), (w_ret_o[i], 0, RET_V, ret_gn_g[i]),
                    (w_mla_o[i], 0, HEADS * MLA_DV, None), (w_out[i], 0, D_MODEL, None),
                    (ple_w_gate[i], 0, D_MODEL, None)])

        dmask, xi, zeta, cdec = _retention_tables()
        y_ret = _retention(h, pos_b, rope_ret, xi, zeta, dmask, cdec, w_ret, s)
        o_mla = _mla_attn(qcat.reshape(b, s, -1), kcat.reshape(b, s, -1),
                          vt.reshape(b, s // ATT_K, HEADS * MLA_DVA, ATT_K))
        h, w_ffn2_out, w_pp = _merge_ln(
            y_ret, o_mla.reshape(t, -1), h, w_gates, w_ro, w_mo, w_o,
            row(ln_g[i, 1]), row(ln_b[i, 1]),
            riders=[(ffn2_w_out[i], 0, D_FF, None), (ple_w_proj[i], 0, D_PLE, None)])

        h = _ffn_ple_ln(h, p[i].reshape(t, -1), w_ffn2_in, w_ffn2_out, w_pg, w_pp,
                        row(ln_g[i, 2]), row(ln_b[i, 2]), row(ln_g[i, 3]), row(ln_b[i, 3]))
    return h.reshape(b, s, d)
```

```python
import functools
import math

import jax
import jax.numpy as jnp
from jax import lax
from jax.experimental import pallas as pl
from jax.experimental.pallas import tpu as pltpu

F32 = jnp.float32
BF16 = jnp.bfloat16

D_MODEL = 1024
CHUNK = 64
D_PLE = 256
D_FF = 2816
HEADS = 8
RET_DK = 128
RET_DV = 256
RET_QK = HEADS * RET_DK
RET_V = HEADS * RET_DV
MLA_NOPE = 128
MLA_ROPE = 64
MLA_DV = 128
Q_LORA = 256
KV_LORA = 256
ROPE_BASE = 10000.0
EPS = 1e-5
ALPHA = 2.0 ** 0.25

LANES = 128
FF_CHUNK = 256
TOKEN_TILE = 512
FFN_TILE = 1024
FFN_CHAIN = 512
RET_BLOCK = 256
ATT_Q = 512
ATT_K = 512
ATT_HEADS = 2
MLA_QK = 256
BF16_ROWS = 16
MLA_DVA = MLA_DV + BF16_ROWS
VMEM_LIMIT = 52 * 1024 * 1024
NEG_BIG = -1e30


def _const_spec(shape):
    nd = len(shape)
    return pl.BlockSpec(shape, lambda *_: (0,) * nd, pipeline_mode=pl.Buffered(1))


def _row_spec(tile, width):
    return pl.BlockSpec((tile, width), lambda i: (i, 0))


def _params(*sem):
    return pltpu.CompilerParams(dimension_semantics=sem, vmem_limit_bytes=VMEM_LIMIT)


def _dot(a, b):
    return jnp.dot(a, b, preferred_element_type=F32)


def _dot_nt(a, bt):
    return lax.dot_general(a, bt, (((1,), (1,)), ((), ())), preferred_element_type=F32)


def _layer_norm(y, g, b):
    mu = jnp.mean(y, axis=-1, keepdims=True)
    d = y - mu
    var = jnp.mean(d * d, axis=-1, keepdims=True)
    return d * lax.rsqrt(var + EPS) * g + b


def _rms_norm(x, g):
    return x * lax.rsqrt(jnp.mean(x * x, axis=-1, keepdims=True) + EPS) * g


def _swiglu(xb, w_in_ref, w_out_ref, act_ref):
    for c in range(D_FF // FF_CHUNK):
        lo = c * FF_CHUNK
        g = _dot(xb, w_in_ref[:, lo:lo + FF_CHUNK])
        u = _dot(xb, w_in_ref[:, D_FF + lo:D_FF + lo + FF_CHUNK])
        act_ref[:, lo:lo + FF_CHUNK] = (g * jax.nn.sigmoid(g) * u).astype(BF16)
    return _dot(act_ref[...], w_out_ref[...])


def _row_chains(tile):
    return [slice(r, r + FFN_CHAIN) for r in range(0, tile, FFN_CHAIN)]


def _ffn_ln_mla_kernel(x_ref, pos_ref, invf_ref, dcos_ref, dsin_ref, w_in_ref, w_out_ref,
                       g_ref, b_ref, wlat_ref, qg_ref, kvg_ref, wuq_ref, wuk_ref, wuvt_ref,
                       o_ref, qcat_ref, kcat_ref, vt_ref, act_ref, cos_ref, sin_ref):
    rope_ok = _rope_tables(pos_ref, invf_ref, dcos_ref, dsin_ref, cos_ref, sin_ref)
    _rope_tables_direct(rope_ok, pos_ref, invf_ref, cos_ref, sin_ref)
    x = x_ref[...]
    f = _swiglu(x.astype(BF16), w_in_ref, w_out_ref, act_ref)
    h = _layer_norm(ALPHA * x + 0.5 * f, g_ref[...], b_ref[...])
    o_ref[...] = h
    _mla_proj(h.astype(BF16), wlat_ref, qg_ref, kvg_ref, wuq_ref, wuk_ref, wuvt_ref,
              qcat_ref, kcat_ref, vt_ref, cos_ref, sin_ref)


class _CastRiders:
    def __init__(self, steps, riders):
        self.arrays, self.scaled = [], []
        self.in_specs, self.out_specs, self.out_shapes = [], [], []
        for a, first, count, scale in riders:
            rows, cols = count // steps, a.shape[1]
            assert rows * steps == count and rows % BF16_ROWS == 0 and first % rows == 0
            self.arrays.append(a)
            self.in_specs.append(
                pl.BlockSpec((rows, cols), lambda i, o=first // rows: (i + o, 0)))
            self.scaled.append(scale is not None)
            if scale is not None:
                self.arrays.append(scale.reshape(count, 1))
                self.in_specs.append(pl.BlockSpec((rows, 1), lambda i: (i, 0)))
            self.out_specs.append(pl.BlockSpec((rows, cols), lambda i: (i, 0)))
            self.out_shapes.append(jax.ShapeDtypeStruct((count, cols), BF16))

    def wrap(self, body, n_in, n_out):
        n_src, n_dst = len(self.arrays), len(self.scaled)

        def kern(*refs):
            ins, rin = refs[:n_in], list(refs[n_in:n_in + n_src])
            outs = refs[n_in + n_src:n_in + n_src + n_out]
            rout = refs[n_in + n_src + n_out:n_in + n_src + n_out + n_dst]
            for dst, scaled in zip(rout, self.scaled):
                src = rin.pop(0)[...]
                if scaled:
                    src = src * rin.pop(0)[...]
                dst[...] = src.astype(BF16)
            body(*ins, *outs, *refs[n_in + n_src + n_out + n_dst:])
        return kern


def _ffn_ln_mla(x, pos_b, rope, w_in, w_out, g, b, wlat, qg, kvg, wuq, wuk, wuvt, riders):
    t = x.shape[0]
    uq_cols = HEADS * (MLA_NOPE + 2 * MLA_ROPE)
    assert TOKEN_TILE == ATT_K
    cast = _CastRiders(t // TOKEN_TILE, riders)
    return pl.pallas_call(
        cast.wrap(_ffn_ln_mla_kernel, 15, 4),
        name="ffn_ln_mla",
        grid=(t // TOKEN_TILE,),
        in_specs=[
            _row_spec(TOKEN_TILE, D_MODEL),
            pl.BlockSpec((1, 1, TOKEN_TILE), lambda i: (i, 0, 0)),
            _const_spec((1, LANES)),
            _const_spec((TOKEN_TILE, LANES)),
            _const_spec((TOKEN_TILE, LANES)),
            _const_spec((D_MODEL, 2 * D_FF)),
            _const_spec((D_FF, D_MODEL)),
            _const_spec((1, D_MODEL)),
            _const_spec((1, D_MODEL)),
            _const_spec((MLA_LAT, D_MODEL)),
            _const_spec((1, Q_LORA)),
            _const_spec((1, KV_LORA)),
            _const_spec((Q_LORA, uq_cols)),
            _const_spec((KV_LORA, HEADS * MLA_NOPE)),
            _const_spec((HEADS * MLA_DV, KV_LORA)),
        ] + cast.in_specs,
        out_specs=[
            _row_spec(TOKEN_TILE, D_MODEL),
            _row_spec(TOKEN_TILE, HEADS * MLA_QK),
            _row_spec(TOKEN_TILE, HEADS * MLA_QK),
            pl.BlockSpec((1, HEADS * MLA_DVA, TOKEN_TILE), lambda i: (i, 0, 0)),
        ] + cast.out_specs,
        out_shape=[
            jax.ShapeDtypeStruct((t, D_MODEL), F32),
            jax.ShapeDtypeStruct((t, HEADS * MLA_QK), BF16),
            jax.ShapeDtypeStruct((t, HEADS * MLA_QK), BF16),
            jax.ShapeDtypeStruct((t // TOKEN_TILE, HEADS * MLA_DVA, TOKEN_TILE), BF16),
        ] + cast.out_shapes,
        scratch_shapes=[pltpu.VMEM((TOKEN_TILE, D_FF), BF16)]
        + [pltpu.VMEM((TOKEN_TILE, LANES), F32)] * 2,
        compiler_params=_params("parallel"),
    )(x, pos_b, *rope, w_in, w_out, g, b, wlat, qg, kvg, wuq, wuk, wuvt, *cast.arrays)


def _ffn_ple_ln_kernel(h_ref, p_ref, w_in_ref, w_out_ref, wg_ref, wp_ref,
                       g2_ref, b2_ref, g3_ref, b3_ref, o_ref, act_ref):
    chains = _row_chains(h_ref.shape[0])
    h2 = []
    for c, rows in enumerate(chains):
        h = h_ref[rows, :]
        f = _swiglu(h.astype(BF16), w_in_ref, w_out_ref, act_ref.at[c])
        h2.append(_layer_norm(ALPHA * h + 0.5 * f, g2_ref[...], b2_ref[...]))
    for c, rows in enumerate(chains):
        gate = jax.nn.sigmoid(_dot(h2[c].astype(BF16), wg_ref[...]))
        proj = _dot(p_ref[rows, :].astype(BF16), wp_ref[...])
        o_ref[rows, :] = _layer_norm(ALPHA * h2[c] + gate * proj, g3_ref[...], b3_ref[...])


def _ffn_ple_ln(h, p, w_in, w_out, wg, wp, g2, b2, g3, b3):
    t = h.shape[0]
    return pl.pallas_call(
        _ffn_ple_ln_kernel,
        name="ffn_ple_ln",
        grid=(t // FFN_TILE,),
        in_specs=[
            _row_spec(FFN_TILE, D_MODEL),
            _row_spec(FFN_TILE, D_PLE),
            _const_spec((D_MODEL, 2 * D_FF)),
            _const_spec((D_FF, D_MODEL)),
            _const_spec((D_MODEL, D_MODEL)),
            _const_spec((D_PLE, D_MODEL)),
            _const_spec((1, D_MODEL)),
            _const_spec((1, D_MODEL)),
            _const_spec((1, D_MODEL)),
            _const_spec((1, D_MODEL)),
        ],
        out_specs=_row_spec(FFN_TILE, D_MODEL),
        out_shape=jax.ShapeDtypeStruct((t, D_MODEL), F32),
        scratch_shapes=[pltpu.VMEM((FFN_TILE // FFN_CHAIN, FFN_CHAIN, D_FF), BF16)],
        compiler_params=_params("parallel"),
    )(h, p, w_in, w_out, wg, wp, g2, b2, g3, b3)


def _rope_tables(pos_ref, invf_ref, dcos_ref, dsin_ref, cos_ref, sin_ref):
    pos = pos_ref[0]
    base = pos[:, 0:1]
    offs = lax.broadcasted_iota(jnp.int32, pos.shape, 1).astype(F32)
    exact_f32_int = 2.0 ** 24 - pos.shape[1]
    consecutive = jnp.logical_and(jnp.all(pos == base + offs),
                                  jnp.all(jnp.abs(base) < exact_f32_int))
    a = base * invf_ref[...]
    ca = jnp.cos(a)
    sa = jnp.sin(a)
    cos_ref[...] = ca * dcos_ref[...] - sa * dsin_ref[...]
    sin_ref[...] = sa * dcos_ref[...] + ca * dsin_ref[...]
    return consecutive


def _rope_tables_direct(valid, pos_ref, invf_ref, cos_ref, sin_ref):
    @pl.when(jnp.logical_not(valid))
    def _():
        pos = jnp.broadcast_to(pos_ref[0], (LANES, pos_ref.shape[2])).T
        ang = pos * invf_ref[...]
        cos_ref[...] = jnp.cos(ang)
        sin_ref[...] = jnp.sin(ang)


def _retention_kernel(h_ref, pos_ref, invf_ref, dcos_ref, dsin_ref, xi_ref, zeta_ref, dmask_ref,
                      cdec_ref, w_ref, o_ref, rq_ref, rqx_ref, rk_ref, rkz_ref, rv_ref, sg_ref,
                      state_ref, cos_ref, sin_ref, *, tiles_per_seq):
    @pl.when(pl.program_id(0) % tiles_per_seq == 0)
    def _():
        state_ref[...] = jnp.zeros_like(state_ref)

    rope_ok = _rope_tables(pos_ref, invf_ref, dcos_ref, dsin_ref, cos_ref, sin_ref)
    _rope_tables_direct(rope_ok, pos_ref, invf_ref, cos_ref, sin_ref)
    hb = h_ref[...].astype(BF16)
    cos = cos_ref[...]
    lane = lax.broadcasted_iota(jnp.int32, (1, LANES), 1)
    sin = sin_ref[...] * jnp.where(lane < RET_DK // 2, -1.0, 1.0)
    k_scale = RET_DK ** -0.5
    g = _dot_nt(hb, w_ref[2 * RET_QK + RET_V:2 * RET_QK + 2 * RET_V, :])
    sg_ref[...] = (g * jax.nn.sigmoid(g)).astype(BF16)
    q = _dot_nt(hb, w_ref[0:RET_QK, :])
    k = _dot_nt(hb, w_ref[RET_QK:2 * RET_QK, :])
    for h in range(HEADS):
        sl = slice(h * RET_DK, (h + 1) * RET_DK)
        tq = q[:, sl]
        tq = tq * cos + pltpu.roll(tq, RET_DK // 2, 1) * sin
        rq_ref[:, sl] = tq.astype(BF16)
        rqx_ref[:, sl] = (tq * xi_ref[:, sl]).astype(BF16)
        tk = k[:, sl]
        tk = (tk * cos + pltpu.roll(tk, RET_DK // 2, 1) * sin) * k_scale
        rk_ref[:, sl] = tk.astype(BF16)
        rkz_ref[:, sl] = (tk * zeta_ref[:, sl]).astype(BF16)
    rv_ref[...] = _dot_nt(hb, w_ref[2 * RET_QK:2 * RET_QK + RET_V, :]).astype(BF16)

    def qk(rows, h):
        qs = slice(h * RET_DK, (h + 1) * RET_DK)
        return lax.dot_general(rq_ref[rows, qs], rk_ref[rows, qs], (((1,), (1,)), ((), ())),
                               preferred_element_type=F32)

    for blk in range(TOKEN_TILE // RET_BLOCK):
        rows = slice(blk * RET_BLOCK, (blk + 1) * RET_BLOCK)
        s_next = qk(rows, 0)
        for h in range(HEADS):
            qs = slice(h * RET_DK, (h + 1) * RET_DK)
            vs = slice(h * RET_DV, (h + 1) * RET_DV)
            v = rv_ref[rows, vs]
            state = state_ref[h]
            s = s_next
            if h + 1 < HEADS:
                s_next = qk(rows, h + 1)
            cross = _dot(rqx_ref[rows, qs], state.astype(BF16))
            upd = lax.dot_general(rkz_ref[rows, qs], v, (((0,), (0,)), ((), ())),
                                  preferred_element_type=F32)
            state_ref[h] = state * cdec_ref[h] + upd
            y = _dot((s * dmask_ref[h]).astype(BF16), v) + cross
            mu = jnp.mean(y, axis=-1, keepdims=True)
            d = y - mu
            var = jnp.mean(d * d, axis=-1, keepdims=True)
            o_ref[rows, vs] = (d * lax.rsqrt(var + EPS) * sg_ref[rows, vs].astype(F32)).astype(BF16)


def _retention(h, pos_b, rope, xi, zeta, dmask, cdec, w, seq_len):
    t = h.shape[0]
    return pl.pallas_call(
        functools.partial(_retention_kernel, tiles_per_seq=seq_len // TOKEN_TILE),
        name="retention",
        grid=(t // TOKEN_TILE,),
        in_specs=[
            _row_spec(TOKEN_TILE, D_MODEL),
            pl.BlockSpec((1, 1, TOKEN_TILE), lambda i: (i, 0, 0)),
            _const_spec((1, LANES)),
            _const_spec((TOKEN_TILE, LANES)),
            _const_spec((TOKEN_TILE, LANES)),
            _const_spec((TOKEN_TILE, RET_QK)),
            _const_spec((TOKEN_TILE, RET_QK)),
            _const_spec((HEADS, RET_BLOCK, RET_BLOCK)),
            _const_spec((HEADS, 1, RET_DV)),
            _const_spec((2 * RET_QK + 2 * RET_V, D_MODEL)),
        ],
        out_specs=_row_spec(TOKEN_TILE, RET_V),
        out_shape=jax.ShapeDtypeStruct((t, RET_V), BF16),
        scratch_shapes=[pltpu.VMEM((TOKEN_TILE, RET_QK), BF16)] * 4
        + [pltpu.VMEM((TOKEN_TILE, RET_V), BF16)] * 2
        + [pltpu.VMEM((HEADS, RET_DK, RET_DV), F32)]
        + [pltpu.VMEM((TOKEN_TILE, LANES), F32)] * 2,
        compiler_params=_params("arbitrary"),
    )(h, pos_b, *rope, xi, zeta, dmask, cdec, w)


MLA_LAT = Q_LORA + KV_LORA + LANES


def _mla_proj(hb, w_ref, qg_ref, kvg_ref, wuq_ref, wuk_ref, wuvt_ref,
              qcat_ref, kcat_ref, vt_ref, cos_ref, sin_ref):
    lane = lax.broadcasted_iota(jnp.int32, (1, LANES), 1)
    first_half = lane < MLA_ROPE

    lat = _dot_nt(hb, w_ref[...])
    cq = lat[:, 0:Q_LORA]
    ckv = lat[:, Q_LORA:Q_LORA + KV_LORA]
    kpe = lat[:, Q_LORA + KV_LORA:MLA_LAT]
    kpe = kpe + pltpu.roll(kpe, MLA_ROPE, 1)
    kpe_rot = pltpu.roll(kpe, MLA_ROPE // 2, 1) * jnp.where(
        lane % MLA_ROPE < MLA_ROPE // 2, -1.0, 1.0)

    qf = _dot(_rms_norm(cq, qg_ref[...]).astype(BF16), wuq_ref[...])
    nope_w = HEADS * MLA_NOPE
    rope_w = HEADS * MLA_ROPE
    for h in range(HEADS):
        qcat_ref[:, h * MLA_QK:h * MLA_QK + MLA_NOPE] = (
            qf[:, h * MLA_NOPE:(h + 1) * MLA_NOPE].astype(BF16))

    ckvn = _rms_norm(ckv, kvg_ref[...])
    kv = _dot(ckvn.astype(BF16), wuk_ref[...])
    vt = _dot(wuvt_ref[...], ckvn.T.astype(BF16)).astype(BF16)
    ones = jnp.ones((MLA_DVA - MLA_DV, vt.shape[1]), BF16)
    for h in range(HEADS):
        vt_ref[0, h * MLA_DVA:h * MLA_DVA + MLA_DV, :] = vt[h * MLA_DV:(h + 1) * MLA_DV, :]
        vt_ref[0, h * MLA_DVA + MLA_DV:(h + 1) * MLA_DVA, :] = ones
        kcat_ref[:, h * MLA_QK:h * MLA_QK + MLA_NOPE] = (
            kv[:, h * MLA_NOPE:(h + 1) * MLA_NOPE].astype(BF16))

    cos = cos_ref[...]
    sin = sin_ref[...]
    for j in range(HEADS // 2):
        r = (qf[:, nope_w + j * LANES:nope_w + (j + 1) * LANES] * cos
             + qf[:, nope_w + rope_w + j * LANES:nope_w + rope_w + (j + 1) * LANES] * sin)
        base = 2 * j * MLA_QK
        qcat_ref[:, base + MLA_NOPE:base + MLA_QK] = jnp.where(first_half, r, 0.0).astype(BF16)
        base += MLA_QK
        qcat_ref[:, base + MLA_NOPE:base + MLA_QK] = jnp.where(first_half, 0.0, r).astype(BF16)
    kr = kpe * cos + kpe_rot * sin
    kr_even = jnp.where(first_half, kr, 0.0).astype(BF16)
    kr_odd = jnp.where(first_half, 0.0, kr).astype(BF16)
    for h in range(HEADS):
        kcat_ref[:, h * MLA_QK + MLA_NOPE:(h + 1) * MLA_QK] = kr_even if h % 2 == 0 else kr_odd


def _softmax_step(st, mx, vt, stats):
    m, acc = stats
    m_new = jnp.maximum(m, mx)
    a = jnp.exp2(m - m_new)
    p = jnp.exp2(st - m_new)
    acc = a * acc + _dot(vt, p.astype(BF16))
    return m_new, acc


def _mla_attn_kernel(q_ref, qn_ref, k_ref, vt_ref, o_ref, s0_ref, s1_ref, mx0_ref, mx1_ref,
                     m_ref, acc_ref):
    qi = pl.program_id(2)
    heads = range(ATT_HEADS)

    def scores(qr, j, buf):
        s_ref, mx_ref = buf
        off = pl.multiple_of(j * ATT_K, ATT_K)
        for hh in heads:
            st = lax.dot_general(
                k_ref[0, pl.ds(off, ATT_K), hh * MLA_QK:(hh + 1) * MLA_QK],
                qr[0, :, hh * MLA_QK:(hh + 1) * MLA_QK],
                (((1,), (1,)), ((), ())), preferred_element_type=F32)
            s_ref[hh] = st
            mx_ref[hh] = jnp.max(st, axis=0, keepdims=True)

    def consume(j, buf, mask=None):
        s_ref, mx_ref = buf
        for hh in heads:
            st = s_ref[hh]
            if mask is None:
                mx = mx_ref[hh]
            else:
                st = jnp.where(mask, st, -jnp.inf)
                mx = jnp.max(st, axis=0, keepdims=True)
            m, acc = _softmax_step(st, mx, vt_ref[0, j, hh * MLA_DVA:(hh + 1) * MLA_DVA, :],
                                   (m_ref[hh], acc_ref[hh]))
            m_ref[hh] = m
            acc_ref[hh] = acc

    m_ref[...] = jnp.full_like(m_ref, NEG_BIG)
    acc_ref[...] = jnp.zeros_like(acc_ref)

    bufs = ((s0_ref, mx0_ref), (s1_ref, mx1_ref))

    @pl.when(qi == 0)
    def _():
        scores(q_ref, 0, bufs[0])

    def step(first, odd):
        cur, oth = bufs[first], bufs[1 - first]

        def pair(jj, c):
            j = 2 * jj
            scores(q_ref, j + 1, oth)
            consume(j, cur)
            scores(q_ref, j + 2, cur)
            consume(j + 1, oth)
            return c

        lax.fori_loop(0, qi // 2, pair, 0)
        key_chunk = lax.broadcasted_iota(jnp.int32, (ATT_K, ATT_Q), 0) // CHUNK
        qry_chunk = lax.broadcasted_iota(jnp.int32, (ATT_K, ATT_Q), 1) // CHUNK
        mask = key_chunk <= qry_chunk
        if odd:
            scores(q_ref, qi, oth)
            consume(qi - 1, cur)
            scores(qn_ref, 0, cur)
            consume(qi, oth, mask)
        else:
            scores(qn_ref, 0, oth)
            consume(qi, cur, mask)

    for c in range(4):
        pl.when(qi % 4 == c)(functools.partial(step, ((c + 1) // 2) % 2, c % 2 == 1))

    for hh in heads:
        inv_l = 1.0 / acc_ref[hh, MLA_DV:MLA_DV + 1, :]
        o_ref[0, :, hh * MLA_DV:(hh + 1) * MLA_DV] = (
            acc_ref[hh, 0:MLA_DV, :] * inv_l).T.astype(BF16)


def _mla_attn(qcat, kcat, vt):
    b, s, _ = qcat.shape
    assert ATT_Q == ATT_K
    n_kb = s // ATT_K
    last_q = s // ATT_Q - 1
    return pl.pallas_call(
        _mla_attn_kernel,
        name="mla_attn",
        grid=(b, HEADS // ATT_HEADS, s // ATT_Q),
        in_specs=[
            pl.BlockSpec((1, ATT_Q, ATT_HEADS * MLA_QK), lambda i, h, j: (i, j, h)),
            pl.BlockSpec((1, ATT_Q, ATT_HEADS * MLA_QK),
                         lambda i, h, j: (i, jnp.minimum(j + 1, last_q), h)),
            pl.BlockSpec((1, s, ATT_HEADS * MLA_QK), lambda i, h, j: (i, 0, h)),
            pl.BlockSpec((1, n_kb, ATT_HEADS * MLA_DVA, ATT_K), lambda i, h, j: (i, 0, h, 0)),
        ],
        out_specs=pl.BlockSpec((1, ATT_Q, ATT_HEADS * MLA_DV), lambda i, h, j: (i, j, h)),
        out_shape=jax.ShapeDtypeStruct((b, s, HEADS * MLA_DV), BF16),
        scratch_shapes=[
            pltpu.VMEM((ATT_HEADS, ATT_K, ATT_Q), F32),
            pltpu.VMEM((ATT_HEADS, ATT_K, ATT_Q), F32),
            pltpu.VMEM((ATT_HEADS, 1, ATT_Q), F32),
            pltpu.VMEM((ATT_HEADS, 1, ATT_Q), F32),
            pltpu.VMEM((ATT_HEADS, 1, ATT_Q), F32),
            pltpu.VMEM((ATT_HEADS, MLA_DVA, ATT_Q), F32),
        ],
        compiler_params=_params("parallel", "parallel", "arbitrary"),
    )(qcat, qcat, kcat, vt)


def _merge_ln_kernel(yr_ref, om_ref, h_ref, wg_ref, wro_ref, wmo_ref, wout_ref,
                     g_ref, b_ref, o_ref):
    for rows in _row_chains(h_ref.shape[0]):
        h = h_ref[rows, :]
        hb = h.astype(BF16)
        mix = (jax.nn.sigmoid(_dot_nt(hb, wg_ref[0:D_MODEL, :]))
               * _dot(yr_ref[rows, :], wro_ref[...])
               + jax.nn.sigmoid(_dot_nt(hb, wg_ref[D_MODEL:2 * D_MODEL, :]))
               * _dot(om_ref[rows, :], wmo_ref[...]))
        mixed = _dot(mix.astype(BF16), wout_ref[...])
        o_ref[rows, :] = _layer_norm(ALPHA * h + mixed, g_ref[...], b_ref[...])


def _merge_ln(yr, om, h, wg, wro, wmo, wout, g, b, riders):
    t = h.shape[0]
    cast = _CastRiders(t // FFN_TILE, riders)
    return pl.pallas_call(
        cast.wrap(_merge_ln_kernel, 9, 1),
        name="merge_ln",
        grid=(t // FFN_TILE,),
        in_specs=[
            _row_spec(FFN_TILE, RET_V),
            _row_spec(FFN_TILE, HEADS * MLA_DV),
            _row_spec(FFN_TILE, D_MODEL),
            _const_spec((2 * D_MODEL, D_MODEL)),
            _const_spec((RET_V, D_MODEL)),
            _const_spec((HEADS * MLA_DV, D_MODEL)),
            _const_spec((D_MODEL, D_MODEL)),
            _const_spec((1, D_MODEL)),
            _const_spec((1, D_MODEL)),
        ] + cast.in_specs,
        out_specs=[_row_spec(FFN_TILE, D_MODEL)] + cast.out_specs,
        out_shape=[jax.ShapeDtypeStruct((t, D_MODEL), F32)] + cast.out_shapes,
        compiler_params=_params("parallel"),
    )(yr, om, h, wg, wro, wmo, wout, g, b, *cast.arrays)


def _retention_tables():
    log_gamma = jnp.log(1.0 - 2.0 ** (-5.0 - jnp.arange(HEADS, dtype=F32)))
    idx = jnp.arange(RET_BLOCK, dtype=F32)
    chunk = jnp.arange(RET_BLOCK) // CHUNK
    visible = chunk[None, :] <= chunk[:, None]
    dist = jnp.abs(idx[:, None] - idx[None, :])
    dmask = jnp.where(visible[None], jnp.exp(log_gamma[:, None, None] * dist[None]), 0.0)
    xi = jnp.exp(log_gamma[:, None] * (idx + 1.0))
    zeta = jnp.exp(log_gamma[:, None] * (RET_BLOCK - 1.0 - idx))
    cdec = jnp.exp(log_gamma * RET_BLOCK)
    def per_token_rows(tab):
        rows = jnp.broadcast_to(tab.T[:, :, None], (RET_BLOCK, HEADS, RET_DK))
        return jnp.tile(rows.reshape(RET_BLOCK, RET_QK), (TOKEN_TILE // RET_BLOCK, 1))

    return (dmask, per_token_rows(xi), per_token_rows(zeta),
            jnp.broadcast_to(cdec[:, None, None], (HEADS, 1, RET_DV)))


def _rope_consts(half):
    invf = ROPE_BASE ** (-jnp.arange(half, dtype=F32) / half)
    invf = jnp.tile(invf, LANES // half).reshape(1, LANES)
    delta = jnp.arange(TOKEN_TILE, dtype=F32)[:, None] * invf
    return invf, jnp.cos(delta), jnp.sin(delta)


def kernel(x, p, positions, ln_g, ln_b, ffn1_w_in, ffn1_w_out, w_in, ret_gn_g, w_ret_o,
           q_norm_g, kv_norm_g, w_uq, w_ukv, w_mla_o, w_out, ffn2_w_in, ffn2_w_out,
           ple_w_gate, ple_w_proj):
    b, s, d = x.shape
    t = b * s
    h = x.reshape(t, d)
    pos_b = positions.astype(F32).reshape(t // TOKEN_TILE, 1, TOKEN_TILE)
    row = lambda v: v.reshape(1, -1)

    for i in range(ln_g.shape[0]):
        w_in_t = jnp.swapaxes(w_in[i], 0, 1)
        c0 = 2 * RET_QK + 2 * RET_V
        c1 = c0 + Q_LORA + KV_LORA + MLA_ROPE
        w_mla = jnp.pad(w_in_t[c0:c1], ((0, LANES - MLA_ROPE), (0, 0))).astype(BF16)

        q_scale = (MLA_NOPE + MLA_ROPE) ** -0.5 * math.log2(math.e)
        uq = w_uq[i].reshape(Q_LORA, HEADS, MLA_NOPE + MLA_ROPE) * q_scale
        uq_nope = uq[:, :, :MLA_NOPE].reshape(Q_LORA, HEADS * MLA_NOPE)
        uq_rope = uq[:, :, MLA_NOPE:]
        uq_rot = jnp.concatenate(
            [-uq_rope[:, :, MLA_ROPE // 2:], uq_rope[:, :, :MLA_ROPE // 2]], axis=2)
        wuq = jnp.concatenate(
            [uq_nope, uq_rope.reshape(Q_LORA, -1), uq_rot.reshape(Q_LORA, -1)], axis=1).astype(BF16)
        ukv = w_ukv[i].reshape(KV_LORA, HEADS, MLA_NOPE + MLA_DV)
        wuk = ukv[:, :, :MLA_NOPE].reshape(KV_LORA, -1).astype(BF16)
        wuvt = ukv[:, :, MLA_NOPE:].reshape(KV_LORA, -1).T.astype(BF16)

        rope_ret = _rope_consts(RET_DK // 2)
        rope_mla = _rope_consts(MLA_ROPE // 2)

        (h, qcat, kcat, vt,
         w_ret, w_gates, w_ffn2_in, w_ro, w_mo, w_o, w_pg) = _ffn_ln_mla(
            h, pos_b, rope_mla, ffn1_w_in[i].astype(BF16), ffn1_w_out[i].astype(BF16),
            row(ln_g[i, 0]), row(ln_b[i, 0]),
            w_mla, row(q_norm_g[i]), row(kv_norm_g[i]), wuq, wuk, wuvt,
            riders=[(w_in_t, 0, c0, None), (w_in_t, c1, 2 * D_MODEL, None),
                    (ffn2_w_in[i], 0, D_MODEL, None), (w_ret_o[i], 0, RET_V, ret_gn_g[i]),
                    (w_mla_o[i], 0, HEADS * MLA_DV, None), (w_out[i], 0, D_MODEL, None),
                    (ple_w_gate[i], 0, D_MODEL, None)])

        dmask, xi, zeta, cdec = _retention_tables()
        y_ret = _retention(h, pos_b, rope_ret, xi, zeta, dmask, cdec, w_ret, s)
        o_mla = _mla_attn(qcat.reshape(b, s, -1), kcat.reshape(b, s, -1),
                          vt.reshape(b, s // ATT_K, HEADS * MLA_DVA, ATT_K))
        h, w_ffn2_out, w_pp = _merge_ln(
            y_ret, o_mla.reshape(t, -1), h, w_gates, w_ro, w_mo, w_o,
            row(ln_g[i, 1]), row(ln_b[i, 1]),
            riders=[(ffn2_w_out[i], 0, D_FF, None), (ple_w_proj[i], 0, D_PLE, None)])

        h = _ffn_ple_ln(h, p[i].reshape(t, -1), w_ffn2_in, w_ffn2_out, w_pg, w_pp,
                        row(ln_g[i, 2]), row(ln_b[i, 2]), row(ln_g[i, 3]), row(ln_b[i, 3]))
    return h.reshape(b, s, d)
```

```python
import functools
import math

import jax
import jax.numpy as jnp
from jax import lax
from jax.experimental import pallas as pl
from jax.experimental.pallas import tpu as pltpu

F32 = jnp.float32
BF16 = jnp.bfloat16

D_MODEL = 1024
CHUNK = 64
D_PLE = 256
D_FF = 2816
HEADS = 8
RET_DK = 128
RET_DV = 256
RET_QK = HEADS * RET_DK
RET_V = HEADS * RET_DV
MLA_NOPE = 128
MLA_ROPE = 64
MLA_DV = 128
Q_LORA = 256
KV_LORA = 256
ROPE_BASE = 10000.0
EPS = 1e-5
ALPHA = 2.0 ** 0.25

LANES = 128
FF_CHUNK = 256
TOKEN_TILE = 512
FFN_TILE = 1024
FFN_CHAIN = 512
RET_BLOCK = 256
ATT_Q = 512
ATT_K = 512
ATT_HEADS = 2
MLA_QK = 256
BF16_ROWS = 16
MLA_DVA = MLA_DV + BF16_ROWS
VMEM_LIMIT = 52 * 1024 * 1024
NEG_BIG = -1e30


def _const_spec(shape):
    nd = len(shape)
    return pl.BlockSpec(shape, lambda *_: (0,) * nd, pipeline_mode=pl.Buffered(1))


def _row_spec(tile, width):
    return pl.BlockSpec((tile, width), lambda i: (i, 0))


def _params(*sem):
    return pltpu.CompilerParams(dimension_semantics=sem, vmem_limit_bytes=VMEM_LIMIT)


def _dot(a, b):
    return jnp.dot(a, b, preferred_element_type=F32)


def _dot_nt(a, bt):
    return lax.dot_general(a, bt, (((1,), (1,)), ((), ())), preferred_element_type=F32)


def _layer_norm(y, g, b):
    mu = jnp.mean(y, axis=-1, keepdims=True)
    d = y - mu
    var = jnp.mean(d * d, axis=-1, keepdims=True)
    return d * lax.rsqrt(var + EPS) * g + b


def _rms_norm(x, g):
    return x * lax.rsqrt(jnp.mean(x * x, axis=-1, keepdims=True) + EPS) * g


def _swiglu(xb, w_in_ref, w_out_ref, act_ref):
    for c in range(D_FF // FF_CHUNK):
        lo = c * FF_CHUNK
        g = _dot(xb, w_in_ref[:, lo:lo + FF_CHUNK])
        u = _dot(xb, w_in_ref[:, D_FF + lo:D_FF + lo + FF_CHUNK])
        act_ref[:, lo:lo + FF_CHUNK] = (g * jax.nn.sigmoid(g) * u).astype(BF16)
    return _dot(act_ref[...], w_out_ref[...])


def _row_chains(tile):
    return [slice(r, r + FFN_CHAIN) for r in range(0, tile, FFN_CHAIN)]


def _ffn_ln_mla_kernel(x_ref, pos_ref, invf_ref, dcos_ref, dsin_ref, w_in_ref, w_out_ref,
                       g_ref, b_ref, wlat_ref, qg_ref, kvg_ref, wuq_ref, wuk_ref, wuvt_ref,
                       o_ref, qcat_ref, kcat_ref, vt_ref, act_ref, cos_ref, sin_ref):
    rope_ok = _rope_tables(pos_ref, invf_ref, dcos_ref, dsin_ref, cos_ref, sin_ref)
    _rope_tables_direct(rope_ok, pos_ref, invf_ref, cos_ref, sin_ref)
    x = x_ref[...]
    f = _swiglu(x.astype(BF16), w_in_ref, w_out_ref, act_ref)
    h = _layer_norm(ALPHA * x + 0.5 * f, g_ref[...], b_ref[...])
    o_ref[...] = h
    _mla_proj(h.astype(BF16), wlat_ref, qg_ref, kvg_ref, wuq_ref, wuk_ref, wuvt_ref,
              qcat_ref, kcat_ref, vt_ref, cos_ref, sin_ref)


class _CastRiders:
    def __init__(self, steps, riders):
        self.arrays, self.scaled = [], []
        self.in_specs, self.out_specs, self.out_shapes = [], [], []
        for a, first, count, scale in riders:
            rows, cols = count // steps, a.shape[1]
            assert rows * steps == count and rows % BF16_ROWS == 0 and first % rows == 0
            self.arrays.append(a)
            self.in_specs.append(
                pl.BlockSpec((rows, cols), lambda i, o=first // rows: (i + o, 0)))
            self.scaled.append(scale is not None)
            if scale is not None:
                self.arrays.append(scale.reshape(count, 1))
                self.in_specs.append(pl.BlockSpec((rows, 1), lambda i: (i, 0)))
            self.out_specs.append(pl.BlockSpec((rows, cols), lambda i: (i, 0)))
            self.out_shapes.append(jax.ShapeDtypeStruct((count, cols), BF16))

    def wrap(self, body, n_in, n_out):
        n_src, n_dst = len(self.arrays), len(self.scaled)

        def kern(*refs):
            ins, rin = refs[:n_in], list(refs[n_in:n_in + n_src])
            outs = refs[n_in + n_src:n_in + n_src + n_out]
            rout = refs[n_in + n_src + n_out:n_in + n_src + n_out + n_dst]
            for dst, scaled in zip(rout, self.scaled):
                src = rin.pop(0)[...]
                if scaled:
                    src = src * rin.pop(0)[...]
                dst[...] = src.astype(BF16)
            body(*ins, *outs, *refs[n_in + n_src + n_out + n_dst:])
        return kern


def _ffn_ln_mla(x, pos_b, rope, w_in, w_out, g, b, wlat, qg, kvg, wuq, wuk, wuvt, riders):
    t = x.shape[0]
    uq_cols = HEADS * (MLA_NOPE + 2 * MLA_ROPE)
    assert TOKEN_TILE == ATT_K
    cast = _CastRiders(t // TOKEN_TILE, riders)
    return pl.pallas_call(
        cast.wrap(_ffn_ln_mla_kernel, 15, 4),
        name="ffn_ln_mla",
        grid=(t // TOKEN_TILE,),
        in_specs=[
            _row_spec(TOKEN_TILE, D_MODEL),
            pl.BlockSpec((1, 1, TOKEN_TILE), lambda i: (i, 0, 0)),
            _const_spec((1, LANES)),
            _const_spec((TOKEN_TILE, LANES)),
            _const_spec((TOKEN_TILE, LANES)),
            _const_spec((D_MODEL, 2 * D_FF)),
            _const_spec((D_FF, D_MODEL)),
            _const_spec((1, D_MODEL)),
            _const_spec((1, D_MODEL)),
            _const_spec((MLA_LAT, D_MODEL)),
            _const_spec((1, Q_LORA)),
            _const_spec((1, KV_LORA)),
            _const_spec((uq_cols, Q_LORA)),
            _const_spec((KV_LORA, HEADS * MLA_NOPE)),
            _const_spec((HEADS * MLA_DV, KV_LORA)),
        ] + cast.in_specs,
        out_specs=[
            _row_spec(TOKEN_TILE, D_MODEL),
            pl.BlockSpec((1, HEADS * MLA_QK, TOKEN_TILE), lambda i: (i, 0, 0)),
            _row_spec(TOKEN_TILE, HEADS * MLA_QK),
            pl.BlockSpec((1, HEADS * MLA_DVA, TOKEN_TILE), lambda i: (i, 0, 0)),
        ] + cast.out_specs,
        out_shape=[
            jax.ShapeDtypeStruct((t, D_MODEL), F32),
            jax.ShapeDtypeStruct((t // TOKEN_TILE, HEADS * MLA_QK, TOKEN_TILE), BF16),
            jax.ShapeDtypeStruct((t, HEADS * MLA_QK), BF16),
            jax.ShapeDtypeStruct((t // TOKEN_TILE, HEADS * MLA_DVA, TOKEN_TILE), BF16),
        ] + cast.out_shapes,
        scratch_shapes=[pltpu.VMEM((TOKEN_TILE, D_FF), BF16)]
        + [pltpu.VMEM((TOKEN_TILE, LANES), F32)] * 2,
        compiler_params=_params("parallel"),
    )(x, pos_b, *rope, w_in, w_out, g, b, wlat, qg, kvg, wuq, wuk, wuvt, *cast.arrays)


def _ffn_ple_ln_kernel(h_ref, p_ref, w_in_ref, w_out_ref, wg_ref, wp_ref,
                       g2_ref, b2_ref, g3_ref, b3_ref, o_ref, act_ref):
    chains = _row_chains(h_ref.shape[0])
    h2 = []
    for c, rows in enumerate(chains):
        h = h_ref[rows, :]
        f = _swiglu(h.astype(BF16), w_in_ref, w_out_ref, act_ref.at[c])
        h2.append(_layer_norm(ALPHA * h + 0.5 * f, g2_ref[...], b2_ref[...]))
    for c, rows in enumerate(chains):
        gate = jax.nn.sigmoid(_dot(h2[c].astype(BF16), wg_ref[...]))
        proj = _dot(p_ref[rows, :].astype(BF16), wp_ref[...])
        o_ref[rows, :] = _layer_norm(ALPHA * h2[c] + gate * proj, g3_ref[...], b3_ref[...])


def _ffn_ple_ln(h, p, w_in, w_out, wg, wp, g2, b2, g3, b3):
    t = h.shape[0]
    return pl.pallas_call(
        _ffn_ple_ln_kernel,
        name="ffn_ple_ln",
        grid=(t // FFN_TILE,),
        in_specs=[
            _row_spec(FFN_TILE, D_MODEL),
            _row_spec(FFN_TILE, D_PLE),
            _const_spec((D_MODEL, 2 * D_FF)),
            _const_spec((D_FF, D_MODEL)),
            _const_spec((D_MODEL, D_MODEL)),
            _const_spec((D_PLE, D_MODEL)),
            _const_spec((1, D_MODEL)),
            _const_spec((1, D_MODEL)),
            _const_spec((1, D_MODEL)),
            _const_spec((1, D_MODEL)),
        ],
        out_specs=_row_spec(FFN_TILE, D_MODEL),
        out_shape=jax.ShapeDtypeStruct((t, D_MODEL), F32),
        scratch_shapes=[pltpu.VMEM((FFN_TILE // FFN_CHAIN, FFN_CHAIN, D_FF), BF16)],
        compiler_params=_params("parallel"),
    )(h, p, w_in, w_out, wg, wp, g2, b2, g3, b3)


def _rope_tables(pos_ref, invf_ref, dcos_ref, dsin_ref, cos_ref, sin_ref):
    pos = pos_ref[0]
    base = pos[:, 0:1]
    offs = lax.broadcasted_iota(jnp.int32, pos.shape, 1).astype(F32)
    exact_f32_int = 2.0 ** 24 - pos.shape[1]
    consecutive = jnp.logical_and(jnp.all(pos == base + offs),
                                  jnp.all(jnp.abs(base) < exact_f32_int))
    a = base * invf_ref[...]
    ca = jnp.cos(a)
    sa = jnp.sin(a)
    cos_ref[...] = ca * dcos_ref[...] - sa * dsin_ref[...]
    sin_ref[...] = sa * dcos_ref[...] + ca * dsin_ref[...]
    return consecutive


def _rope_tables_direct(valid, pos_ref, invf_ref, cos_ref, sin_ref):
    @pl.when(jnp.logical_not(valid))
    def _():
        pos = jnp.broadcast_to(pos_ref[0], (LANES, pos_ref.shape[2])).T
        ang = pos * invf_ref[...]
        cos_ref[...] = jnp.cos(ang)
        sin_ref[...] = jnp.sin(ang)


def _retention_kernel(h_ref, pos_ref, invf_ref, dcos_ref, dsin_ref, xi_ref, zeta_ref, dmask_ref,
                      cdec_ref, w_ref, o_ref, rq_ref, rqx_ref, rk_ref, rkz_ref, rv_ref, sg_ref,
                      state_ref, cos_ref, sin_ref, *, tiles_per_seq):
    @pl.when(pl.program_id(0) % tiles_per_seq == 0)
    def _():
        state_ref[...] = jnp.zeros_like(state_ref)

    rope_ok = _rope_tables(pos_ref, invf_ref, dcos_ref, dsin_ref, cos_ref, sin_ref)
    _rope_tables_direct(rope_ok, pos_ref, invf_ref, cos_ref, sin_ref)
    hb = h_ref[...].astype(BF16)
    cos = cos_ref[...]
    lane = lax.broadcasted_iota(jnp.int32, (1, LANES), 1)
    sin = sin_ref[...] * jnp.where(lane < RET_DK // 2, -1.0, 1.0)
    k_scale = RET_DK ** -0.5
    g = _dot_nt(hb, w_ref[2 * RET_QK + RET_V:2 * RET_QK + 2 * RET_V, :])
    sg_ref[...] = (g * jax.nn.sigmoid(g)).astype(BF16)
    q = _dot_nt(hb, w_ref[0:RET_QK, :])
    k = _dot_nt(hb, w_ref[RET_QK:2 * RET_QK, :])
    for h in range(HEADS):
        sl = slice(h * RET_DK, (h + 1) * RET_DK)
        tq = q[:, sl]
        tq = tq * cos + pltpu.roll(tq, RET_DK // 2, 1) * sin
        rq_ref[:, sl] = tq.astype(BF16)
        rqx_ref[:, sl] = (tq * xi_ref[:, sl]).astype(BF16)
        tk = k[:, sl]
        tk = (tk * cos + pltpu.roll(tk, RET_DK // 2, 1) * sin) * k_scale
        rk_ref[:, sl] = tk.astype(BF16)
        rkz_ref[:, sl] = (tk * zeta_ref[:, sl]).astype(BF16)
    rv_ref[...] = _dot_nt(hb, w_ref[2 * RET_QK:2 * RET_QK + RET_V, :]).astype(BF16)

    def qk(rows, h):
        qs = slice(h * RET_DK, (h + 1) * RET_DK)
        return lax.dot_general(rq_ref[rows, qs], rk_ref[rows, qs], (((1,), (1,)), ((), ())),
                               preferred_element_type=F32)

    for blk in range(TOKEN_TILE // RET_BLOCK):
        rows = slice(blk * RET_BLOCK, (blk + 1) * RET_BLOCK)
        s_next = qk(rows, 0)
        for h in range(HEADS):
            qs = slice(h * RET_DK, (h + 1) * RET_DK)
            vs = slice(h * RET_DV, (h + 1) * RET_DV)
            v = rv_ref[rows, vs]
            state = state_ref[h]
            s = s_next
            if h + 1 < HEADS:
                s_next = qk(rows, h + 1)
            cross = _dot(rqx_ref[rows, qs], state.astype(BF16))
            upd = lax.dot_general(rkz_ref[rows, qs], v, (((0,), (0,)), ((), ())),
                                  preferred_element_type=F32)
            state_ref[h] = state * cdec_ref[h] + upd
            y = _dot((s * dmask_ref[h]).astype(BF16), v) + cross
            mu = jnp.mean(y, axis=-1, keepdims=True)
            d = y - mu
            var = jnp.mean(d * d, axis=-1, keepdims=True)
            o_ref[rows, vs] = (d * lax.rsqrt(var + EPS) * sg_ref[rows, vs].astype(F32)).astype(BF16)


def _retention(h, pos_b, rope, xi, zeta, dmask, cdec, w, seq_len):
    t = h.shape[0]
    return pl.pallas_call(
        functools.partial(_retention_kernel, tiles_per_seq=seq_len // TOKEN_TILE),
        name="retention",
        grid=(t // TOKEN_TILE,),
        in_specs=[
            _row_spec(TOKEN_TILE, D_MODEL),
            pl.BlockSpec((1, 1, TOKEN_TILE), lambda i: (i, 0, 0)),
            _const_spec((1, LANES)),
            _const_spec((TOKEN_TILE, LANES)),
            _const_spec((TOKEN_TILE, LANES)),
            _const_spec((TOKEN_TILE, RET_QK)),
            _const_spec((TOKEN_TILE, RET_QK)),
            _const_spec((HEADS, RET_BLOCK, RET_BLOCK)),
            _const_spec((HEADS, 1, RET_DV)),
            _const_spec((2 * RET_QK + 2 * RET_V, D_MODEL)),
        ],
        out_specs=_row_spec(TOKEN_TILE, RET_V),
        out_shape=jax.ShapeDtypeStruct((t, RET_V), BF16),
        scratch_shapes=[pltpu.VMEM((TOKEN_TILE, RET_QK), BF16)] * 4
        + [pltpu.VMEM((TOKEN_TILE, RET_V), BF16)] * 2
        + [pltpu.VMEM((HEADS, RET_DK, RET_DV), F32)]
        + [pltpu.VMEM((TOKEN_TILE, LANES), F32)] * 2,
        compiler_params=_params("arbitrary"),
    )(h, pos_b, *rope, xi, zeta, dmask, cdec, w)


MLA_LAT = Q_LORA + KV_LORA + LANES


def _mla_proj(hb, w_ref, qg_ref, kvg_ref, wuq_ref, wuk_ref, wuvt_ref,
              qcat_ref, kcat_ref, vt_ref, cos_ref, sin_ref):
    lane = lax.broadcasted_iota(jnp.int32, (1, LANES), 1)
    first_half = lane < MLA_ROPE

    lat = _dot_nt(hb, w_ref[...])
    cq = lat[:, 0:Q_LORA]
    ckv = lat[:, Q_LORA:Q_LORA + KV_LORA]
    kpe = lat[:, Q_LORA + KV_LORA:MLA_LAT]
    kpe = kpe + pltpu.roll(kpe, MLA_ROPE, 1)
    kpe_rot = pltpu.roll(kpe, MLA_ROPE // 2, 1) * jnp.where(
        lane % MLA_ROPE < MLA_ROPE // 2, -1.0, 1.0)

    qft = _dot(wuq_ref[...], _rms_norm(cq, qg_ref[...]).T.astype(BF16))
    nope_w = HEADS * MLA_NOPE
    rope_w = HEADS * MLA_ROPE
    for h in range(HEADS):
        qcat_ref[0, h * MLA_QK:h * MLA_QK + MLA_NOPE, :] = (
            qft[h * MLA_NOPE:(h + 1) * MLA_NOPE, :].astype(BF16))

    ckvn = _rms_norm(ckv, kvg_ref[...])
    kv = _dot(ckvn.astype(BF16), wuk_ref[...])
    vt = _dot(wuvt_ref[...], ckvn.T.astype(BF16)).astype(BF16)
    ones = jnp.ones((MLA_DVA - MLA_DV, vt.shape[1]), BF16)
    for h in range(HEADS):
        vt_ref[0, h * MLA_DVA:h * MLA_DVA + MLA_DV, :] = vt[h * MLA_DV:(h + 1) * MLA_DV, :]
        vt_ref[0, h * MLA_DVA + MLA_DV:(h + 1) * MLA_DVA, :] = ones
        kcat_ref[:, h * MLA_QK:h * MLA_QK + MLA_NOPE] = (
            kv[:, h * MLA_NOPE:(h + 1) * MLA_NOPE].astype(BF16))

    cos = cos_ref[...]
    sin = sin_ref[...]
    cos_t = cos.T
    sin_t = sin.T
    zeros = jnp.zeros((MLA_ROPE, cos_t.shape[1]), BF16)
    for j in range(HEADS // 2):
        r = (qft[nope_w + j * LANES:nope_w + (j + 1) * LANES, :] * cos_t
             + qft[nope_w + rope_w + j * LANES:nope_w + rope_w + (j + 1) * LANES, :] * sin_t
             ).astype(BF16)
        base = 2 * j * MLA_QK + MLA_NOPE
        qcat_ref[0, base:base + MLA_ROPE, :] = r[0:MLA_ROPE, :]
        qcat_ref[0, base + MLA_ROPE:base + LANES, :] = zeros
        base += MLA_QK
        qcat_ref[0, base:base + MLA_ROPE, :] = zeros
        qcat_ref[0, base + MLA_ROPE:base + LANES, :] = r[MLA_ROPE:LANES, :]
    kr = kpe * cos + kpe_rot * sin
    kr_even = jnp.where(first_half, kr, 0.0).astype(BF16)
    kr_odd = jnp.where(first_half, 0.0, kr).astype(BF16)
    for h in range(HEADS):
        kcat_ref[:, h * MLA_QK + MLA_NOPE:(h + 1) * MLA_QK] = kr_even if h % 2 == 0 else kr_odd


def _softmax_step(st, mx, vt, stats):
    m, acc = stats
    m_new = jnp.maximum(m, mx)
    a = jnp.exp2(m - m_new)
    p = jnp.exp2(st - m_new)
    acc = a * acc + _dot(vt, p.astype(BF16))
    return m_new, acc


def _mla_attn_kernel(q_ref, qn_ref, k_ref, vt_ref, o_ref, s0_ref, s1_ref, mx0_ref, mx1_ref,
                     m_ref, acc_ref):
    qi = pl.program_id(2)
    heads = range(ATT_HEADS)

    def scores(qr, j, buf):
        s_ref, mx_ref = buf
        off = pl.multiple_of(j * ATT_K, ATT_K)
        for hh in heads:
            st = _dot(k_ref[0, pl.ds(off, ATT_K), hh * MLA_QK:(hh + 1) * MLA_QK],
                      qr[0, 0, hh * MLA_QK:(hh + 1) * MLA_QK, :])
            s_ref[hh] = st
            mx_ref[hh] = jnp.max(st, axis=0, keepdims=True)

    def consume(j, buf, mask=None):
        s_ref, mx_ref = buf
        for hh in heads:
            st = s_ref[hh]
            if mask is None:
                mx = mx_ref[hh]
            else:
                st = jnp.where(mask, st, -jnp.inf)
                mx = jnp.max(st, axis=0, keepdims=True)
            m, acc = _softmax_step(st, mx, vt_ref[0, j, hh * MLA_DVA:(hh + 1) * MLA_DVA, :],
                                   (m_ref[hh], acc_ref[hh]))
            m_ref[hh] = m
            acc_ref[hh] = acc

    m_ref[...] = jnp.full_like(m_ref, NEG_BIG)
    acc_ref[...] = jnp.zeros_like(acc_ref)

    bufs = ((s0_ref, mx0_ref), (s1_ref, mx1_ref))

    @pl.when(qi == 0)
    def _():
        scores(q_ref, 0, bufs[0])

    def step(first, odd):
        cur, oth = bufs[first], bufs[1 - first]

        def pair(jj, c):
            j = 2 * jj
            scores(q_ref, j + 1, oth)
            consume(j, cur)
            scores(q_ref, j + 2, cur)
            consume(j + 1, oth)
            return c

        lax.fori_loop(0, qi // 2, pair, 0)
        key_chunk = lax.broadcasted_iota(jnp.int32, (ATT_K, ATT_Q), 0) // CHUNK
        qry_chunk = lax.broadcasted_iota(jnp.int32, (ATT_K, ATT_Q), 1) // CHUNK
        mask = key_chunk <= qry_chunk
        if odd:
            scores(q_ref, qi, oth)
            consume(qi - 1, cur)
            scores(qn_ref, 0, cur)
            consume(qi, oth, mask)
        else:
            scores(qn_ref, 0, oth)
            consume(qi, cur, mask)

    for c in range(4):
        pl.when(qi % 4 == c)(functools.partial(step, ((c + 1) // 2) % 2, c % 2 == 1))

    for hh in heads:
        inv_l = 1.0 / acc_ref[hh, MLA_DV:MLA_DV + 1, :]
        o_ref[0, :, hh * MLA_DV:(hh + 1) * MLA_DV] = (
            acc_ref[hh, 0:MLA_DV, :] * inv_l).T.astype(BF16)


def _mla_attn(qcat_t, kcat, vt):
    b, s, _ = kcat.shape
    assert ATT_Q == ATT_K == TOKEN_TILE
    n_kb = s // ATT_K
    last_q = s // ATT_Q - 1
    q_block = (1, 1, ATT_HEADS * MLA_QK, ATT_Q)
    return pl.pallas_call(
        _mla_attn_kernel,
        name="mla_attn",
        grid=(b, HEADS // ATT_HEADS, s // ATT_Q),
        in_specs=[
            pl.BlockSpec(q_block, lambda i, h, j: (i, j, h, 0)),
            pl.BlockSpec(q_block, lambda i, h, j: (i, jnp.minimum(j + 1, last_q), h, 0)),
            pl.BlockSpec((1, s, ATT_HEADS * MLA_QK), lambda i, h, j: (i, 0, h)),
            pl.BlockSpec((1, n_kb, ATT_HEADS * MLA_DVA, ATT_K), lambda i, h, j: (i, 0, h, 0)),
        ],
        out_specs=pl.BlockSpec((1, ATT_Q, ATT_HEADS * MLA_DV), lambda i, h, j: (i, j, h)),
        out_shape=jax.ShapeDtypeStruct((b, s, HEADS * MLA_DV), BF16),
        scratch_shapes=[
            pltpu.VMEM((ATT_HEADS, ATT_K, ATT_Q), F32),
            pltpu.VMEM((ATT_HEADS, ATT_K, ATT_Q), F32),
            pltpu.VMEM((ATT_HEADS, 1, ATT_Q), F32),
            pltpu.VMEM((ATT_HEADS, 1, ATT_Q), F32),
            pltpu.VMEM((ATT_HEADS, 1, ATT_Q), F32),
            pltpu.VMEM((ATT_HEADS, MLA_DVA, ATT_Q), F32),
        ],
        compiler_params=_params("parallel", "parallel", "arbitrary"),
    )(qcat_t, qcat_t, kcat, vt)


def _merge_ln_kernel(yr_ref, om_ref, h_ref, wg_ref, wro_ref, wmo_ref, wout_ref,
                     g_ref, b_ref, o_ref):
    for rows in _row_chains(h_ref.shape[0]):
        h = h_ref[rows, :]
        hb = h.astype(BF16)
        mix = (jax.nn.sigmoid(_dot_nt(hb, wg_ref[0:D_MODEL, :]))
               * _dot(yr_ref[rows, :], wro_ref[...])
               + jax.nn.sigmoid(_dot_nt(hb, wg_ref[D_MODEL:2 * D_MODEL, :]))
               * _dot(om_ref[rows, :], wmo_ref[...]))
        mixed = _dot(mix.astype(BF16), wout_ref[...])
        o_ref[rows, :] = _layer_norm(ALPHA * h + mixed, g_ref[...], b_ref[...])


def _merge_ln(yr, om, h, wg, wro, wmo, wout, g, b, riders):
    t = h.shape[0]
    cast = _CastRiders(t // FFN_TILE, riders)
    return pl.pallas_call(
        cast.wrap(_merge_ln_kernel, 9, 1),
        name="merge_ln",
        grid=(t // FFN_TILE,),
        in_specs=[
            _row_spec(FFN_TILE, RET_V),
            _row_spec(FFN_TILE, HEADS * MLA_DV),
            _row_spec(FFN_TILE, D_MODEL),
            _const_spec((2 * D_MODEL, D_MODEL)),
            _const_spec((RET_V, D_MODEL)),
            _const_spec((HEADS * MLA_DV, D_MODEL)),
            _const_spec((D_MODEL, D_MODEL)),
            _const_spec((1, D_MODEL)),
            _const_spec((1, D_MODEL)),
        ] + cast.in_specs,
        out_specs=[_row_spec(FFN_TILE, D_MODEL)] + cast.out_specs,
        out_shape=[jax.ShapeDtypeStruct((t, D_MODEL), F32)] + cast.out_shapes,
        compiler_params=_params("parallel"),
    )(yr, om, h, wg, wro, wmo, wout, g, b, *cast.arrays)


def _retention_tables():
    log_gamma = jnp.log(1.0 - 2.0 ** (-5.0 - jnp.arange(HEADS, dtype=F32)))
    idx = jnp.arange(RET_BLOCK, dtype=F32)
    chunk = jnp.arange(RET_BLOCK) // CHUNK
    visible = chunk[None, :] <= chunk[:, None]
    dist = jnp.abs(idx[:, None] - idx[None, :])
    dmask = jnp.where(visible[None], jnp.exp(log_gamma[:, None, None] * dist[None]), 0.0)
    xi = jnp.exp(log_gamma[:, None] * (idx + 1.0))
    zeta = jnp.exp(log_gamma[:, None] * (RET_BLOCK - 1.0 - idx))
    cdec = jnp.exp(log_gamma * RET_BLOCK)
    def per_token_rows(tab):
        rows = jnp.broadcast_to(tab.T[:, :, None], (RET_BLOCK, HEADS, RET_DK))
        return jnp.tile(rows.reshape(RET_BLOCK, RET_QK), (TOKEN_TILE // RET_BLOCK, 1))

    return (dmask, per_token_rows(xi), per_token_rows(zeta),
            jnp.broadcast_to(cdec[:, None, None], (HEADS, 1, RET_DV)))


def _rope_consts(half):
    invf = ROPE_BASE ** (-jnp.arange(half, dtype=F32) / half)
    invf = jnp.tile(invf, LANES // half).reshape(1, LANES)
    delta = jnp.arange(TOKEN_TILE, dtype=F32)[:, None] * invf
    return invf, jnp.cos(delta), jnp.sin(delta)


def kernel(x, p, positions, ln_g, ln_b, ffn1_w_in, ffn1_w_out, w_in, ret_gn_g, w_ret_o,
           q_norm_g, kv_norm_g, w_uq, w_ukv, w_mla_o, w_out, ffn2_w_in, ffn2_w_out,
           ple_w_gate, ple_w_proj):
    b, s, d = x.shape
    t = b * s
    h = x.reshape(t, d)
    pos_b = positions.astype(F32).reshape(t // TOKEN_TILE, 1, TOKEN_TILE)
    row = lambda v: v.reshape(1, -1)

    for i in range(ln_g.shape[0]):
        w_in_t = jnp.swapaxes(w_in[i], 0, 1)
        c0 = 2 * RET_QK + 2 * RET_V
        c1 = c0 + Q_LORA + KV_LORA + MLA_ROPE
        w_mla = jnp.pad(w_in_t[c0:c1], ((0, LANES - MLA_ROPE), (0, 0))).astype(BF16)

        q_scale = (MLA_NOPE + MLA_ROPE) ** -0.5 * math.log2(math.e)
        uq = w_uq[i].reshape(Q_LORA, HEADS, MLA_NOPE + MLA_ROPE) * q_scale
        uq_nope = uq[:, :, :MLA_NOPE].reshape(Q_LORA, HEADS * MLA_NOPE)
        uq_rope = uq[:, :, MLA_NOPE:]
        uq_rot = jnp.concatenate(
            [-uq_rope[:, :, MLA_ROPE // 2:], uq_rope[:, :, :MLA_ROPE // 2]], axis=2)
        wuq = jnp.concatenate(
            [uq_nope, uq_rope.reshape(Q_LORA, -1), uq_rot.reshape(Q_LORA, -1)], axis=1).T.astype(BF16)
        ukv = w_ukv[i].reshape(KV_LORA, HEADS, MLA_NOPE + MLA_DV)
        wuk = ukv[:, :, :MLA_NOPE].reshape(KV_LORA, -1).astype(BF16)
        wuvt = ukv[:, :, MLA_NOPE:].reshape(KV_LORA, -1).T.astype(BF16)

        rope_ret = _rope_consts(RET_DK // 2)
        rope_mla = _rope_consts(MLA_ROPE // 2)

        (h, qcat, kcat, vt,
         w_ret, w_gates, w_ffn2_in, w_ro, w_mo, w_o, w_pg) = _ffn_ln_mla(
            h, pos_b, rope_mla, ffn1_w_in[i].astype(BF16), ffn1_w_out[i].astype(BF16),
            row(ln_g[i, 0]), row(ln_b[i, 0]),
            w_mla, row(q_norm_g[i]), row(kv_norm_g[i]), wuq, wuk, wuvt,
            riders=[(w_in_t, 0, c0, None), (w_in_t, c1, 2 * D_MODEL, None),
                    (ffn2_w_in[i], 0, D_MODEL, None), (w_ret_o[i], 0, RET_V, ret_gn_g[i]),
                    (w_mla_o[i], 0, HEADS * MLA_DV, None), (w_out[i], 0, D_MODEL, None),
                    (ple_w_gate[i], 0, D_MODEL, None)])

        dmask, xi, zeta, cdec = _retention_tables()
        y_ret = _retention(h, pos_b, rope_ret, xi, zeta, dmask, cdec, w_ret, s)
        o_mla = _mla_attn(qcat.reshape(b, s // ATT_Q, HEADS * MLA_QK, ATT_Q),
                          kcat.reshape(b, s, -1),
                          vt.reshape(b, s // ATT_K, HEADS * MLA_DVA, ATT_K))
        h, w_ffn2_out, w_pp = _merge_ln(
            y_ret, o_mla.reshape(t, -1), h, w_gates, w_ro, w_mo, w_o,
            row(ln_g[i, 1]), row(ln_b[i, 1]),
            riders=[(ffn2_w_out[i], 0, D_FF, None), (ple_w_proj[i], 0, D_PLE, None)])

        h = _ffn_ple_ln(h, p[i].reshape(t, -1), w_ffn2_in, w_ffn2_out, w_pg, w_pp,
                        row(ln_g[i, 2]), row(ln_b[i, 2]), row(ln_g[i, 3]), row(ln_b[i, 3]))
    return h.reshape(b, s, d)
```

```python
import functools
import math

import jax
import jax.numpy as jnp
from jax import lax
from jax.experimental import pallas as pl
from jax.experimental.pallas import tpu as pltpu

F32 = jnp.float32
BF16 = jnp.bfloat16

D_MODEL = 1024
CHUNK = 64
D_PLE = 256
D_FF = 2816
HEADS = 8
RET_DK = 128
RET_DV = 256
RET_QK = HEADS * RET_DK
RET_V = HEADS * RET_DV
MLA_NOPE = 128
MLA_ROPE = 64
MLA_DV = 128
Q_LORA = 256
KV_LORA = 256
ROPE_BASE = 10000.0
EPS = 1e-5
ALPHA = 2.0 ** 0.25

LANES = 128
FF_CHUNK = 256
TOKEN_TILE = 512
FFN_TILE = 1024
FFN_CHAIN = 512
RET_BLOCK = 256
ATT_Q = 512
ATT_K = 512
ATT_HEADS = 2
MLA_QK = 256
BF16_ROWS = 16
MLA_DVA = MLA_DV + BF16_ROWS
VMEM_LIMIT = 52 * 1024 * 1024
NEG_BIG = -1e30


def _const_spec(shape):
    nd = len(shape)
    return pl.BlockSpec(shape, lambda *_: (0,) * nd, pipeline_mode=pl.Buffered(1))


def _row_spec(tile, width):
    return pl.BlockSpec((tile, width), lambda i: (i, 0))


def _params(*sem):
    return pltpu.CompilerParams(dimension_semantics=sem, vmem_limit_bytes=VMEM_LIMIT)


def _dot(a, b):
    return jnp.dot(a, b, preferred_element_type=F32)


def _dot_nt(a, bt):
    return lax.dot_general(a, bt, (((1,), (1,)), ((), ())), preferred_element_type=F32)


def _layer_norm(y, g, b):
    mu = jnp.mean(y, axis=-1, keepdims=True)
    d = y - mu
    var = jnp.mean(d * d, axis=-1, keepdims=True)
    return d * lax.rsqrt(var + EPS) * g + b


def _rms_norm(x, g):
    return x * lax.rsqrt(jnp.mean(x * x, axis=-1, keepdims=True) + EPS) * g


def _swiglu(xb, w_in_ref, w_out_ref, act_ref):
    for c in range(D_FF // FF_CHUNK):
        lo = c * FF_CHUNK
        g = _dot(xb, w_in_ref[:, lo:lo + FF_CHUNK])
        u = _dot(xb, w_in_ref[:, D_FF + lo:D_FF + lo + FF_CHUNK])
        act_ref[:, lo:lo + FF_CHUNK] = (g * jax.nn.sigmoid(g) * u).astype(BF16)
    return _dot(act_ref[...], w_out_ref[...])


def _row_chains(tile):
    return [slice(r, r + FFN_CHAIN) for r in range(0, tile, FFN_CHAIN)]


def _ffn_ln_mla_kernel(x_ref, pos_ref, invf_ref, dcos_ref, dsin_ref, w_in_ref, w_out_ref,
                       g_ref, b_ref, wlat_ref, qg_ref, kvg_ref, wuq_ref, wuk_ref, wuvt_ref,
                       o_ref, qcat_ref, kcat_ref, vt_ref, act_ref, cos_ref, sin_ref):
    rope_ok = _rope_tables(pos_ref, invf_ref, dcos_ref, dsin_ref, cos_ref, sin_ref)
    _rope_tables_direct(rope_ok, pos_ref, invf_ref, cos_ref, sin_ref)
    x = x_ref[...]
    f = _swiglu(x.astype(BF16), w_in_ref, w_out_ref, act_ref)
    h = _layer_norm(ALPHA * x + 0.5 * f, g_ref[...], b_ref[...])
    o_ref[...] = h
    _mla_proj(h.astype(BF16), wlat_ref, qg_ref, kvg_ref, wuq_ref, wuk_ref, wuvt_ref,
              qcat_ref, kcat_ref, vt_ref, cos_ref, sin_ref)


class _CastRiders:
    def __init__(self, steps, riders):
        self.arrays, self.kinds = [], []
        self.in_specs, self.out_specs, self.out_shapes = [], [], []
        for a, first, count, scale, t_rows in riders:
            cols = a.shape[1]
            rows = count // steps if t_rows is None else t_rows
            last = count // rows - 1
            assert count % rows == 0 and last < steps and first % rows == 0
            assert rows % (BF16_ROWS if t_rows is None else LANES) == 0
            blk = lambda i, last=last: jnp.minimum(i, last)
            self.arrays.append(a)
            self.in_specs.append(
                pl.BlockSpec((rows, cols), lambda i, o=first // rows, blk=blk: (blk(i) + o, 0)))
            self.kinds.append((scale is not None, t_rows is not None))
            if scale is not None:
                self.arrays.append(scale.reshape(count, 1))
                self.in_specs.append(pl.BlockSpec((rows, 1), lambda i, blk=blk: (blk(i), 0)))
            if t_rows is None:
                self.out_specs.append(pl.BlockSpec((rows, cols), lambda i, blk=blk: (blk(i), 0)))
                self.out_shapes.append(jax.ShapeDtypeStruct((count, cols), BF16))
            else:
                self.out_specs.append(pl.BlockSpec((cols, rows), lambda i, blk=blk: (0, blk(i))))
                self.out_shapes.append(jax.ShapeDtypeStruct((cols, count), BF16))

    def wrap(self, body, n_in, n_out):
        n_src, n_dst = len(self.arrays), len(self.kinds)

        def kern(*refs):
            ins, rin = refs[:n_in], list(refs[n_in:n_in + n_src])
            outs = refs[n_in + n_src:n_in + n_src + n_out]
            rout = refs[n_in + n_src + n_out:n_in + n_src + n_out + n_dst]
            for dst, (scaled, transposed) in zip(rout, self.kinds):
                src = rin.pop(0)[...]
                if scaled:
                    src = src * rin.pop(0)[...]
                dst[...] = (src.T if transposed else src).astype(BF16)
            body(*ins, *outs, *refs[n_in + n_src + n_out + n_dst:])
        return kern


def _ffn_ln_mla(x, pos_b, rope, w_in, w_out, g, b, wlat, qg, kvg, wuq, wuk, wuvt, riders):
    t = x.shape[0]
    uq_cols = HEADS * (MLA_NOPE + 2 * MLA_ROPE)
    assert TOKEN_TILE == ATT_K
    cast = _CastRiders(t // TOKEN_TILE, riders)
    return pl.pallas_call(
        cast.wrap(_ffn_ln_mla_kernel, 15, 4),
        name="ffn_ln_mla",
        grid=(t // TOKEN_TILE,),
        in_specs=[
            _row_spec(TOKEN_TILE, D_MODEL),
            pl.BlockSpec((1, 1, TOKEN_TILE), lambda i: (i, 0, 0)),
            _const_spec((1, LANES)),
            _const_spec((TOKEN_TILE, LANES)),
            _const_spec((TOKEN_TILE, LANES)),
            _const_spec((D_MODEL, 2 * D_FF)),
            _const_spec((D_FF, D_MODEL)),
            _const_spec((1, D_MODEL)),
            _const_spec((1, D_MODEL)),
            _const_spec((D_MODEL, MLA_LAT)),
            _const_spec((1, Q_LORA)),
            _const_spec((1, KV_LORA)),
            _const_spec((uq_cols, Q_LORA)),
            _const_spec((KV_LORA, HEADS * MLA_NOPE)),
            _const_spec((HEADS * MLA_DV, KV_LORA)),
        ] + cast.in_specs,
        out_specs=[
            _row_spec(TOKEN_TILE, D_MODEL),
            pl.BlockSpec((1, HEADS * MLA_QK, TOKEN_TILE), lambda i: (i, 0, 0)),
            _row_spec(TOKEN_TILE, HEADS * MLA_QK),
            pl.BlockSpec((1, HEADS * MLA_DVA, TOKEN_TILE), lambda i: (i, 0, 0)),
        ] + cast.out_specs,
        out_shape=[
            jax.ShapeDtypeStruct((t, D_MODEL), F32),
            jax.ShapeDtypeStruct((t // TOKEN_TILE, HEADS * MLA_QK, TOKEN_TILE), BF16),
            jax.ShapeDtypeStruct((t, HEADS * MLA_QK), BF16),
            jax.ShapeDtypeStruct((t // TOKEN_TILE, HEADS * MLA_DVA, TOKEN_TILE), BF16),
        ] + cast.out_shapes,
        scratch_shapes=[pltpu.VMEM((TOKEN_TILE, D_FF), BF16)]
        + [pltpu.VMEM((TOKEN_TILE, LANES), F32)] * 2,
        compiler_params=_params("parallel"),
    )(x, pos_b, *rope, w_in, w_out, g, b, wlat, qg, kvg, wuq, wuk, wuvt, *cast.arrays)


def _ffn_ple_ln_kernel(h_ref, p_ref, w_in_ref, w_out_ref, wg_ref, wp_ref,
                       g2_ref, b2_ref, g3_ref, b3_ref, o_ref, act_ref):
    chains = _row_chains(h_ref.shape[0])
    h2 = []
    for c, rows in enumerate(chains):
        h = h_ref[rows, :]
        f = _swiglu(h.astype(BF16), w_in_ref, w_out_ref, act_ref.at[c])
        h2.append(_layer_norm(ALPHA * h + 0.5 * f, g2_ref[...], b2_ref[...]))
    for c, rows in enumerate(chains):
        gate = jax.nn.sigmoid(_dot(h2[c].astype(BF16), wg_ref[...]))
        proj = _dot(p_ref[rows, :].astype(BF16), wp_ref[...])
        o_ref[rows, :] = _layer_norm(ALPHA * h2[c] + gate * proj, g3_ref[...], b3_ref[...])


def _ffn_ple_ln(h, p, w_in, w_out, wg, wp, g2, b2, g3, b3):
    t = h.shape[0]
    return pl.pallas_call(
        _ffn_ple_ln_kernel,
        name="ffn_ple_ln",
        grid=(t // FFN_TILE,),
        in_specs=[
            _row_spec(FFN_TILE, D_MODEL),
            _row_spec(FFN_TILE, D_PLE),
            _const_spec((D_MODEL, 2 * D_FF)),
            _const_spec((D_FF, D_MODEL)),
            _const_spec((D_MODEL, D_MODEL)),
            _const_spec((D_PLE, D_MODEL)),
            _const_spec((1, D_MODEL)),
            _const_spec((1, D_MODEL)),
            _const_spec((1, D_MODEL)),
            _const_spec((1, D_MODEL)),
        ],
        out_specs=_row_spec(FFN_TILE, D_MODEL),
        out_shape=jax.ShapeDtypeStruct((t, D_MODEL), F32),
        scratch_shapes=[pltpu.VMEM((FFN_TILE // FFN_CHAIN, FFN_CHAIN, D_FF), BF16)],
        compiler_params=_params("parallel"),
    )(h, p, w_in, w_out, wg, wp, g2, b2, g3, b3)


def _rope_tables(pos_ref, invf_ref, dcos_ref, dsin_ref, cos_ref, sin_ref):
    pos = pos_ref[0]
    base = pos[:, 0:1]
    offs = lax.broadcasted_iota(jnp.int32, pos.shape, 1).astype(F32)
    exact_f32_int = 2.0 ** 24 - pos.shape[1]
    consecutive = jnp.logical_and(jnp.all(pos == base + offs),
                                  jnp.all(jnp.abs(base) < exact_f32_int))
    a = base * invf_ref[...]
    ca = jnp.cos(a)
    sa = jnp.sin(a)
    cos_ref[...] = ca * dcos_ref[...] - sa * dsin_ref[...]
    sin_ref[...] = sa * dcos_ref[...] + ca * dsin_ref[...]
    return consecutive


def _rope_tables_direct(valid, pos_ref, invf_ref, cos_ref, sin_ref):
    @pl.when(jnp.logical_not(valid))
    def _():
        pos = jnp.broadcast_to(pos_ref[0], (LANES, pos_ref.shape[2])).T
        ang = pos * invf_ref[...]
        cos_ref[...] = jnp.cos(ang)
        sin_ref[...] = jnp.sin(ang)


def _retention_kernel(h_ref, pos_ref, invf_ref, dcos_ref, dsin_ref, xi_ref, zeta_ref, dmask_ref,
                      cdec_ref, w_ref, o_ref, rq_ref, rqx_ref, rk_ref, rkz_ref, rv_ref, sg_ref,
                      state_ref, cos_ref, sin_ref, *, tiles_per_seq):
    @pl.when(pl.program_id(0) % tiles_per_seq == 0)
    def _():
        state_ref[...] = jnp.zeros_like(state_ref)

    rope_ok = _rope_tables(pos_ref, invf_ref, dcos_ref, dsin_ref, cos_ref, sin_ref)
    _rope_tables_direct(rope_ok, pos_ref, invf_ref, cos_ref, sin_ref)
    hb = h_ref[...].astype(BF16)
    cos = cos_ref[...]
    lane = lax.broadcasted_iota(jnp.int32, (1, LANES), 1)
    sin = sin_ref[...] * jnp.where(lane < RET_DK // 2, -1.0, 1.0)
    k_scale = RET_DK ** -0.5
    g = _dot(hb, w_ref[:, 2 * RET_QK + RET_V:2 * RET_QK + 2 * RET_V])
    sg_ref[...] = (g * jax.nn.sigmoid(g)).astype(BF16)
    q = _dot(hb, w_ref[:, 0:RET_QK])
    k = _dot(hb, w_ref[:, RET_QK:2 * RET_QK])
    for h in range(HEADS):
        sl = slice(h * RET_DK, (h + 1) * RET_DK)
        tq = q[:, sl]
        tq = tq * cos + pltpu.roll(tq, RET_DK // 2, 1) * sin
        rq_ref[:, sl] = tq.astype(BF16)
        rqx_ref[:, sl] = (tq * xi_ref[:, sl]).astype(BF16)
        tk = k[:, sl]
        tk = (tk * cos + pltpu.roll(tk, RET_DK // 2, 1) * sin) * k_scale
        rk_ref[:, sl] = tk.astype(BF16)
        rkz_ref[:, sl] = (tk * zeta_ref[:, sl]).astype(BF16)
    rv_ref[...] = _dot(hb, w_ref[:, 2 * RET_QK:2 * RET_QK + RET_V]).astype(BF16)

    def qk(rows, h):
        qs = slice(h * RET_DK, (h + 1) * RET_DK)
        return lax.dot_general(rq_ref[rows, qs], rk_ref[rows, qs], (((1,), (1,)), ((), ())),
                               preferred_element_type=F32)

    for blk in range(TOKEN_TILE // RET_BLOCK):
        rows = slice(blk * RET_BLOCK, (blk + 1) * RET_BLOCK)
        s_next = qk(rows, 0)
        for h in range(HEADS):
            qs = slice(h * RET_DK, (h + 1) * RET_DK)
            vs = slice(h * RET_DV, (h + 1) * RET_DV)
            v = rv_ref[rows, vs]
            state = state_ref[h]
            s = s_next
            if h + 1 < HEADS:
                s_next = qk(rows, h + 1)
            cross = _dot(rqx_ref[rows, qs], state.astype(BF16))
            upd = lax.dot_general(rkz_ref[rows, qs], v, (((0,), (0,)), ((), ())),
                                  preferred_element_type=F32)
            state_ref[h] = state * cdec_ref[h] + upd
            y = _dot((s * dmask_ref[h]).astype(BF16), v) + cross
            mu = jnp.mean(y, axis=-1, keepdims=True)
            d = y - mu
            var = jnp.mean(d * d, axis=-1, keepdims=True)
            o_ref[rows, vs] = (d * lax.rsqrt(var + EPS) * sg_ref[rows, vs].astype(F32)).astype(BF16)


def _retention(h, pos_b, rope, xi, zeta, dmask, cdec, w, seq_len):
    t = h.shape[0]
    return pl.pallas_call(
        functools.partial(_retention_kernel, tiles_per_seq=seq_len // TOKEN_TILE),
        name="retention",
        grid=(t // TOKEN_TILE,),
        in_specs=[
            _row_spec(TOKEN_TILE, D_MODEL),
            pl.BlockSpec((1, 1, TOKEN_TILE), lambda i: (i, 0, 0)),
            _const_spec((1, LANES)),
            _const_spec((TOKEN_TILE, LANES)),
            _const_spec((TOKEN_TILE, LANES)),
            _const_spec((TOKEN_TILE, RET_QK)),
            _const_spec((TOKEN_TILE, RET_QK)),
            _const_spec((HEADS, RET_BLOCK, RET_BLOCK)),
            _const_spec((HEADS, 1, RET_DV)),
            _const_spec((D_MODEL, 2 * RET_QK + 2 * RET_V)),
        ],
        out_specs=_row_spec(TOKEN_TILE, RET_V),
        out_shape=jax.ShapeDtypeStruct((t, RET_V), BF16),
        scratch_shapes=[pltpu.VMEM((TOKEN_TILE, RET_QK), BF16)] * 4
        + [pltpu.VMEM((TOKEN_TILE, RET_V), BF16)] * 2
        + [pltpu.VMEM((HEADS, RET_DK, RET_DV), F32)]
        + [pltpu.VMEM((TOKEN_TILE, LANES), F32)] * 2,
        compiler_params=_params("arbitrary"),
    )(h, pos_b, *rope, xi, zeta, dmask, cdec, w)


MLA_LAT = Q_LORA + KV_LORA + LANES


def _mla_proj(hb, w_ref, qg_ref, kvg_ref, wuq_ref, wuk_ref, wuvt_ref,
              qcat_ref, kcat_ref, vt_ref, cos_ref, sin_ref):
    lane = lax.broadcasted_iota(jnp.int32, (1, LANES), 1)
    first_half = lane < MLA_ROPE

    lat = _dot(hb, w_ref[...])
    cq = lat[:, 0:Q_LORA]
    ckv = lat[:, Q_LORA:Q_LORA + KV_LORA]
    kpe = lat[:, Q_LORA + KV_LORA:MLA_LAT]
    kpe = kpe + pltpu.roll(kpe, MLA_ROPE, 1)
    kpe_rot = pltpu.roll(kpe, MLA_ROPE // 2, 1) * jnp.where(
        lane % MLA_ROPE < MLA_ROPE // 2, -1.0, 1.0)

    qft = _dot(wuq_ref[...], _rms_norm(cq, qg_ref[...]).T.astype(BF16))
    nope_w = HEADS * MLA_NOPE
    rope_w = HEADS * MLA_ROPE
    for h in range(HEADS):
        qcat_ref[0, h * MLA_QK:h * MLA_QK + MLA_NOPE, :] = (
            qft[h * MLA_NOPE:(h + 1) * MLA_NOPE, :].astype(BF16))

    ckvn = _rms_norm(ckv, kvg_ref[...])
    kv = _dot(ckvn.astype(BF16), wuk_ref[...])
    vt = _dot(wuvt_ref[...], ckvn.T.astype(BF16)).astype(BF16)
    ones = jnp.ones((MLA_DVA - MLA_DV, vt.shape[1]), BF16)
    for h in range(HEADS):
        vt_ref[0, h * MLA_DVA:h * MLA_DVA + MLA_DV, :] = vt[h * MLA_DV:(h + 1) * MLA_DV, :]
        vt_ref[0, h * MLA_DVA + MLA_DV:(h + 1) * MLA_DVA, :] = ones
        kcat_ref[:, h * MLA_QK:h * MLA_QK + MLA_NOPE] = (
            kv[:, h * MLA_NOPE:(h + 1) * MLA_NOPE].astype(BF16))

    cos = cos_ref[...]
    sin = sin_ref[...]
    cos_t = cos.T
    sin_t = sin.T
    zeros = jnp.zeros((MLA_ROPE, cos_t.shape[1]), BF16)
    for j in range(HEADS // 2):
        r = (qft[nope_w + j * LANES:nope_w + (j + 1) * LANES, :] * cos_t
             + qft[nope_w + rope_w + j * LANES:nope_w + rope_w + (j + 1) * LANES, :] * sin_t
             ).astype(BF16)
        base = 2 * j * MLA_QK + MLA_NOPE
        qcat_ref[0, base:base + MLA_ROPE, :] = r[0:MLA_ROPE, :]
        qcat_ref[0, base + MLA_ROPE:base + LANES, :] = zeros
        base += MLA_QK
        qcat_ref[0, base:base + MLA_ROPE, :] = zeros
        qcat_ref[0, base + MLA_ROPE:base + LANES, :] = r[MLA_ROPE:LANES, :]
    kr = kpe * cos + kpe_rot * sin
    kr_even = jnp.where(first_half, kr, 0.0).astype(BF16)
    kr_odd = jnp.where(first_half, 0.0, kr).astype(BF16)
    for h in range(HEADS):
        kcat_ref[:, h * MLA_QK + MLA_NOPE:(h + 1) * MLA_QK] = kr_even if h % 2 == 0 else kr_odd


def _softmax_step(st, mx, vt, stats):
    m, acc = stats
    m_new = jnp.maximum(m, mx)
    a = jnp.exp2(m - m_new)
    p = jnp.exp2(st - m_new)
    acc = a * acc + _dot(vt, p.astype(BF16))
    return m_new, acc


def _mla_attn_kernel(q_ref, qn_ref, k_ref, vt_ref, o_ref, s0_ref, s1_ref, mx0_ref, mx1_ref,
                     m_ref, acc_ref):
    qi = pl.program_id(2)
    heads = range(ATT_HEADS)

    def scores(qr, j, buf):
        s_ref, mx_ref = buf
        off = pl.multiple_of(j * ATT_K, ATT_K)
        for hh in heads:
            st = _dot(k_ref[0, pl.ds(off, ATT_K), hh * MLA_QK:(hh + 1) * MLA_QK],
                      qr[0, 0, hh * MLA_QK:(hh + 1) * MLA_QK, :])
            s_ref[hh] = st
            mx_ref[hh] = jnp.max(st, axis=0, keepdims=True)

    def consume(j, buf, mask=None):
        s_ref, mx_ref = buf
        for hh in heads:
            st = s_ref[hh]
            if mask is None:
                mx = mx_ref[hh]
            else:
                st = jnp.where(mask, st, -jnp.inf)
                mx = jnp.max(st, axis=0, keepdims=True)
            m, acc = _softmax_step(st, mx, vt_ref[0, j, hh * MLA_DVA:(hh + 1) * MLA_DVA, :],
                                   (m_ref[hh], acc_ref[hh]))
            m_ref[hh] = m
            acc_ref[hh] = acc

    m_ref[...] = jnp.full_like(m_ref, NEG_BIG)
    acc_ref[...] = jnp.zeros_like(acc_ref)

    bufs = ((s0_ref, mx0_ref), (s1_ref, mx1_ref))

    @pl.when(qi == 0)
    def _():
        scores(q_ref, 0, bufs[0])

    def step(first, odd):
        cur, oth = bufs[first], bufs[1 - first]

        def pair(jj, c):
            j = 2 * jj
            scores(q_ref, j + 1, oth)
            consume(j, cur)
            scores(q_ref, j + 2, cur)
            consume(j + 1, oth)
            return c

        lax.fori_loop(0, qi // 2, pair, 0)
        key_chunk = lax.broadcasted_iota(jnp.int32, (ATT_K, ATT_Q), 0) // CHUNK
        qry_chunk = lax.broadcasted_iota(jnp.int32, (ATT_K, ATT_Q), 1) // CHUNK
        mask = key_chunk <= qry_chunk
        if odd:
            scores(q_ref, qi, oth)
            consume(qi - 1, cur)
            scores(qn_ref, 0, cur)
            consume(qi, oth, mask)
        else:
            scores(qn_ref, 0, oth)
            consume(qi, cur, mask)

    for c in range(4):
        pl.when(qi % 4 == c)(functools.partial(step, ((c + 1) // 2) % 2, c % 2 == 1))

    for hh in heads:
        inv_l = 1.0 / acc_ref[hh, MLA_DV:MLA_DV + 1, :]
        o_ref[0, :, hh * MLA_DV:(hh + 1) * MLA_DV] = (
            acc_ref[hh, 0:MLA_DV, :] * inv_l).T.astype(BF16)


def _mla_attn(qcat_t, kcat, vt):
    b, s, _ = kcat.shape
    assert ATT_Q == ATT_K == TOKEN_TILE
    n_kb = s // ATT_K
    last_q = s // ATT_Q - 1
    q_block = (1, 1, ATT_HEADS * MLA_QK, ATT_Q)
    return pl.pallas_call(
        _mla_attn_kernel,
        name="mla_attn",
        grid=(b, HEADS // ATT_HEADS, s // ATT_Q),
        in_specs=[
            pl.BlockSpec(q_block, lambda i, h, j: (i, j, h, 0)),
            pl.BlockSpec(q_block, lambda i, h, j: (i, jnp.minimum(j + 1, last_q), h, 0)),
            pl.BlockSpec((1, s, ATT_HEADS * MLA_QK), lambda i, h, j: (i, 0, h)),
            pl.BlockSpec((1, n_kb, ATT_HEADS * MLA_DVA, ATT_K), lambda i, h, j: (i, 0, h, 0)),
        ],
        out_specs=pl.BlockSpec((1, ATT_Q, ATT_HEADS * MLA_DV), lambda i, h, j: (i, j, h)),
        out_shape=jax.ShapeDtypeStruct((b, s, HEADS * MLA_DV), BF16),
        scratch_shapes=[
            pltpu.VMEM((ATT_HEADS, ATT_K, ATT_Q), F32),
            pltpu.VMEM((ATT_HEADS, ATT_K, ATT_Q), F32),
            pltpu.VMEM((ATT_HEADS, 1, ATT_Q), F32),
            pltpu.VMEM((ATT_HEADS, 1, ATT_Q), F32),
            pltpu.VMEM((ATT_HEADS, 1, ATT_Q), F32),
            pltpu.VMEM((ATT_HEADS, MLA_DVA, ATT_Q), F32),
        ],
        compiler_params=_params("parallel", "parallel", "arbitrary"),
    )(qcat_t, qcat_t, kcat, vt)


def _merge_ln_kernel(yr_ref, om_ref, h_ref, wg_ref, wro_ref, wmo_ref, wout_ref,
                     g_ref, b_ref, o_ref):
    for rows in _row_chains(h_ref.shape[0]):
        h = h_ref[rows, :]
        hb = h.astype(BF16)
        mix = (jax.nn.sigmoid(_dot_nt(hb, wg_ref[0:D_MODEL, :]))
               * _dot(yr_ref[rows, :], wro_ref[...])
               + jax.nn.sigmoid(_dot_nt(hb, wg_ref[D_MODEL:2 * D_MODEL, :]))
               * _dot(om_ref[rows, :], wmo_ref[...]))
        mixed = _dot(mix.astype(BF16), wout_ref[...])
        o_ref[rows, :] = _layer_norm(ALPHA * h + mixed, g_ref[...], b_ref[...])


def _merge_ln(yr, om, h, wg, wro, wmo, wout, g, b, riders):
    t = h.shape[0]
    cast = _CastRiders(t // FFN_TILE, riders)
    return pl.pallas_call(
        cast.wrap(_merge_ln_kernel, 9, 1),
        name="merge_ln",
        grid=(t // FFN_TILE,),
        in_specs=[
            _row_spec(FFN_TILE, RET_V),
            _row_spec(FFN_TILE, HEADS * MLA_DV),
            _row_spec(FFN_TILE, D_MODEL),
            _const_spec((2 * D_MODEL, D_MODEL)),
            _const_spec((RET_V, D_MODEL)),
            _const_spec((HEADS * MLA_DV, D_MODEL)),
            _const_spec((D_MODEL, D_MODEL)),
            _const_spec((1, D_MODEL)),
            _const_spec((1, D_MODEL)),
        ] + cast.in_specs,
        out_specs=[_row_spec(FFN_TILE, D_MODEL)] + cast.out_specs,
        out_shape=[jax.ShapeDtypeStruct((t, D_MODEL), F32)] + cast.out_shapes,
        compiler_params=_params("parallel"),
    )(yr, om, h, wg, wro, wmo, wout, g, b, *cast.arrays)


def _retention_tables():
    log_gamma = jnp.log(1.0 - 2.0 ** (-5.0 - jnp.arange(HEADS, dtype=F32)))
    idx = jnp.arange(RET_BLOCK, dtype=F32)
    chunk = jnp.arange(RET_BLOCK) // CHUNK
    visible = chunk[None, :] <= chunk[:, None]
    dist = jnp.abs(idx[:, None] - idx[None, :])
    dmask = jnp.where(visible[None], jnp.exp(log_gamma[:, None, None] * dist[None]), 0.0)
    xi = jnp.exp(log_gamma[:, None] * (idx + 1.0))
    zeta = jnp.exp(log_gamma[:, None] * (RET_BLOCK - 1.0 - idx))
    cdec = jnp.exp(log_gamma * RET_BLOCK)
    def per_token_rows(tab):
        rows = jnp.broadcast_to(tab.T[:, :, None], (RET_BLOCK, HEADS, RET_DK))
        return jnp.tile(rows.reshape(RET_BLOCK, RET_QK), (TOKEN_TILE // RET_BLOCK, 1))

    return (dmask, per_token_rows(xi), per_token_rows(zeta),
            jnp.broadcast_to(cdec[:, None, None], (HEADS, 1, RET_DV)))


def _rope_consts(half):
    invf = ROPE_BASE ** (-jnp.arange(half, dtype=F32) / half)
    invf = jnp.tile(invf, LANES // half).reshape(1, LANES)
    delta = jnp.arange(TOKEN_TILE, dtype=F32)[:, None] * invf
    return invf, jnp.cos(delta), jnp.sin(delta)


def kernel(x, p, positions, ln_g, ln_b, ffn1_w_in, ffn1_w_out, w_in, ret_gn_g, w_ret_o,
           q_norm_g, kv_norm_g, w_uq, w_ukv, w_mla_o, w_out, ffn2_w_in, ffn2_w_out,
           ple_w_gate, ple_w_proj):
    b, s, d = x.shape
    t = b * s
    h = x.reshape(t, d)
    pos_b = positions.astype(F32).reshape(t // TOKEN_TILE, 1, TOKEN_TILE)
    row = lambda v: v.reshape(1, -1)

    for i in range(ln_g.shape[0]):
        w_in_t = jnp.swapaxes(w_in[i], 0, 1)
        c0 = 2 * RET_QK + 2 * RET_V
        c1 = c0 + Q_LORA + KV_LORA + MLA_ROPE
        w_mla = jnp.pad(w_in_t[c0:c1], ((0, LANES - MLA_ROPE), (0, 0))).T.astype(BF16)

        q_scale = (MLA_NOPE + MLA_ROPE) ** -0.5 * math.log2(math.e)
        uq = w_uq[i].reshape(Q_LORA, HEADS, MLA_NOPE + MLA_ROPE) * q_scale
        uq_nope = uq[:, :, :MLA_NOPE].reshape(Q_LORA, HEADS * MLA_NOPE)
        uq_rope = uq[:, :, MLA_NOPE:]
        uq_rot = jnp.concatenate(
            [-uq_rope[:, :, MLA_ROPE // 2:], uq_rope[:, :, :MLA_ROPE // 2]], axis=2)
        wuq = jnp.concatenate(
            [uq_nope, uq_rope.reshape(Q_LORA, -1), uq_rot.reshape(Q_LORA, -1)], axis=1).T.astype(BF16)
        ukv = w_ukv[i].reshape(KV_LORA, HEADS, MLA_NOPE + MLA_DV)
        wuk = ukv[:, :, :MLA_NOPE].reshape(KV_LORA, -1).astype(BF16)
        wuvt = ukv[:, :, MLA_NOPE:].reshape(KV_LORA, -1).T.astype(BF16)

        rope_ret = _rope_consts(RET_DK // 2)
        rope_mla = _rope_consts(MLA_ROPE // 2)

        (h, qcat, kcat, vt,
         w_ret, w_gates, w_ffn2_in, w_ro, w_mo, w_o, w_pg) = _ffn_ln_mla(
            h, pos_b, rope_mla, ffn1_w_in[i].astype(BF16), ffn1_w_out[i].astype(BF16),
            row(ln_g[i, 0]), row(ln_b[i, 0]),
            w_mla, row(q_norm_g[i]), row(kv_norm_g[i]), wuq, wuk, wuvt,
            riders=[(w_in_t, 0, c0, None, 2 * LANES),
                    (w_in_t, c1, 2 * D_MODEL, None, None),
                    (ffn2_w_in[i], 0, D_MODEL, None, None),
                    (w_ret_o[i], 0, RET_V, ret_gn_g[i], None),
                    (w_mla_o[i], 0, HEADS * MLA_DV, None, None),
                    (w_out[i], 0, D_MODEL, None, None),
                    (ple_w_gate[i], 0, D_MODEL, None, None)])

        dmask, xi, zeta, cdec = _retention_tables()
        y_ret = _retention(h, pos_b, rope_ret, xi, zeta, dmask, cdec, w_ret, s)
        o_mla = _mla_attn(qcat.reshape(b, s // ATT_Q, HEADS * MLA_QK, ATT_Q),
                          kcat.reshape(b, s, -1),
                          vt.reshape(b, s // ATT_K, HEADS * MLA_DVA, ATT_K))
        h, w_ffn2_out, w_pp = _merge_ln(
            y_ret, o_mla.reshape(t, -1), h, w_gates, w_ro, w_mo, w_o,
            row(ln_g[i, 1]), row(ln_b[i, 1]),
            riders=[(ffn2_w_out[i], 0, D_FF, None, None), (ple_w_proj[i], 0, D_PLE, None, None)])

        h = _ffn_ple_ln(h, p[i].reshape(t, -1), w_ffn2_in, w_ffn2_out, w_pg, w_pp,
                        row(ln_g[i, 2]), row(ln_b[i, 2]), row(ln_g[i, 3]), row(ln_b[i, 3]))
    return h.reshape(b, s, d)
```
